```python
import math
import jax
import jax.numpy as jnp
from jax import lax
import numpy as np

D_MODEL = 1024
BATCH = 1
SEQ = 16384
DEPTH = 4

CTX_LEN = 256
GRID_W = 64
EPS = 1e-6

DA_HEADS = 4
DA_DH = 64
DA_DV = 2 * DA_DH
DA_QK_W = DA_HEADS * 2 * DA_DH
DA_W = DA_HEADS * DA_DV
DA_SCALE = DA_DH ** -0.5
Q_BLOCK = 128
ROPE_PAIRS = DA_DH // 4
ROPE_BASE = 10000.0

HG_HEADS = 4
HG_DK = 128
HG_DV = 128
HG_KW = HG_HEADS * HG_DK
HG_W = HG_HEADS * HG_DV
HG_CHUNK = 64

N_EXPERTS = 32
TOP_K = 4
D_FF = 1024
SWIGLU_ALPHA = 1.702
SWIGLU_LIMIT = 7.0
MOE_BLOCK = 256

IN_WIDTHS = (DA_QK_W, DA_QK_W, DA_W, HG_KW, HG_KW, HG_KW, HG_W, HG_W, D_MODEL, D_MODEL)
IN_COLS = 2 * DA_QK_W + DA_W + 3 * HG_KW + 2 * HG_W + 2 * D_MODEL

kernel_name = 'hybrid_diffattn_hgrn2_moe_dit'


def _rmsnorm(x, g):
    x32 = x.astype(jnp.float32)
    y = x32 * lax.rsqrt(jnp.mean(x32 * x32, axis=-1, keepdims=True) + EPS)
    return y.astype(x.dtype) * g


def _split_in(p):
    parts, o = [], 0
    for w in IN_WIDTHS:
        parts.append(p[..., o:o + w])
        o += w
    return parts


def _rotate_half(x, ang):
    x1, x2 = jnp.split(x, 2, axis=-1)
    cos, sin = jnp.cos(ang), jnp.sin(ang)
    return jnp.concatenate([x1 * cos - x2 * sin, x2 * cos + x1 * sin], axis=-1)


def _rope_2d(x, row_ang, col_ang):
    x32 = x.astype(jnp.float32)
    xr, xc = jnp.split(x32, 2, axis=-1)
    out = jnp.concatenate([_rotate_half(xr, row_ang), _rotate_half(xc, col_ang)], axis=-1)
    return out.astype(x.dtype)


def _diff_attn(qb, k, v, lam):
    s = jnp.einsum('bqhmd,bkhmd->bhmqk', qb, k).astype(jnp.float32) * DA_SCALE
    p = jax.nn.softmax(s, axis=-1)
    pd = p[:, :, 0] - lam * p[:, :, 1]
    return jnp.einsum('bhqk,bkhv->bqhv', pd.astype(v.dtype), v)


def _forget(z, lb):
    z32 = z.astype(jnp.float32)
    lb = lb.reshape(HG_HEADS, HG_DK)
    log_f = jnp.logaddexp(jnp.log(lb), jnp.log1p(-lb) + jax.nn.log_sigmoid(z32))
    k = (1.0 - lb) * jax.nn.sigmoid(-z32)
    return log_f, k


def _chunk_scan(q, k, v, log_f, s0):
    B, L = q.shape[:2]
    n = L // HG_CHUNK
    def chunks(t):
        return jnp.moveaxis(t.astype(jnp.float32).reshape(B, n, HG_CHUNK, HG_HEADS, t.shape[-1]), 1, 0)
    mask = jnp.tril(jnp.ones((HG_CHUNK, HG_CHUNK), bool))[None, :, :, None, None]
    def step(S, inp):
        qc, kc, vc, gc = inp
        b = jnp.cumsum(gc, axis=1)
        o_inter = jnp.einsum('bthd,bhdv->bthv', qc * jnp.exp(b), S)
        diff = b[:, :, None] - b[:, None, :]
        decay = jnp.exp(jnp.where(mask, diff, -jnp.inf))
        A = jnp.einsum('bthd,btshd,bshd->bhts', qc, decay, kc)
        o_intra = jnp.einsum('bhts,bshv->bthv', A, vc)
        b_last = b[:, -1]
        S_new = jnp.exp(b_last)[..., None] * S + jnp.einsum('bshd,bshv->bhdv', kc * jnp.exp(b_last[:, None] - b), vc)
        return S_new, o_inter + o_intra
    S_fin, o = lax.scan(step, s0, (chunks(q), chunks(k), chunks(v), chunks(log_f)))
    o = jnp.moveaxis(o, 0, 1).reshape(B, L, HG_HEADS, HG_DV)
    return o.astype(v.dtype), S_fin


def _hgrn2_direction(q, z, v, lb, s0, reverse):
    log_f, k = _forget(z, lb)
    if reverse:
        q, k, v, log_f = (jnp.flip(t, axis=1) for t in (q, k, v, log_f))
    o, s = _chunk_scan(q, k, v, log_f, s0)
    if reverse:
        o = jnp.flip(o, axis=1)
    return o, s


def _token_mixer(h, hc, row_ang, col_ang, w_in, qn_g, kn_g, lam_qk, subln_g,
                 lb_f, lb_b, hg_norm_g, w_branch_a, w_branch_b, w_out, layer, with_ctx):
    B, S, _ = h.shape
    Lc = hc.shape[1]
    qa, ka, va, qh, zf, zb, vh, gh, ga, gb = _split_in(h @ w_in)
    qac, kac, vac, qhc, zfc, zbc, vhc, ghc, gac, gbc = _split_in(hc @ w_in)

    lam_init = 0.8 - 0.6 * math.exp(-0.3 * layer)
    lq1, lk1, lq2, lk2 = (lam_qk[i].astype(jnp.float32) for i in range(4))
    lam = jnp.exp(jnp.sum(lq1 * lk1)) - jnp.exp(jnp.sum(lq2 * lk2)) + lam_init
    def qk(t, L, g):
        return _rmsnorm(t.reshape(B, L, DA_HEADS, 2, DA_DH), g)
    q = _rope_2d(qk(qa, S, qn_g), row_ang, col_ang)
    k = _rope_2d(qk(ka, S, kn_g), row_ang, col_ang)
    kc = qk(kac, Lc, kn_g)
    vc = vac.reshape(B, Lc, DA_HEADS, DA_DV)
    k_all = jnp.concatenate([kc, k], axis=1)
    v_all = jnp.concatenate([vc, va.reshape(B, S, DA_HEADS, DA_DV)], axis=1)
    nb = S // Q_BLOCK
    q_blocks = jnp.moveaxis(q.reshape(B, nb, Q_BLOCK, DA_HEADS, 2, DA_DH), 1, 0)
    oa = lax.map(lambda qb: _diff_attn(qb, k_all, v_all, lam), q_blocks)
    oa = jnp.moveaxis(oa, 0, 1).reshape(B, S, DA_HEADS, DA_DV)
    oa = (_rmsnorm(oa, subln_g) * (1.0 - lam_init)).reshape(B, S, DA_W)

    def hd(t, L, d):
        return t.reshape(B, L, HG_HEADS, d)
    s0 = jnp.zeros((B, HG_HEADS, HG_DK, HG_DV), jnp.float32)
    ob, obc = 0.0, 0.0
    for z, zc, lb, rev in ((zf, zfc, lb_f, False), (zb, zbc, lb_b, True)):
        oc_d, sc_d = _hgrn2_direction(hd(qhc, Lc, HG_DK), hd(zc, Lc, HG_DK), hd(vhc, Lc, HG_DV), lb, s0, rev)
        o_d, _ = _hgrn2_direction(hd(qh, S, HG_DK), hd(z, S, HG_DK), hd(vh, S, HG_DV), lb, sc_d, rev)
        ob = ob + o_d
        obc = obc + oc_d
    ob = (_rmsnorm(ob, hg_norm_g) * jax.nn.silu(hd(gh, S, HG_DV))).reshape(B, S, HG_W)

    y = (jax.nn.sigmoid(ga) * (oa @ w_branch_a) + jax.nn.sigmoid(gb) * (ob @ w_branch_b)) @ w_out
    if not with_ctx:
        return y, None
    qc = qk(qac, Lc, qn_g)
    oac = _diff_attn(qc, kc, vc, lam)
    oac = (_rmsnorm(oac, subln_g) * (1.0 - lam_init)).reshape(B, Lc, DA_W)
    obc = (_rmsnorm(obc, hg_norm_g) * jax.nn.silu(hd(ghc, Lc, HG_DV))).reshape(B, Lc, HG_W)
    yc = (jax.nn.sigmoid(gac) * (oac @ w_branch_a) + jax.nn.sigmoid(gbc) * (obc @ w_branch_b)) @ w_out
    return y, yc


def _moe(tok, router_w, router_b, w_gu, b_gu, w_down, b_down):
    T, D = tok.shape
    logits = (tok @ router_w + router_b).astype(jnp.float32)
    top_v, top_i = lax.top_k(logits, TOP_K)
    top_w = jax.nn.softmax(top_v, axis=-1)
    n_assign = T * TOP_K
    flat_e = top_i.reshape(-1)
    flat_t = jnp.repeat(jnp.arange(T, dtype=jnp.int32), TOP_K)
    flat_w = top_w.reshape(-1)
    order = jnp.argsort(flat_e, stable=True)
    e_s, t_s, w_s = flat_e[order], flat_t[order], flat_w[order]
    counts = jnp.bincount(flat_e, length=N_EXPERTS)
    starts = jnp.cumsum(counts) - counts
    padded = (counts + MOE_BLOCK - 1) // MOE_BLOCK * MOE_BLOCK
    pad_ends = jnp.cumsum(padded)
    pad_starts = pad_ends - padded
    dest = pad_starts[e_s] + (jnp.arange(n_assign) - starts[e_s])
    n_rows = -(-n_assign // MOE_BLOCK) * MOE_BLOCK + N_EXPERTS * MOE_BLOCK
    n_blocks = n_rows // MOE_BLOCK
    row_tok = jnp.full((n_rows,), T, jnp.int32).at[dest].set(t_s)
    row_w = jnp.zeros((n_rows,), tok.dtype).at[dest].set(w_s.astype(tok.dtype))
    blk_start = jnp.arange(n_blocks) * MOE_BLOCK
    blk_e = jnp.minimum(jnp.sum(blk_start[:, None] >= pad_ends[None, :], axis=1), N_EXPERTS - 1)
    xs = jnp.concatenate([tok, jnp.zeros((1, D), tok.dtype)], axis=0)[row_tok].reshape(n_blocks, MOE_BLOCK, D)
    def expert_block(args):
        xb, e = args
        gate, up = jnp.split(xb @ w_gu[e] + b_gu[e], 2, axis=-1)
        gate = jnp.minimum(gate, SWIGLU_LIMIT)
        up = jnp.clip(up, -SWIGLU_LIMIT, SWIGLU_LIMIT)
        hdn = (up + 1.0) * gate * jax.nn.sigmoid(SWIGLU_ALPHA * gate)
        return hdn @ w_down[e] + b_down[e]
    ys = lax.map(expert_block, (xs, blk_e)).reshape(n_rows, D)
    out = jnp.zeros((T + 1, D), tok.dtype).at[row_tok].add(row_w[:, None] * ys)
    return out[:T]


def setup_inputs(seed: int = 0) -> dict:
    key = jax.random.key(seed)
    ks = jax.random.split(key, 24)
    f32 = jnp.float32
    D = D_MODEL
    def nrm(k, shape, scale):
        return jax.random.normal(k, shape, f32) * scale
    return {
        'x': nrm(ks[0], (BATCH, SEQ, D), 1.0),
        'c': nrm(ks[1], (BATCH, D), 1.0),
        'ctx': nrm(ks[2], (BATCH, CTX_LEN, D), 1.0),
        'c_ctx': nrm(ks[3], (D,), 1.0),
        'ada_w': nrm(ks[4], (DEPTH, D, 6 * D), 0.5 * D ** -0.5),
        'ada_b': nrm(ks[5], (DEPTH, 6 * D), 0.02),
        'norm1_g': 1.0 + nrm(ks[6], (DEPTH, D), 0.02),
        'norm2_g': 1.0 + nrm(ks[7], (DEPTH, D), 0.02),
        'w_in': nrm(ks[8], (DEPTH, D, IN_COLS), D ** -0.5),
        'qn_g': 1.0 + nrm(ks[9], (DEPTH, DA_DH), 0.02),
        'kn_g': 1.0 + nrm(ks[10], (DEPTH, DA_DH), 0.02),
        'lam_qk': nrm(ks[11], (DEPTH, 4, DA_DH), 0.1),
        'subln_g': 1.0 + nrm(ks[12], (DEPTH, DA_DV), 0.02),
        'hg_lb': nrm(ks[13], (2, DEPTH, HG_KW), 0.1),
        'hg_norm_g': 1.0 + nrm(ks[14], (DEPTH, HG_DV), 0.02),
        'w_branch_a': nrm(ks[15], (DEPTH, DA_W, D), DA_W ** -0.5),
        'w_branch_b': nrm(ks[16], (DEPTH, HG_W, D), HG_W ** -0.5),
        'w_out': nrm(ks[17], (DEPTH, D, D), D ** -0.5),
        'router_w': nrm(ks[18], (DEPTH, D, N_EXPERTS), D ** -0.5),
        'router_b': nrm(ks[19], (DEPTH, N_EXPERTS), 0.01),
        'w_gu': nrm(ks[20], (DEPTH, N_EXPERTS, D, 2 * D_FF), D ** -0.5),
        'b_gu': nrm(ks[21], (DEPTH, N_EXPERTS, 2 * D_FF), 0.01),
        'w_down': nrm(ks[22], (DEPTH, N_EXPERTS, D_FF, D), D_FF ** -0.5),
        'b_down': nrm(ks[23], (DEPTH, N_EXPERTS, D), 0.01),
    }


def reference(x, c, ctx, c_ctx, ada_w, ada_b, norm1_g, norm2_g, w_in, qn_g, kn_g, lam_qk,
              subln_g, hg_lb, hg_norm_g, w_branch_a, w_branch_b, w_out, router_w, router_b,
              w_gu, b_gu, w_down, b_down):
    B, S, D = x.shape
    Lc = ctx.shape[1]
    rows_n = S // GRID_W
    row = jnp.repeat(jnp.arange(rows_n), GRID_W).astype(jnp.float32)
    col = jnp.tile(jnp.arange(GRID_W), rows_n).astype(jnp.float32)
    freq = ROPE_BASE ** (-jnp.arange(ROPE_PAIRS, dtype=jnp.float32) / ROPE_PAIRS)
    row_ang = (row[:, None] * freq)[:, None, None, :]
    col_ang = (col[:, None] * freq)[:, None, None, :]
    cs = jnp.cumsum(jax.nn.softmax(hg_lb.astype(jnp.float32), axis=1), axis=1)
    lb_all = cs - cs[:, :1]
    for l in range(DEPTH):
        last = l == DEPTH - 1
        mod = (jax.nn.silu(c) @ ada_w[l] + ada_b[l])[:, None, :]
        mod_c = jax.nn.silu(c_ctx) @ ada_w[l] + ada_b[l]
        sh1, sc1, g1, sh2, sc2, g2 = jnp.split(mod, 6, axis=-1)
        sh1c, sc1c, g1c, sh2c, sc2c, g2c = jnp.split(mod_c, 6, axis=-1)
        h = _rmsnorm(x, norm1_g[l]) * (1.0 + sc1) + sh1
        hc = _rmsnorm(ctx, norm1_g[l]) * (1.0 + sc1c) + sh1c
        y, yc = _token_mixer(h, hc, row_ang, col_ang, w_in[l], qn_g[l], kn_g[l], lam_qk[l], subln_g[l],
                             lb_all[0, l], lb_all[1, l], hg_norm_g[l], w_branch_a[l], w_branch_b[l],
                             w_out[l], l, not last)
        x = x + g1 * y
        h2 = _rmsnorm(x, norm2_g[l]) * (1.0 + sc2) + sh2
        tok = h2.reshape(B * S, D)
        if not last:
            ctx = ctx + g1c * yc
            h2c = _rmsnorm(ctx, norm2_g[l]) * (1.0 + sc2c) + sh2c
            tok = jnp.concatenate([tok, h2c.reshape(B * Lc, D)], axis=0)
        m = _moe(tok, router_w[l], router_b[l], w_gu[l], b_gu[l], w_down[l], b_down[l])
        x = x + g2 * m[:B * S].reshape(B, S, D)
        if not last:
            ctx = ctx + g2c * m[B * S:].reshape(B, Lc, D)
    return x
```

```python
import functools
import math

import jax
import jax.numpy as jnp
from jax import lax
from jax.experimental import pallas as pl
from jax.experimental.pallas import tpu as pltpu

F32 = jnp.float32
BF16 = jnp.bfloat16

D_MODEL = 1024
GRID_W = 64
EPS = 1e-6

DA_HEADS = 4
DA_DH = 64
DA_DV = 2 * DA_DH
DA_W = DA_HEADS * DA_DV
DA_SCALE = DA_DH ** -0.5
ROPE_PAIRS = DA_DH // 4
ROPE_BASE = 10000.0

HG_HEADS = 4
HG_DK = 128
HG_W = HG_HEADS * HG_DK
HG_CHUNK = 64
HG_SUB = 16

N_EXPERTS = 32
TOP_K = 4
D_FF = 1024
SWIGLU_ALPHA = 1.702
SWIGLU_LIMIT = 7.0
MOE_BLOCK = 256

IN_COLS = 6144
ROW_TILE = 256
VT_ROWS = DA_DV + 16
NEG_BIG = -1e30
LOG2E = 1.4426950408889634

VMEM_LIMIT = 56 * 1024 * 1024


def _cparams(sem):
    return pltpu.CompilerParams(dimension_semantics=sem, vmem_limit_bytes=VMEM_LIMIT)


def _split3(x):
    a = x.astype(BF16)
    r = x - a.astype(F32)
    b = r.astype(BF16)
    c = (r - b.astype(F32)).astype(BF16)
    return a, b, c


def _dot(a, b):
    return jnp.dot(a, b, preferred_element_type=F32)


def _dot_nt(a, b):
    return lax.dot_general(a, b, (((1,), (1,)), ((), ())), preferred_element_type=F32)


def _dot_tn(a, b):
    return lax.dot_general(a, b, (((0,), (0,)), ((), ())), preferred_element_type=F32)


def _sigmoid(x):
    return 1.0 / (1.0 + jnp.exp(-x))


def _mod_kernel(c_ref, w_ref, b_ref, o_ref):
    cv = c_ref[...]
    a = cv * _sigmoid(cv)
    w = w_ref[0]
    a1, a2, a3 = _split3(a)
    w1, w2, w3 = _split3(w)
    acc = _dot(a1, w1) + (_dot(a1, w2) + _dot(a2, w1)) + (_dot(a2, w2) + _dot(a1, w3) + _dot(a3, w1))
    o_ref[0] = acc + b_ref[0]


def _modulation(cc, ada_w, ada_b):
    depth, d, n = ada_w.shape
    tn = 1536
    return pl.pallas_call(
        _mod_kernel,
        grid=(depth, n // tn),
        in_specs=[
            pl.BlockSpec((8, d), lambda l, j: (0, 0)),
            pl.BlockSpec((1, d, tn), lambda l, j: (l, 0, j)),
            pl.BlockSpec((1, 1, tn), lambda l, j: (l, 0, j)),
        ],
        out_specs=pl.BlockSpec((1, 8, tn), lambda l, j: (l, 0, j)),
        out_shape=jax.ShapeDtypeStruct((depth, 8, n), F32),
        compiler_params=_cparams(("parallel", "parallel")),
        name="adaln_mod",
    )(cc, ada_w, ada_b.reshape(depth, 1, n))


def _inproj_kernel(x_ref, g_ref, mod_ref, w_ref, o_ref, h_scr, *, n_lat, tm):
    i = pl.program_id(0)
    j = pl.program_id(1)
    d = x_ref.shape[1]

    @pl.when(j == 0)
    def _():
        x = x_ref[...]
        y = x * lax.rsqrt(jnp.mean(x * x, axis=-1, keepdims=True) + EPS) * g_ref[...]
        row = i * tm + lax.broadcasted_iota(jnp.int32, (tm, 1), 0)
        lat = row < n_lat
        sh = jnp.where(lat, mod_ref[0:1, 0:d], mod_ref[1:2, 0:d])
        sc = jnp.where(lat, mod_ref[0:1, d:2 * d], mod_ref[1:2, d:2 * d])
        h_scr[...] = (y * (1.0 + sc) + sh).astype(BF16)

    o_ref[...] = _dot(h_scr[...], w_ref[...])


def _inproj(x, g, mod, w_bf, n_lat):
    t, d = x.shape
    n = w_bf.shape[1]
    tm = 1280 if t % 1280 == 0 else ROW_TILE
    tn = 1536
    return pl.pallas_call(
        functools.partial(_inproj_kernel, n_lat=n_lat, tm=tm),
        grid=(t // tm, n // tn),
        in_specs=[
            pl.BlockSpec((tm, d), lambda i, j: (i, 0)),
            pl.BlockSpec((1, d), lambda i, j: (0, 0)),
            pl.BlockSpec((8, mod.shape[1]), lambda i, j: (0, 0)),
            pl.BlockSpec((d, tn), lambda i, j: (0, j)),
        ],
        out_specs=pl.BlockSpec((tm, tn), lambda i, j: (i, j)),
        out_shape=jax.ShapeDtypeStruct((t, n), F32),
        scratch_shapes=[pltpu.VMEM((tm, d), BF16)],
        compiler_params=_cparams(("parallel", "arbitrary")),
        name="inproj",
    )(x, g, mod, w_bf)


def _segment_mean_sq(x, seg_ref, width):
    x2 = x * x
    hi = x2.astype(BF16)
    lo = (x2 - hi.astype(F32)).astype(BF16)
    seg = seg_ref[...]
    return (_dot(hi, seg) + _dot(lo, seg)) * (1.0 / width)


def _prep_kernel(q_ref, k_ref, v_ref, cos_ref, sin_ref, seg_ref, qg_ref, kg_ref,
                 k_out, qt_out, vt_out):
    tm = q_ref.shape[0]
    cos = cos_ref[...]
    sin = sin_ref[...]
    lane = lax.broadcasted_iota(jnp.int32, (tm, DA_W), 1)
    first_half = (lane & 31) < 16

    def norm_rope(x, g):
        y = x * lax.rsqrt(_segment_mean_sq(x, seg_ref, DA_DH) + EPS) * g
        fwd = pltpu.roll(y, DA_W - 16, 1)
        bwd = pltpu.roll(y, 16, 1)
        partner = jnp.where(first_half, fwd, bwd)
        return y * cos + partner * sin

    k_out[...] = norm_rope(k_ref[...], kg_ref[...]).astype(BF16)

    q = norm_rope(q_ref[...], qg_ref[...]) * (DA_SCALE * LOG2E)
    lane_h = lax.broadcasted_iota(jnp.int32, (tm, DA_DV), 1)
    v = v_ref[...]
    ones = jnp.ones((VT_ROWS - DA_DV, tm), BF16)
    for h in range(DA_HEADS):
        qh = q[:, h * DA_DV:(h + 1) * DA_DV]
        for m in range(2):
            keep = (lane_h < DA_DH) if m == 0 else (lane_h >= DA_DH)
            r = (2 * h + m) * DA_DV
            qt_out[r:r + DA_DV, :] = jnp.where(keep, qh, 0.0).T.astype(BF16)
        vt_out[h * VT_ROWS:h * VT_ROWS + DA_DV, :] = v[:, h * DA_DV:(h + 1) * DA_DV].T.astype(BF16)
        vt_out[h * VT_ROWS + DA_DV:(h + 1) * VT_ROWS, :] = ones


def _prep(p, cos_t, sin_t, seg64, qg, kg):
    t = p.shape[0]
    tm = ROW_TILE
    return pl.pallas_call(
        _prep_kernel,
        grid=(t // tm,),
        in_specs=[
            pl.BlockSpec((tm, DA_W), lambda i: (i, 0)),
            pl.BlockSpec((tm, DA_W), lambda i: (i, 1)),
            pl.BlockSpec((tm, DA_W), lambda i: (i, 2)),
            pl.BlockSpec((tm, DA_W), lambda i: (i, 0)),
            pl.BlockSpec((tm, DA_W), lambda i: (i, 0)),
            pl.BlockSpec((DA_W, DA_W), lambda i: (0, 0)),
            pl.BlockSpec((1, DA_W), lambda i: (0, 0)),
            pl.BlockSpec((1, DA_W), lambda i: (0, 0)),
        ],
        out_specs=[
            pl.BlockSpec((tm, DA_W), lambda i: (i, 0)),
            pl.BlockSpec((2 * DA_HEADS * DA_DV, tm), lambda i: (0, i)),
            pl.BlockSpec((DA_HEADS * VT_ROWS, tm), lambda i: (0, i)),
        ],
        out_shape=[
            jax.ShapeDtypeStruct((t, DA_W), BF16),
            jax.ShapeDtypeStruct((2 * DA_HEADS * DA_DV, t), BF16),
            jax.ShapeDtypeStruct((DA_HEADS * VT_ROWS, t), BF16),
        ],
        compiler_params=_cparams(("parallel",)),
        name="qkv_prep",
    )(p, p, p, cos_t, sin_t, seg64, qg, kg)


def _attn_kernel(qt_ref, k_ref, vt_ref, lam_ref, sg_ref, o_ref, m_scr, acc_scr, *, lam_init):
    kk = pl.program_id(1)
    nk = pl.num_programs(1)

    @pl.when(kk == 0)
    def _():
        m_scr[...] = jnp.full(m_scr.shape, NEG_BIG, F32)
        acc_scr[...] = jnp.zeros(acc_scr.shape, F32)

    for h in range(DA_HEADS):
        k_h = k_ref[:, h * DA_DV:(h + 1) * DA_DV]
        vt_h = vt_ref[h * VT_ROWS:(h + 1) * VT_ROWS, :]
        for m in range(2):
            r = 2 * h + m
            s = _dot(k_h, qt_ref[r * DA_DV:(r + 1) * DA_DV, :])
            m_prev = m_scr[r:r + 1, :]
            m_new = jnp.maximum(m_prev, jnp.max(s, axis=0, keepdims=True))
            alpha = jnp.exp2(m_prev - m_new)
            p = jnp.exp2((s - m_new).astype(BF16))
            acc_scr[r] = alpha * acc_scr[r] + _dot(vt_h, p)
            m_scr[r:r + 1, :] = m_new

    @pl.when(kk == nk - 1)
    def _():
        lq = lam_ref[...]
        lam = (jnp.exp(jnp.sum(lq[0:1] * lq[1:2], axis=-1, keepdims=True))
               - jnp.exp(jnp.sum(lq[2:3] * lq[3:4], axis=-1, keepdims=True)) + lam_init)
        for h in range(DA_HEADS):
            a1 = acc_scr[2 * h]
            a2 = acc_scr[2 * h + 1]
            o = a1[0:DA_DV] / a1[DA_DV:DA_DV + 1] - lam * (a2[0:DA_DV] / a2[DA_DV:DA_DV + 1])
            o = o * lax.rsqrt(jnp.mean(o * o, axis=0, keepdims=True) + EPS)
            o = o * sg_ref[...] * (1.0 - lam_init)
            o_ref[:, h * DA_DV:(h + 1) * DA_DV] = o.T.astype(o_ref.dtype)


def _attention(qt, k, vt, lam_qk, subln_col, lam_init, *, q_start, n_q, k_start, n_k, tq, tk):
    qo, ko = q_start // tq, k_start // tk
    return pl.pallas_call(
        functools.partial(_attn_kernel, lam_init=lam_init),
        grid=(n_q // tq, n_k // tk),
        in_specs=[
            pl.BlockSpec((qt.shape[0], tq), lambda i, j: (0, i + qo)),
            pl.BlockSpec((tk, DA_W), lambda i, j: (j + ko, 0)),
            pl.BlockSpec((vt.shape[0], tk), lambda i, j: (0, j + ko)),
            pl.BlockSpec((4, DA_DH), lambda i, j: (0, 0)),
            pl.BlockSpec((DA_DV, 1), lambda i, j: (0, 0)),
        ],
        out_specs=pl.BlockSpec((tq, DA_W), lambda i, j: (i, 0)),
        out_shape=jax.ShapeDtypeStruct((n_q, DA_W), BF16),
        scratch_shapes=[
            pltpu.VMEM((2 * DA_HEADS, tq), F32),
            pltpu.VMEM((2 * DA_HEADS, VT_ROWS, tq), F32),
        ],
        compiler_params=_cparams(("parallel", "arbitrary")),
        name="diff_attn",
    )(qt, k, vt, lam_qk, subln_col)


def _hg_chunk(q, z, v, lb, st_ref, b_scr, k_scr, rev):
    c, w = q.shape
    nsub = c // HG_SUB
    row = lax.broadcasted_iota(jnp.int32, (c, 1), 0)

    ls = jnp.minimum(z, 0.0) - jnp.log1p(jnp.exp(-jnp.abs(z)))
    y = jnp.log1p(-lb) + ls
    log_lb = jnp.log(lb)
    log_f = jnp.maximum(log_lb, y) + jnp.log1p(jnp.exp(-jnp.abs(log_lb - y)))
    key = (1.0 - lb) / (1.0 + jnp.exp(z))

    ri = lax.broadcasted_iota(jnp.int32, (c, c), 0)
    ci = lax.broadcasted_iota(jnp.int32, (c, c), 1)
    tri = jnp.where((ci >= ri) if rev else (ci <= ri), 1.0, 0.0).astype(BF16)
    g1, g2, g3 = _split3(log_f)
    b = _dot(tri, g1) + _dot(tri, g2) + _dot(tri, g3)
    end = 0 if rev else c - 1
    b_end = b[end:end + 1, :]

    b_scr[...] = b
    k_scr[...] = key

    q_in = (q * jnp.exp(b)).astype(BF16)
    k_out = (key * jnp.exp(b_end - b)).astype(BF16)
    decay = jnp.exp(b_end)

    q_parts, k_parts = [], []
    for blk in range(nsub):
        if (blk == nsub - 1) if rev else (blk == 0):
            continue
        lo, hi = blk * HG_SUB, (blk + 1) * HG_SUB
        ref_row = hi if rev else lo - 1
        r = b[ref_row:ref_row + 1, :]
        in_blk = (row >= lo) & (row < hi)
        earlier = (row >= hi) if rev else (row < lo)
        q_parts.append((q * jnp.exp(jnp.where(in_blk, b - r, NEG_BIG))).astype(BF16))
        k_parts.append((key * jnp.exp(jnp.where(earlier, r - b, NEG_BIG))).astype(BF16))

    sub_row = row & (HG_SUB - 1)
    blk_base = row - sub_row
    col = lax.broadcasted_iota(jnp.int32, (c, c), 1)

    def pair_step(s, a_diag):
        b_s = jnp.concatenate(
            [jnp.broadcast_to(b_scr[pl.ds(blk * HG_SUB + s, 1), :], (HG_SUB, w)) for blk in range(nsub)], axis=0)
        k_s = jnp.concatenate(
            [jnp.broadcast_to(k_scr[pl.ds(blk * HG_SUB + s, 1), :], (HG_SUB, w)) for blk in range(nsub)], axis=0)
        valid = (sub_row <= s) if rev else (sub_row >= s)
        e = jnp.exp(jnp.where(valid, b - b_s, NEG_BIG))
        pr = q * e * k_s
        hit = col == (blk_base + s)
        out = []
        for h in range(HG_HEADS):
            cs = jnp.sum(pr[:, h * HG_DK:(h + 1) * HG_DK], axis=-1, keepdims=True)
            out.append(jnp.where(hit, cs, a_diag[h]))
        return tuple(out)

    a_diag = lax.fori_loop(0, HG_SUB, pair_step,
                           tuple(jnp.zeros((c, c), F32) for _ in range(HG_HEADS)))

    outs = []
    for h in range(HG_HEADS):
        sl = slice(h * HG_DK, (h + 1) * HG_DK)
        a = a_diag[h] + _dot_nt(jnp.concatenate([t[:, sl] for t in q_parts], axis=1),
                                jnp.concatenate([t[:, sl] for t in k_parts], axis=1))
        st = st_ref[h]
        v_h = v[:, sl].astype(BF16)
        outs.append(_dot_nt(q_in[:, sl], st.astype(BF16)) + _dot(a.astype(BF16), v_h))
        st_ref[h] = st * decay[:, sl] + _dot_tn(v_h, k_out[:, sl])
    return jnp.concatenate(outs, axis=1)


def _hgrn_kernel(qf_ref, zf_ref, vf_ref, qb_ref, zb_ref, vb_ref, lbf_ref, lbb_ref,
                 of_ref, ob_ref, sf_scr, sb_scr, bf_scr, kf_scr, bb_scr, kb_scr):
    @pl.when(pl.program_id(0) == 0)
    def _():
        sf_scr[...] = jnp.zeros(sf_scr.shape, F32)
        sb_scr[...] = jnp.zeros(sb_scr.shape, F32)

    n_chunks = qf_ref.shape[0] // HG_CHUNK

    def body(ci, carry):
        rf = pl.ds(pl.multiple_of(ci * HG_CHUNK, HG_CHUNK), HG_CHUNK)
        of_ref[rf, :] = _hg_chunk(qf_ref[rf, :], zf_ref[rf, :], vf_ref[rf, :], lbf_ref[...],
                                  sf_scr, bf_scr, kf_scr, False)
        rb = pl.ds(pl.multiple_of((n_chunks - 1 - ci) * HG_CHUNK, HG_CHUNK), HG_CHUNK)
        ob_ref[rb, :] = _hg_chunk(qb_ref[rb, :], zb_ref[rb, :], vb_ref[rb, :], lbb_ref[...],
                                  sb_scr, bb_scr, kb_scr, True)
        return carry

    lax.fori_loop(0, n_chunks, body, 0)


def _hgrn(p, lb_f, lb_b, n_lat):
    t = p.shape[0]
    tm = ROW_TILE
    nb = t // tm
    n_lat_b = n_lat // tm
    n_ctx_b = nb - n_lat_b

    def jf(i):
        return jnp.where(i < n_ctx_b, n_lat_b + i, i - n_ctx_b)

    def jb(i):
        return jnp.where(i < n_ctx_b, nb - 1 - i, nb - 1 - i)

    def spec(order, colblk):
        return pl.BlockSpec((tm, HG_W), lambda i: (order(i), colblk))

    return pl.pallas_call(
        _hgrn_kernel,
        grid=(nb,),
        in_specs=[spec(jf, 3), spec(jf, 4), spec(jf, 6), spec(jb, 3), spec(jb, 5), spec(jb, 6),
                  pl.BlockSpec((1, HG_W), lambda i: (0, 0)), pl.BlockSpec((1, HG_W), lambda i: (0, 0))],
        out_specs=[pl.BlockSpec((tm, HG_W), lambda i: (jf(i), 0)),
                   pl.BlockSpec((tm, HG_W), lambda i: (jb(i), 0))],
        out_shape=[jax.ShapeDtypeStruct((t, HG_W), F32), jax.ShapeDtypeStruct((t, HG_W), F32)],
        scratch_shapes=[pltpu.VMEM((HG_HEADS, HG_DK, HG_DK), F32), pltpu.VMEM((HG_HEADS, HG_DK, HG_DK), F32),
                        pltpu.VMEM((HG_CHUNK, HG_W), F32), pltpu.VMEM((HG_CHUNK, HG_W), F32),
                        pltpu.VMEM((HG_CHUNK, HG_W), F32), pltpu.VMEM((HG_CHUNK, HG_W), F32)],
        compiler_params=_cparams(("arbitrary",)),
        name="hgrn2",
    )(p, p, p, p, p, p, lb_f, lb_b)


def _post_kernel(x_ref, oa_ref, of_ref, ob_ref, gh_ref, ga_ref, gb_ref, mod_ref, hg_ref, n2_ref,
                 wa_ref, wb_ref, wo_ref, rw_ref, rb_ref,
                 x1_ref, h2_ref, gw_ref, gi_ref, rk_ref, cnt_ref, run_scr, *, n_lat, tm):
    i = pl.program_id(0)
    d = x_ref.shape[1]

    @pl.when(i == 0)
    def _():
        run_scr[...] = jnp.zeros(run_scr.shape, F32)

    lat = i * tm < n_lat

    def mod(c):
        return jnp.where(lat, mod_ref[0:1, c * d:(c + 1) * d], mod_ref[1:2, c * d:(c + 1) * d])

    o = of_ref[...] + ob_ref[...]
    parts = []
    for h in range(HG_HEADS):
        oh = o[:, h * HG_DK:(h + 1) * HG_DK]
        parts.append(oh * lax.rsqrt(jnp.mean(oh * oh, axis=-1, keepdims=True) + EPS))
    gh = gh_ref[...]
    ob = jnp.concatenate(parts, axis=1) * hg_ref[...] * (gh * _sigmoid(gh))
    ya = _dot(oa_ref[...], wa_ref[...])
    yb = _dot(ob.astype(BF16), wb_ref[...])
    mix = _sigmoid(ga_ref[...]) * ya + _sigmoid(gb_ref[...]) * yb
    x1 = x_ref[...] + mod(2) * _dot(mix.astype(BF16), wo_ref[...])
    x1_ref[...] = x1

    h2 = x1 * lax.rsqrt(jnp.mean(x1 * x1, axis=-1, keepdims=True) + EPS) * n2_ref[...]
    h2 = h2 * (1.0 + mod(4)) + mod(3)
    h2_ref[...] = h2

    h_hi = h2.astype(BF16)
    h_lo = (h2 - h_hi.astype(F32)).astype(BF16)
    rw = rw_ref[...]
    r_hi = rw.astype(BF16)
    r_lo = (rw - r_hi.astype(F32)).astype(BF16)
    logits = _dot(h_hi, r_hi) + (_dot(h_hi, r_lo) + _dot(h_lo, r_hi)) + rb_ref[...]

    lane = lax.broadcasted_iota(jnp.int32, logits.shape, 1)
    work = jnp.where(lane < N_EXPERTS, logits, -jnp.inf)
    sel_any = jnp.zeros(logits.shape, F32)
    vals, idxs = [], []
    for _ in range(TOP_K):
        mx = jnp.max(work, axis=-1, keepdims=True)
        idx = jnp.min(jnp.where(work == mx, lane, 2 * N_EXPERTS), axis=-1, keepdims=True)
        hit = lane == idx
        vals.append(mx)
        idxs.append(idx)
        sel_any = jnp.where(hit, 1.0, sel_any)
        work = jnp.where(hit, -jnp.inf, work)
    es = [jnp.exp(vk - vals[0]) for vk in vals]
    denom = es[0] + es[1] + es[2] + es[3]

    ri = lax.broadcasted_iota(jnp.int32, (tm, tm), 0)
    ci = lax.broadcasted_iota(jnp.int32, (tm, tm), 1)
    below = jnp.where(ci < ri, 1.0, 0.0).astype(BF16)
    rank_e = run_scr[0:1, :] + _dot(below, sel_any.astype(BF16))
    gw = jnp.zeros(logits.shape, F32)
    gi = jnp.zeros(logits.shape, jnp.int32)
    rk = jnp.zeros(logits.shape, jnp.int32)
    for k in range(TOP_K):
        at_k = lane == k
        gw = jnp.where(at_k, es[k] / denom, gw)
        gi = jnp.where(at_k, idxs[k], gi)
        rnk = jnp.sum(jnp.where(lane == idxs[k], rank_e, 0.0), axis=-1, keepdims=True)
        rk = jnp.where(at_k, rnk.astype(jnp.int32), rk)
    gw_ref[...] = gw
    gi_ref[...] = gi
    rk_ref[...] = rk
    run = run_scr[0:1, :] + jnp.sum(sel_any, axis=0, keepdims=True)
    run_scr[...] = jnp.broadcast_to(run, run_scr.shape)
    cnt_ref[...] = jnp.broadcast_to(run, cnt_ref.shape)


def _post(x, oa, o_f, o_b, p, mod, hg_g, n2_g, wa, wb, wo, rw_pad, rb_pad, n_lat):
    t, d = x.shape
    tm = ROW_TILE

    def rows(width, colblk=0):
        return pl.BlockSpec((tm, width), lambda i: (i, colblk))

    def whole(a):
        return pl.BlockSpec(a.shape, lambda i: (0,) * a.ndim)

    return pl.pallas_call(
        functools.partial(_post_kernel, n_lat=n_lat, tm=tm),
        grid=(t // tm,),
        in_specs=[rows(d), rows(DA_W), rows(HG_W), rows(HG_W),
                  rows(HG_W, 7), rows(d, 4), rows(d, 5),
                  whole(mod), whole(hg_g), whole(n2_g), whole(wa), whole(wb), whole(wo),
                  whole(rw_pad), whole(rb_pad)],
        out_specs=[rows(d), rows(d), rows(128), rows(128), rows(128),
                   pl.BlockSpec((8, 128), lambda i: (0, 0))],
        out_shape=[jax.ShapeDtypeStruct((t, d), F32), jax.ShapeDtypeStruct((t, d), F32),
                   jax.ShapeDtypeStruct((t, 128), F32), jax.ShapeDtypeStruct((t, 128), jnp.int32),
                   jax.ShapeDtypeStruct((t, 128), jnp.int32), jax.ShapeDtypeStruct((8, 128), F32)],
        scratch_shapes=[pltpu.VMEM((8, 128), F32)],
        compiler_params=_cparams(("arbitrary",)),
        name="merge_router",
    )(x, oa, o_f, o_b, p, p, p, mod, hg_g, n2_g, wa, wb, wo, rw_pad, rb_pad)


def _dispatch_kernel(dest_ref, h_ref, xs_in, xs_out, sem):
    del xs_in
    n = dest_ref.shape[0]
    tm = h_ref.shape[0]

    def copy(r):
        tok = r & (tm - 1)
        return pltpu.make_async_copy(h_ref.at[pl.ds(tok, 1)], xs_out.at[pl.ds(dest_ref[r], 1)], sem)

    def start(r, c):
        copy(r).start()
        return c

    def wait(r, c):
        copy(r).wait()
        return c

    lax.fori_loop(0, n, start, 0)
    lax.fori_loop(0, n, wait, 0)


def _dispatch(h2, dest_tiles, xs_zero):
    t, d = h2.shape
    tm = ROW_TILE
    return pl.pallas_call(
        _dispatch_kernel,
        grid=(t // tm,),
        in_specs=[pl.BlockSpec((tm * TOP_K,), lambda i: (i,), memory_space=pltpu.SMEM),
                  pl.BlockSpec((tm, d), lambda i: (i, 0)),
                  pl.BlockSpec(memory_space=pl.ANY)],
        out_specs=pl.BlockSpec(memory_space=pl.ANY),
        out_shape=jax.ShapeDtypeStruct(xs_zero.shape, xs_zero.dtype),
        scratch_shapes=[pltpu.SemaphoreType.DMA(())],
        input_output_aliases={2: 0},
        compiler_params=_cparams(("arbitrary",)),
        name="moe_dispatch",
    )(dest_tiles, h2, xs_zero)


def _expert_kernel(be_ref, nu_ref, xs_ref, wgu_ref, bgu_ref, wd_ref, bd_ref, ys_ref):
    del be_ref
    used = pl.program_id(0) < nu_ref[0]

    @pl.when(used)
    def _():
        gu = _dot(xs_ref[...].astype(BF16), wgu_ref[0]) + bgu_ref[0]
        gate = jnp.minimum(gu[:, :D_FF], SWIGLU_LIMIT)
        up = jnp.clip(gu[:, D_FF:], -SWIGLU_LIMIT, SWIGLU_LIMIT)
        hdn = (up + 1.0) * gate * _sigmoid(SWIGLU_ALPHA * gate)
        ys_ref[...] = _dot(hdn.astype(BF16), wd_ref[0]) + bd_ref[0]

    @pl.when(jnp.logical_not(used))
    def _():
        ys_ref[...] = jnp.zeros(ys_ref.shape, F32)


def _experts(xs, blk_e, n_used, wgu, bgu, wd, bd):
    n_rows, d = xs.shape
    nblk = n_rows // MOE_BLOCK

    def blk(i, be, nu):
        return jnp.minimum(i, nu[0] - 1)

    grid_spec = pltpu.PrefetchScalarGridSpec(
        num_scalar_prefetch=2,
        grid=(nblk,),
        in_specs=[
            pl.BlockSpec((MOE_BLOCK, d), lambda i, be, nu: (blk(i, be, nu), 0)),
            pl.BlockSpec((1, d, 2 * D_FF), lambda i, be, nu: (be[blk(i, be, nu)], 0, 0)),
            pl.BlockSpec((1, 1, 2 * D_FF), lambda i, be, nu: (be[blk(i, be, nu)], 0, 0)),
            pl.BlockSpec((1, D_FF, d), lambda i, be, nu: (be[blk(i, be, nu)], 0, 0)),
            pl.BlockSpec((1, 1, d), lambda i, be, nu: (be[blk(i, be, nu)], 0, 0)),
        ],
        out_specs=pl.BlockSpec((MOE_BLOCK, d), lambda i, be, nu: (i, 0)),
    )
    return pl.pallas_call(
        _expert_kernel,
        grid_spec=grid_spec,
        out_shape=jax.ShapeDtypeStruct((n_rows, d), F32),
        compiler_params=_cparams(("arbitrary",)),
        name="moe_experts",
    )(blk_e, n_used, xs, wgu, bgu, wd, bd)


def _combine_kernel(dest_ref, x1_ref, gw_ref, mod_ref, ys_ref, o_ref, buf, sem, *, n_lat, tm):
    i = pl.program_id(0)
    d = x1_ref.shape[1]
    n = dest_ref.shape[0]

    def copy(r):
        return pltpu.make_async_copy(ys_ref.at[pl.ds(dest_ref[r], 1)], buf.at[pl.ds(r, 1)], sem)

    def start(r, c):
        copy(r).start()
        return c

    def wait(r, c):
        copy(r).wait()
        return c

    lax.fori_loop(0, n, start, 0)
    lax.fori_loop(0, n, wait, 0)

    lat = i * tm < n_lat
    g2 = jnp.where(lat, mod_ref[0:1, 5 * d:6 * d], mod_ref[1:2, 5 * d:6 * d])
    gw = gw_ref[...]
    m = gw[:, 0:1] * buf[0:tm, :]
    for k in range(1, TOP_K):
        m = m + gw[:, k:k + 1] * buf[k * tm:(k + 1) * tm, :]
    o_ref[...] = x1_ref[...] + g2 * m


def _combine(x1, gw, mod, ys, dest_tiles, n_lat):
    t, d = x1.shape
    tm = ROW_TILE
    return pl.pallas_call(
        functools.partial(_combine_kernel, n_lat=n_lat, tm=tm),
        grid=(t // tm,),
        in_specs=[pl.BlockSpec((tm * TOP_K,), lambda i: (i,), memory_space=pltpu.SMEM),
                  pl.BlockSpec((tm, d), lambda i: (i, 0)),
                  pl.BlockSpec((tm, 128), lambda i: (i, 0)),
                  pl.BlockSpec(mod.shape, lambda i: (0, 0)),
                  pl.BlockSpec(memory_space=pl.ANY)],
        out_specs=pl.BlockSpec((tm, d), lambda i: (i, 0)),
        out_shape=jax.ShapeDtypeStruct((t, d), F32),
        scratch_shapes=[pltpu.VMEM((tm * TOP_K, d), F32), pltpu.SemaphoreType.DMA(())],
        compiler_params=_cparams(("arbitrary",)),
        name="moe_combine",
    )(dest_tiles, x1, gw, mod, ys)


def _moe_plan(gi, rk, counts):
    t = gi.shape[0]
    top_i = gi[:, :TOP_K]
    rank = rk[:, :TOP_K]
    cnt = counts[0, :N_EXPERTS].astype(jnp.int32)
    padded = (cnt + MOE_BLOCK - 1) // MOE_BLOCK * MOE_BLOCK
    pad_ends = jnp.cumsum(padded)
    pad_starts = pad_ends - padded
    dest = pad_starts[top_i] + rank
    n_rows = t * TOP_K + N_EXPERTS * MOE_BLOCK
    blk_start = jnp.arange(n_rows // MOE_BLOCK, dtype=jnp.int32) * MOE_BLOCK
    blk_e = jnp.minimum(jnp.sum(blk_start[:, None] >= pad_ends[None, :], axis=1), N_EXPERTS - 1).astype(jnp.int32)
    n_used = (pad_ends[-1:] // MOE_BLOCK).astype(jnp.int32)
    dest_tiles = dest.reshape(t // ROW_TILE, ROW_TILE, TOP_K).transpose(0, 2, 1).reshape(-1).astype(jnp.int32)
    return dest_tiles, blk_e, n_used, n_rows


def _rope_tables(n_lat, n_ctx):
    pos = jnp.arange(n_lat)
    row = (pos // GRID_W).astype(F32)
    col = (pos % GRID_W).astype(F32)
    freq = ROPE_BASE ** (-jnp.arange(ROPE_PAIRS, dtype=F32) / ROPE_PAIRS)
    ra = row[:, None] * freq
    ca = col[:, None] * freq
    cos64 = jnp.concatenate([jnp.cos(ra), jnp.cos(ra), jnp.cos(ca), jnp.cos(ca)], axis=1)
    sin64 = jnp.concatenate([-jnp.sin(ra), jnp.sin(ra), -jnp.sin(ca), jnp.sin(ca)], axis=1)
    cos_t = jnp.concatenate([jnp.tile(cos64, (1, DA_W // DA_DH)), jnp.ones((n_ctx, DA_W), F32)], axis=0)
    sin_t = jnp.concatenate([jnp.tile(sin64, (1, DA_W // DA_DH)), jnp.zeros((n_ctx, DA_W), F32)], axis=0)
    return cos_t, sin_t


def kernel(x, c, ctx, c_ctx, ada_w, ada_b, norm1_g, norm2_g, w_in, qn_g, kn_g, lam_qk, subln_g, hg_lb,
           hg_norm_g, w_branch_a, w_branch_b, w_out, router_w, router_b, w_gu, b_gu, w_down, b_down):
    bsz, n_lat, d = x.shape
    n_ctx = ctx.shape[1]
    depth = ada_w.shape[0]
    assert bsz == 1 and d == D_MODEL and n_lat % 512 == 0 and n_ctx % ROW_TILE == 0
    t = n_lat + n_ctx

    xx = jnp.concatenate([x[0], ctx[0]], axis=0)
    cc = jnp.zeros((8, d), F32).at[0].set(c[0]).at[1].set(c_ctx)
    mods = _modulation(cc, ada_w, ada_b)

    cos_t, sin_t = _rope_tables(n_lat, n_ctx)
    lane = jnp.arange(DA_W)
    seg64 = (lane[:, None] // DA_DH == lane[None, :] // DA_DH).astype(BF16)
    cs = jnp.cumsum(jax.nn.softmax(hg_lb.astype(F32), axis=1), axis=1)
    lb_all = cs - cs[:, :1]
    rw_pad = jnp.zeros((depth, d, 128), F32).at[:, :, :N_EXPERTS].set(router_w)
    rb_pad = jnp.zeros((depth, 1, 128), F32).at[:, 0, :N_EXPERTS].set(router_b)

    tk = 1280 if t % 1280 == 0 else ROW_TILE
    for l in range(depth):
        lam_init = 0.8 - 0.6 * math.exp(-0.3 * l)
        mod = mods[l]
        p = _inproj(xx, norm1_g[l][None], mod, w_in[l].astype(BF16), n_lat)
        k_r, q_t, v_t = _prep(p, cos_t, sin_t, seg64,
                              jnp.tile(qn_g[l], DA_W // DA_DH)[None], jnp.tile(kn_g[l], DA_W // DA_DH)[None])
        sg_col = subln_g[l][:, None]
        oa_lat = _attention(q_t, k_r, v_t, lam_qk[l], sg_col, lam_init,
                            q_start=0, n_q=n_lat, k_start=0, n_k=t, tq=512, tk=tk)
        oa_ctx = _attention(q_t, k_r, v_t, lam_qk[l], sg_col, lam_init,
                            q_start=n_lat, n_q=n_ctx, k_start=n_lat, n_k=n_ctx, tq=ROW_TILE, tk=ROW_TILE)
        oa = jnp.concatenate([oa_lat, oa_ctx], axis=0)
        o_f, o_b = _hgrn(p, lb_all[0, l][None], lb_all[1, l][None], n_lat)
        x1, h2, gw, gi, rk, counts = _post(
            xx, oa, o_f, o_b, p, mod, jnp.tile(hg_norm_g[l], HG_HEADS)[None], norm2_g[l][None],
            w_branch_a[l].astype(BF16), w_branch_b[l].astype(BF16), w_out[l].astype(BF16),
            rw_pad[l], rb_pad[l], n_lat)
        dest_tiles, blk_e, n_used, n_rows = _moe_plan(gi, rk, counts)
        xs = _dispatch(h2, dest_tiles, jnp.zeros((n_rows, d), F32))
        ys = _experts(xs, blk_e, n_used, w_gu[l].astype(BF16), b_gu[l][:, None, :],
                      w_down[l].astype(BF16), b_down[l][:, None, :])
        xx = _combine(x1, gw, mod, ys, dest_tiles, n_lat)
    return xx[:n_lat][None]
```

```python
import functools
import math

import jax
import jax.numpy as jnp
from jax import lax
from jax.experimental import pallas as pl
from jax.experimental.pallas import tpu as pltpu

F32 = jnp.float32
BF16 = jnp.bfloat16

D_MODEL = 1024
GRID_W = 64
EPS = 1e-6

DA_HEADS = 4
DA_DH = 64
DA_DV = 2 * DA_DH
DA_W = DA_HEADS * DA_DV
DA_SCALE = DA_DH ** -0.5
ROPE_PAIRS = DA_DH // 4
ROPE_BASE = 10000.0

HG_HEADS = 4
HG_DK = 128
HG_W = HG_HEADS * HG_DK
HG_CHUNK = 64

N_EXPERTS = 32
TOP_K = 4
D_FF = 1024
SWIGLU_ALPHA = 1.702
SWIGLU_LIMIT = 7.0
MOE_BLOCK = 256

IN_COLS = 6144
ROW_TILE = 256
VT_ROWS = DA_DV + 16
NEG_BIG = -1e30
LOG2E = 1.4426950408889634

VMEM_LIMIT = 56 * 1024 * 1024
ATT_TQ = 512
SCORE_BOUND_MAX = 40.0
SCORE_BOUND_SLACK = 1.01


def _cparams(sem):
    return pltpu.CompilerParams(dimension_semantics=sem, vmem_limit_bytes=VMEM_LIMIT)


def _split3(x):
    a = x.astype(BF16)
    r = x - a.astype(F32)
    b = r.astype(BF16)
    c = (r - b.astype(F32)).astype(BF16)
    return a, b, c


def _dot(a, b):
    return jnp.dot(a, b, preferred_element_type=F32)


def _dot_nt(a, b):
    return lax.dot_general(a, b, (((1,), (1,)), ((), ())), preferred_element_type=F32)


def _dot_tn(a, b):
    return lax.dot_general(a, b, (((0,), (0,)), ((), ())), preferred_element_type=F32)


def _sigmoid(x):
    return 1.0 / (1.0 + jnp.exp(-x))


def _store_row_tiles(ref, val):
    for s in range(ref.shape[1]):
        ref[:, s, :] = val[:, s * 128:(s + 1) * 128]


def _load_row_tiles(ref, lo, n):
    return jnp.concatenate([ref[lo:lo + n, s, :] for s in range(ref.shape[1])], axis=1)


def _mod_kernel(c_ref, w_ref, b_ref, o_ref):
    cv = c_ref[...]
    a = cv * _sigmoid(cv)
    w = w_ref[0]
    a1, a2, a3 = _split3(a)
    w1, w2, w3 = _split3(w)
    acc = _dot(a1, w1) + (_dot(a1, w2) + _dot(a2, w1)) + (_dot(a2, w2) + _dot(a1, w3) + _dot(a3, w1))
    o_ref[0] = acc + b_ref[0]


def _modulation(cc, ada_w, ada_b):
    depth, d, n = ada_w.shape
    tn = 1536
    return pl.pallas_call(
        _mod_kernel,
        grid=(depth, n // tn),
        in_specs=[
            pl.BlockSpec((8, d), lambda l, j: (0, 0)),
            pl.BlockSpec((1, d, tn), lambda l, j: (l, 0, j)),
            pl.BlockSpec((1, 1, tn), lambda l, j: (l, 0, j)),
        ],
        out_specs=pl.BlockSpec((1, 8, tn), lambda l, j: (l, 0, j)),
        out_shape=jax.ShapeDtypeStruct((depth, 8, n), F32),
        compiler_params=_cparams(("parallel", "parallel")),
        name="adaln_mod",
    )(cc, ada_w, ada_b.reshape(depth, 1, n))


def _inproj_kernel(x_ref, g_ref, mod_ref, w_ref, o_ref, h_scr, *, n_lat, tm):
    i = pl.program_id(0)
    j = pl.program_id(1)
    d = x_ref.shape[1]

    @pl.when(j == 0)
    def _():
        x = x_ref[...]
        y = x * lax.rsqrt(jnp.mean(x * x, axis=-1, keepdims=True) + EPS) * g_ref[...]
        row = i * tm + lax.broadcasted_iota(jnp.int32, (tm, 1), 0)
        lat = row < n_lat
        sh = jnp.where(lat, mod_ref[0:1, 0:d], mod_ref[1:2, 0:d])
        sc = jnp.where(lat, mod_ref[0:1, d:2 * d], mod_ref[1:2, d:2 * d])
        h_scr[...] = (y * (1.0 + sc) + sh).astype(BF16)

    o_ref[...] = _dot(h_scr[...], w_ref[...])


def _inproj(x, g, mod, w_bf, n_lat):
    t, d = x.shape
    n = w_bf.shape[1]
    tm = 1280 if t % 1280 == 0 else ROW_TILE
    tn = 1536
    return pl.pallas_call(
        functools.partial(_inproj_kernel, n_lat=n_lat, tm=tm),
        grid=(t // tm, n // tn),
        in_specs=[
            pl.BlockSpec((tm, d), lambda i, j: (i, 0)),
            pl.BlockSpec((1, d), lambda i, j: (0, 0)),
            pl.BlockSpec((8, mod.shape[1]), lambda i, j: (0, 0)),
            pl.BlockSpec((d, tn), lambda i, j: (0, j)),
        ],
        out_specs=pl.BlockSpec((tm, tn), lambda i, j: (i, j)),
        out_shape=jax.ShapeDtypeStruct((t, n), F32),
        scratch_shapes=[pltpu.VMEM((tm, d), BF16)],
        compiler_params=_cparams(("parallel", "arbitrary")),
        name="inproj",
    )(x, g, mod, w_bf)


def _segment_mean_sq(x, seg_ref, width):
    x2 = x * x
    hi = x2.astype(BF16)
    lo = (x2 - hi.astype(F32)).astype(BF16)
    seg = seg_ref[...]
    return (_dot(hi, seg) + _dot(lo, seg)) * (1.0 / width)


def _prep_kernel(q_ref, k_ref, v_ref, cos_ref, sin_ref, seg_ref, qg_ref, kg_ref,
                 k_out, qt_out, vt_out, qn_out, kn_out):
    tm = q_ref.shape[0]
    cos = cos_ref[...]
    sin = sin_ref[...]
    lane = lax.broadcasted_iota(jnp.int32, (tm, DA_W), 1)
    first_half = (lane & 31) < 16

    def norm_rope(x, g):
        y = x * lax.rsqrt(_segment_mean_sq(x, seg_ref, DA_DH) + EPS) * g
        fwd = pltpu.roll(y, DA_W - 16, 1)
        bwd = pltpu.roll(y, 16, 1)
        partner = jnp.where(first_half, fwd, bwd)
        return y * cos + partner * sin

    k_b = norm_rope(k_ref[...], kg_ref[...]).astype(BF16)
    k_out[...] = k_b
    k_f = k_b.astype(F32)
    k_n2 = _dot((k_f * k_f).astype(BF16), seg_ref[...])
    kn_out[...] = jnp.broadcast_to(jnp.max(k_n2, axis=0, keepdims=True), kn_out.shape)

    q = norm_rope(q_ref[...], qg_ref[...]) * (DA_SCALE * LOG2E)
    lane_h = lax.broadcasted_iota(jnp.int32, (tm, DA_DV), 1)
    v = v_ref[...]
    ones = jnp.ones((VT_ROWS - DA_DV, tm), BF16)
    for h in range(DA_HEADS):
        qh = q[:, h * DA_DV:(h + 1) * DA_DV]
        for m in range(2):
            keep = (lane_h < DA_DH) if m == 0 else (lane_h >= DA_DH)
            r = 2 * h + m
            qt_b = jnp.where(keep, qh, 0.0).T.astype(BF16)
            qt_out[r * DA_DV:(r + 1) * DA_DV, :] = qt_b
            qt_f = qt_b.astype(F32)
            qn_out[r:r + 1, :] = jnp.sum(qt_f * qt_f, axis=0, keepdims=True)
        vt_out[h * VT_ROWS:h * VT_ROWS + DA_DV, :] = v[:, h * DA_DV:(h + 1) * DA_DV].T.astype(BF16)
        vt_out[h * VT_ROWS + DA_DV:(h + 1) * VT_ROWS, :] = ones


def _prep(p, cos_t, sin_t, seg64, qg, kg):
    t = p.shape[0]
    tm = ROW_TILE
    return pl.pallas_call(
        _prep_kernel,
        grid=(t // tm,),
        in_specs=[
            pl.BlockSpec((tm, DA_W), lambda i: (i, 0)),
            pl.BlockSpec((tm, DA_W), lambda i: (i, 1)),
            pl.BlockSpec((tm, DA_W), lambda i: (i, 2)),
            pl.BlockSpec((tm, DA_W), lambda i: (i, 0)),
            pl.BlockSpec((tm, DA_W), lambda i: (i, 0)),
            pl.BlockSpec((DA_W, DA_W), lambda i: (0, 0)),
            pl.BlockSpec((1, DA_W), lambda i: (0, 0)),
            pl.BlockSpec((1, DA_W), lambda i: (0, 0)),
        ],
        out_specs=[
            pl.BlockSpec((tm, DA_W), lambda i: (i, 0)),
            pl.BlockSpec((2 * DA_HEADS * DA_DV, tm), lambda i: (0, i)),
            pl.BlockSpec((DA_HEADS * VT_ROWS, tm), lambda i: (0, i)),
            pl.BlockSpec((2 * DA_HEADS, tm), lambda i: (0, i)),
            pl.BlockSpec((8, DA_W), lambda i: (i, 0)),
        ],
        out_shape=[
            jax.ShapeDtypeStruct((t, DA_W), BF16),
            jax.ShapeDtypeStruct((2 * DA_HEADS * DA_DV, t), BF16),
            jax.ShapeDtypeStruct((DA_HEADS * VT_ROWS, t), BF16),
            jax.ShapeDtypeStruct((2 * DA_HEADS, t), F32),
            jax.ShapeDtypeStruct((8 * (t // tm), DA_W), F32),
        ],
        compiler_params=_cparams(("parallel",)),
        name="qkv_prep",
    )(p, p, p, cos_t, sin_t, seg64, qg, kg)


def _attn_finalize(lam_ref, sg_ref, o_ref, acc_scr, lam_init):
    lq = lam_ref[...]
    lam = (jnp.exp(jnp.sum(lq[0:1] * lq[1:2], axis=-1, keepdims=True))
           - jnp.exp(jnp.sum(lq[2:3] * lq[3:4], axis=-1, keepdims=True)) + lam_init)
    for h in range(DA_HEADS):
        a1 = acc_scr[2 * h]
        a2 = acc_scr[2 * h + 1]
        o = a1[0:DA_DV] / a1[DA_DV:DA_DV + 1] - lam * (a2[0:DA_DV] / a2[DA_DV:DA_DV + 1])
        o = o * lax.rsqrt(jnp.mean(o * o, axis=0, keepdims=True) + EPS)
        o = o * sg_ref[...] * (1.0 - lam_init)
        o_ref[:, h * DA_DV:(h + 1) * DA_DV] = o.T.astype(o_ref.dtype)


def _attn_kernel(qt_ref, k_ref, vt_ref, mref_ref, lam_ref, sg_ref, o_ref, m_scr, acc_scr, *, lam_init):
    del mref_ref
    kk = pl.program_id(1)
    nk = pl.num_programs(1)

    @pl.when(kk == 0)
    def _():
        m_scr[...] = jnp.full(m_scr.shape, NEG_BIG, F32)
        acc_scr[...] = jnp.zeros(acc_scr.shape, F32)

    for h in range(DA_HEADS):
        k_h = k_ref[:, h * DA_DV:(h + 1) * DA_DV]
        vt_h = vt_ref[h * VT_ROWS:(h + 1) * VT_ROWS, :]
        for m in range(2):
            r = 2 * h + m
            s = _dot(k_h, qt_ref[r * DA_DV:(r + 1) * DA_DV, :])
            m_prev = m_scr[r:r + 1, :]
            m_new = jnp.maximum(m_prev, jnp.max(s, axis=0, keepdims=True))
            alpha = jnp.exp2(m_prev - m_new)
            p = jnp.exp2((s - m_new).astype(BF16))
            acc_scr[r] = alpha * acc_scr[r] + _dot(vt_h, p)
            m_scr[r:r + 1, :] = m_new

    @pl.when(kk == nk - 1)
    def _():
        _attn_finalize(lam_ref, sg_ref, o_ref, acc_scr, lam_init)


def _attn_fixed_kernel(qt_ref, k_ref, vt_ref, mref_ref, lam_ref, sg_ref, o_ref, acc_scr, *, lam_init):
    kk = pl.program_id(1)
    nk = pl.num_programs(1)

    @pl.when(kk == 0)
    def _():
        acc_scr[...] = jnp.zeros(acc_scr.shape, F32)

    for h in range(DA_HEADS):
        k_h = k_ref[:, h * DA_DV:(h + 1) * DA_DV]
        vt_h = vt_ref[h * VT_ROWS:(h + 1) * VT_ROWS, :]
        for m in range(2):
            r = 2 * h + m
            s = _dot(k_h, qt_ref[r * DA_DV:(r + 1) * DA_DV, :])
            p = jnp.exp2((s - mref_ref[r:r + 1, :]).astype(BF16))
            acc_scr[r] += _dot(vt_h, p)

    @pl.when(kk == nk - 1)
    def _():
        _attn_finalize(lam_ref, sg_ref, o_ref, acc_scr, lam_init)


def _attention(qt, k, vt, mref, lam_qk, subln_col, lam_init, fixed_ref, *, q_start, n_q, k_start, n_k, tq, tk):
    qo, ko = q_start // tq, k_start // tk
    acc = pltpu.VMEM((2 * DA_HEADS, VT_ROWS, tq), F32)
    if fixed_ref:
        body, scratch, name = _attn_fixed_kernel, [acc], "diff_attn_fixed"
    else:
        body, scratch, name = _attn_kernel, [pltpu.VMEM((2 * DA_HEADS, tq), F32), acc], "diff_attn_online"
    return pl.pallas_call(
        functools.partial(body, lam_init=lam_init),
        grid=(n_q // tq, n_k // tk),
        in_specs=[
            pl.BlockSpec((qt.shape[0], tq), lambda i, j: (0, i + qo)),
            pl.BlockSpec((tk, DA_W), lambda i, j: (j + ko, 0)),
            pl.BlockSpec((vt.shape[0], tk), lambda i, j: (0, j + ko)),
            pl.BlockSpec((2 * DA_HEADS, tq), lambda i, j: (0, i + qo)),
            pl.BlockSpec((4, DA_DH), lambda i, j: (0, 0)),
            pl.BlockSpec((DA_DV, 1), lambda i, j: (0, 0)),
        ],
        out_specs=pl.BlockSpec((tq, DA_W), lambda i, j: (i, 0)),
        out_shape=jax.ShapeDtypeStruct((n_q, DA_W), BF16),
        scratch_shapes=scratch,
        compiler_params=_cparams(("parallel", "arbitrary")),
        name=name,
    )(qt, k, vt, mref, lam_qk, subln_col)


def _attention_all(qt, k, vt, qn2, kn2_tiles, lam_qk, subln_col, lam_init, n_lat, n_ctx, tk):
    t = n_lat + n_ctx
    n_maps = 2 * DA_HEADS
    kmax2 = jnp.max(kn2_tiles, axis=0).reshape(n_maps, DA_DH)[:, 0]
    mref = jnp.sqrt(qn2) * jnp.sqrt(kmax2)[:, None] * SCORE_BOUND_SLACK
    bounded = jnp.max(mref) <= SCORE_BOUND_MAX

    def run(fixed_ref):
        def f(args):
            lat = _attention(*args, lam_init, fixed_ref,
                             q_start=0, n_q=n_lat, k_start=0, n_k=t, tq=ATT_TQ, tk=tk)
            ctx = _attention(*args, lam_init, fixed_ref,
                             q_start=n_lat, n_q=n_ctx, k_start=n_lat, n_k=n_ctx, tq=ROW_TILE, tk=ROW_TILE)
            return jnp.concatenate([lat, ctx], axis=0)
        return f

    return lax.cond(bounded, run(True), run(False), (qt, k, vt, mref, lam_qk, subln_col))


def _hg_chunk(q, z, v, lb, st_ref, rev):
    c, w = q.shape
    row = lax.broadcasted_iota(jnp.int32, (c, 1), 0)

    ls = jnp.minimum(z, 0.0) - jnp.log1p(jnp.exp(-jnp.abs(z)))
    y = jnp.log1p(-lb) + ls
    log_lb = jnp.log(lb)
    log_f = jnp.maximum(log_lb, y) + jnp.log1p(jnp.exp(-jnp.abs(log_lb - y)))
    key = (1.0 - lb) / (1.0 + jnp.exp(z))

    ri = lax.broadcasted_iota(jnp.int32, (c, c), 0)
    ci = lax.broadcasted_iota(jnp.int32, (c, c), 1)
    tri = jnp.where((ci >= ri) if rev else (ci <= ri), 1.0, 0.0).astype(BF16)
    g1, g2, g3 = _split3(log_f)
    b = _dot(tri, g1) + _dot(tri, g2) + _dot(tri, g3)
    end = 0 if rev else c - 1
    b_end = b[end:end + 1, :]

    q_in = (q * jnp.exp(b)).astype(BF16)
    k_out = (key * jnp.exp(b_end - b)).astype(BF16)
    decay = jnp.exp(b_end)

    up = pltpu.roll(log_f, 1, 0)
    dn = pltpu.roll(log_f, c - 1, 0)
    q_lv, k_lv, masks = [], [], []
    lvl = c // 2
    while lvl >= 1:
        half = (row // lvl) & 1
        q_rows = (half == 0) if rev else (half == 1)
        if lvl >= 4:
            pos = lvl if rev else lvl - 1
            ref = jnp.broadcast_to(b.reshape(c // (2 * lvl), 2 * lvl, w)[:, pos:pos + 1, :],
                                   (c // (2 * lvl), 2 * lvl, w)).reshape(c, w)
            eq, ek = b - ref, ref - b
        elif lvl == 2:
            j = row & 3
            if rev:
                eq = jnp.where(j == 1, log_f, log_f + dn)
                ek = jnp.where(j == 2, 0.0, up)
            else:
                eq = jnp.where(j == 2, log_f, log_f + up)
                ek = jnp.where(j == 1, 0.0, dn)
        else:
            eq, ek = log_f, jnp.zeros_like(log_f)
        q_lv.append((q * jnp.exp(jnp.where(q_rows, eq, NEG_BIG))).astype(BF16))
        k_lv.append((key * jnp.exp(jnp.where(q_rows, NEG_BIG, ek))).astype(BF16))
        masks.append(None if 2 * lvl == c else
                     jnp.where((ri // (2 * lvl)) == (ci // (2 * lvl)), 1.0, 0.0))
        lvl //= 2
    qk = q * key
    on_diag = ri == ci

    outs = []
    for h in range(HG_HEADS):
        sl = slice(h * HG_DK, (h + 1) * HG_DK)
        a = jnp.where(on_diag, jnp.sum(qk[:, sl], axis=-1, keepdims=True), 0.0)
        for q_l, k_l, msk in zip(q_lv, k_lv, masks):
            a_l = _dot_nt(q_l[:, sl], k_l[:, sl])
            a = a + (a_l if msk is None else a_l * msk)
        st = st_ref[h]
        v_h = v[:, sl].astype(BF16)
        outs.append(_dot_nt(q_in[:, sl], st.astype(BF16)) + _dot(a.astype(BF16), v_h))
        st_ref[h] = st * decay[:, sl] + _dot_tn(v_h, k_out[:, sl])
    return jnp.concatenate(outs, axis=1)


def _hgrn_kernel(qf_ref, zf_ref, vf_ref, qb_ref, zb_ref, vb_ref, lbf_ref, lbb_ref,
                 of_ref, ob_ref, sf_scr, sb_scr):
    @pl.when(pl.program_id(0) == 0)
    def _():
        sf_scr[...] = jnp.zeros(sf_scr.shape, F32)
        sb_scr[...] = jnp.zeros(sb_scr.shape, F32)

    n_chunks = qf_ref.shape[0] // HG_CHUNK

    def body(ci, carry):
        rf = pl.ds(pl.multiple_of(ci * HG_CHUNK, HG_CHUNK), HG_CHUNK)
        of_ref[rf, :] = _hg_chunk(qf_ref[rf, :], zf_ref[rf, :], vf_ref[rf, :], lbf_ref[...], sf_scr, False)
        rb = pl.ds(pl.multiple_of((n_chunks - 1 - ci) * HG_CHUNK, HG_CHUNK), HG_CHUNK)
        ob_ref[rb, :] = _hg_chunk(qb_ref[rb, :], zb_ref[rb, :], vb_ref[rb, :], lbb_ref[...], sb_scr, True)
        return carry

    lax.fori_loop(0, n_chunks, body, 0)


def _hgrn(p, lb_f, lb_b, n_lat):
    t = p.shape[0]
    tm = ROW_TILE
    nb = t // tm
    n_lat_b = n_lat // tm
    n_ctx_b = nb - n_lat_b

    def jf(i):
        return jnp.where(i < n_ctx_b, n_lat_b + i, i - n_ctx_b)

    def jb(i):
        return jnp.where(i < n_ctx_b, nb - 1 - i, nb - 1 - i)

    def spec(order, colblk):
        return pl.BlockSpec((tm, HG_W), lambda i: (order(i), colblk))

    return pl.pallas_call(
        _hgrn_kernel,
        grid=(nb,),
        in_specs=[spec(jf, 3), spec(jf, 4), spec(jf, 6), spec(jb, 3), spec(jb, 5), spec(jb, 6),
                  pl.BlockSpec((1, HG_W), lambda i: (0, 0)), pl.BlockSpec((1, HG_W), lambda i: (0, 0))],
        out_specs=[pl.BlockSpec((tm, HG_W), lambda i: (jf(i), 0)),
                   pl.BlockSpec((tm, HG_W), lambda i: (jb(i), 0))],
        out_shape=[jax.ShapeDtypeStruct((t, HG_W), F32), jax.ShapeDtypeStruct((t, HG_W), F32)],
        scratch_shapes=[pltpu.VMEM((HG_HEADS, HG_DK, HG_DK), F32), pltpu.VMEM((HG_HEADS, HG_DK, HG_DK), F32)],
        compiler_params=_cparams(("arbitrary",)),
        name="hgrn2",
    )(p, p, p, p, p, p, lb_f, lb_b)


def _post_kernel(x_ref, oa_ref, of_ref, ob_ref, gh_ref, ga_ref, gb_ref, mod_ref, hg_ref, n2_ref,
                 wa_ref, wb_ref, wo_ref, rw_ref, rb_ref,
                 x1_ref, h2_ref, gw_ref, gi_ref, rk_ref, cnt_ref, run_scr, *, n_lat, tm):
    i = pl.program_id(0)
    d = x_ref.shape[1]

    @pl.when(i == 0)
    def _():
        run_scr[...] = jnp.zeros(run_scr.shape, F32)

    lat = i * tm < n_lat

    def mod(c):
        return jnp.where(lat, mod_ref[0:1, c * d:(c + 1) * d], mod_ref[1:2, c * d:(c + 1) * d])

    o = of_ref[...] + ob_ref[...]
    parts = []
    for h in range(HG_HEADS):
        oh = o[:, h * HG_DK:(h + 1) * HG_DK]
        parts.append(oh * lax.rsqrt(jnp.mean(oh * oh, axis=-1, keepdims=True) + EPS))
    gh = gh_ref[...]
    ob = jnp.concatenate(parts, axis=1) * hg_ref[...] * (gh * _sigmoid(gh))
    ya = _dot(oa_ref[...], wa_ref[...])
    yb = _dot(ob.astype(BF16), wb_ref[...])
    mix = _sigmoid(ga_ref[...]) * ya + _sigmoid(gb_ref[...]) * yb
    x1 = x_ref[...] + mod(2) * _dot(mix.astype(BF16), wo_ref[...])
    x1_ref[...] = x1

    h2 = x1 * lax.rsqrt(jnp.mean(x1 * x1, axis=-1, keepdims=True) + EPS) * n2_ref[...]
    h2 = h2 * (1.0 + mod(4)) + mod(3)
    _store_row_tiles(h2_ref, h2)

    h_hi = h2.astype(BF16)
    h_lo = (h2 - h_hi.astype(F32)).astype(BF16)
    rw = rw_ref[...]
    r_hi = rw.astype(BF16)
    r_lo = (rw - r_hi.astype(F32)).astype(BF16)
    logits = _dot(h_hi, r_hi) + (_dot(h_hi, r_lo) + _dot(h_lo, r_hi)) + rb_ref[...]

    lane = lax.broadcasted_iota(jnp.int32, logits.shape, 1)
    work = jnp.where(lane < N_EXPERTS, logits, -jnp.inf)
    sel_any = jnp.zeros(logits.shape, F32)
    vals, idxs = [], []
    for _ in range(TOP_K):
        mx = jnp.max(work, axis=-1, keepdims=True)
        idx = jnp.min(jnp.where(work == mx, lane, 2 * N_EXPERTS), axis=-1, keepdims=True)
        hit = lane == idx
        vals.append(mx)
        idxs.append(idx)
        sel_any = jnp.where(hit, 1.0, sel_any)
        work = jnp.where(hit, -jnp.inf, work)
    es = [jnp.exp(vk - vals[0]) for vk in vals]
    denom = es[0] + es[1] + es[2] + es[3]

    ri = lax.broadcasted_iota(jnp.int32, (tm, tm), 0)
    ci = lax.broadcasted_iota(jnp.int32, (tm, tm), 1)
    below = jnp.where(ci < ri, 1.0, 0.0).astype(BF16)
    rank_e = run_scr[0:1, :] + _dot(below, sel_any.astype(BF16))
    gw = jnp.zeros(logits.shape, F32)
    gi = jnp.zeros(logits.shape, jnp.int32)
    rk = jnp.zeros(logits.shape, jnp.int32)
    for k in range(TOP_K):
        at_k = lane == k
        gw = jnp.where(at_k, es[k] / denom, gw)
        gi = jnp.where(at_k, idxs[k], gi)
        rnk = jnp.sum(jnp.where(lane == idxs[k], rank_e, 0.0), axis=-1, keepdims=True)
        rk = jnp.where(at_k, rnk.astype(jnp.int32), rk)
    gw_ref[...] = gw
    gi_ref[...] = gi
    rk_ref[...] = rk
    run = run_scr[0:1, :] + jnp.sum(sel_any, axis=0, keepdims=True)
    run_scr[...] = jnp.broadcast_to(run, run_scr.shape)
    cnt_ref[...] = jnp.broadcast_to(run, cnt_ref.shape)


def _post(x, oa, o_f, o_b, p, mod, hg_g, n2_g, wa, wb, wo, rw_pad, rb_pad, n_lat):
    t, d = x.shape
    tm = ROW_TILE

    def rows(width, colblk=0):
        return pl.BlockSpec((tm, width), lambda i: (i, colblk))

    def whole(a):
        return pl.BlockSpec(a.shape, lambda i: (0,) * a.ndim)

    return pl.pallas_call(
        functools.partial(_post_kernel, n_lat=n_lat, tm=tm),
        grid=(t // tm,),
        in_specs=[rows(d), rows(DA_W), rows(HG_W), rows(HG_W),
                  rows(HG_W, 7), rows(d, 4), rows(d, 5),
                  whole(mod), whole(hg_g), whole(n2_g), whole(wa), whole(wb), whole(wo),
                  whole(rw_pad), whole(rb_pad)],
        out_specs=[rows(d), pl.BlockSpec((tm, d // 128, 128), lambda i: (i, 0, 0)), rows(128), rows(128), rows(128),
                   pl.BlockSpec((8, 128), lambda i: (0, 0))],
        out_shape=[jax.ShapeDtypeStruct((t, d), F32), jax.ShapeDtypeStruct((t, d // 128, 128), F32),
                   jax.ShapeDtypeStruct((t, 128), F32), jax.ShapeDtypeStruct((t, 128), jnp.int32),
                   jax.ShapeDtypeStruct((t, 128), jnp.int32), jax.ShapeDtypeStruct((8, 128), F32)],
        scratch_shapes=[pltpu.VMEM((8, 128), F32)],
        compiler_params=_cparams(("arbitrary",)),
        name="merge_router",
    )(x, oa, o_f, o_b, p, p, p, mod, hg_g, n2_g, wa, wb, wo, rw_pad, rb_pad)


def _dispatch_kernel(dest_ref, h_ref, xs_in, xs_out, sem):
    del xs_in
    n = dest_ref.shape[0]
    tm = h_ref.shape[0]

    def start(r, c):
        pltpu.make_async_copy(h_ref.at[r & (tm - 1)], xs_out.at[dest_ref[r]], sem).start()
        return c

    lax.fori_loop(0, n, start, 0, unroll=8)
    pltpu.make_async_copy(xs_out.at[pl.ds(0, n)], xs_out.at[pl.ds(0, n)], sem).wait()


def _dispatch(h2, dest_tiles, xs_zero):
    t = h2.shape[0]
    tm = ROW_TILE
    return pl.pallas_call(
        _dispatch_kernel,
        grid=(t // tm,),
        in_specs=[pl.BlockSpec((tm * TOP_K,), lambda i: (i,), memory_space=pltpu.SMEM),
                  pl.BlockSpec((tm,) + h2.shape[1:], lambda i: (i, 0, 0)),
                  pl.BlockSpec(memory_space=pl.ANY)],
        out_specs=pl.BlockSpec(memory_space=pl.ANY),
        out_shape=jax.ShapeDtypeStruct(xs_zero.shape, xs_zero.dtype),
        scratch_shapes=[pltpu.SemaphoreType.DMA(())],
        input_output_aliases={2: 0},
        compiler_params=_cparams(("arbitrary",)),
        name="moe_dispatch",
    )(dest_tiles, h2, xs_zero)


def _expert_kernel(be_ref, nu_ref, xs_ref, wgu_ref, bgu_ref, wd_ref, bd_ref, ys_ref, wgu_bf, wd_bf):
    i = pl.program_id(0)
    used = i < nu_ref[0]

    @pl.when(jnp.logical_and(used, jnp.logical_or(i == 0, be_ref[i] != be_ref[jnp.maximum(i - 1, 0)])))
    def _():
        wgu_bf[...] = wgu_ref[0, 0].astype(BF16)
        wd_bf[...] = wd_ref[0, 0].astype(BF16)

    @pl.when(used)
    def _():
        x = _load_row_tiles(xs_ref, 0, xs_ref.shape[0]).astype(BF16)
        gu = _dot(x, wgu_bf[...]) + bgu_ref[0, 0]
        gate = jnp.minimum(gu[:, :D_FF], SWIGLU_LIMIT)
        up = jnp.clip(gu[:, D_FF:], -SWIGLU_LIMIT, SWIGLU_LIMIT)
        hdn = (up + 1.0) * gate * _sigmoid(SWIGLU_ALPHA * gate)
        _store_row_tiles(ys_ref, _dot(hdn.astype(BF16), wd_bf[...]) + bd_ref[0, 0])

    @pl.when(jnp.logical_not(used))
    def _():
        ys_ref[...] = jnp.zeros(ys_ref.shape, F32)


def _experts(xs, blk_e, n_used, wgu, bgu, wd, bd, layer):
    n_rows = xs.shape[0]
    d = xs.shape[1] * xs.shape[2]
    nblk = n_rows // MOE_BLOCK

    def blk(i, be, nu):
        return jnp.minimum(i, nu[0] - 1)

    def expert(i, be, nu):
        return (layer, be[blk(i, be, nu)], 0, 0)

    grid_spec = pltpu.PrefetchScalarGridSpec(
        num_scalar_prefetch=2,
        grid=(nblk,),
        in_specs=[
            pl.BlockSpec((MOE_BLOCK,) + xs.shape[1:], lambda i, be, nu: (blk(i, be, nu), 0, 0)),
            pl.BlockSpec((1, 1, d, 2 * D_FF), expert),
            pl.BlockSpec((1, 1, 1, 2 * D_FF), expert),
            pl.BlockSpec((1, 1, D_FF, d), expert),
            pl.BlockSpec((1, 1, 1, d), expert),
        ],
        out_specs=pl.BlockSpec((MOE_BLOCK,) + xs.shape[1:], lambda i, be, nu: (i, 0, 0)),
        scratch_shapes=[pltpu.VMEM((d, 2 * D_FF), BF16), pltpu.VMEM((D_FF, d), BF16)],
    )
    return pl.pallas_call(
        _expert_kernel,
        grid_spec=grid_spec,
        out_shape=jax.ShapeDtypeStruct(xs.shape, F32),
        compiler_params=_cparams(("arbitrary",)),
        name="moe_experts",
    )(blk_e, n_used, xs, wgu, bgu, wd, bd)


def _combine_kernel(dest_ref, x1_ref, gw_ref, mod_ref, ys_ref, o_ref, buf, sem, *, n_lat, tm):
    i = pl.program_id(0)
    d = x1_ref.shape[1]
    n = dest_ref.shape[0]

    def start(r, c):
        pltpu.make_async_copy(ys_ref.at[dest_ref[r]], buf.at[r], sem).start()
        return c

    lax.fori_loop(0, n, start, 0, unroll=8)
    pltpu.make_async_copy(ys_ref.at[pl.ds(0, n)], buf, sem).wait()

    lat = i * tm < n_lat
    g2 = jnp.where(lat, mod_ref[0:1, 5 * d:6 * d], mod_ref[1:2, 5 * d:6 * d])
    gw = gw_ref[...]
    m = gw[:, 0:1] * _load_row_tiles(buf, 0, tm)
    for k in range(1, TOP_K):
        m = m + gw[:, k:k + 1] * _load_row_tiles(buf, k * tm, tm)
    o_ref[...] = x1_ref[...] + g2 * m


def _combine(x1, gw, mod, ys, dest_tiles, n_lat):
    t, d = x1.shape
    tm = ROW_TILE
    return pl.pallas_call(
        functools.partial(_combine_kernel, n_lat=n_lat, tm=tm),
        grid=(t // tm,),
        in_specs=[pl.BlockSpec((tm * TOP_K,), lambda i: (i,), memory_space=pltpu.SMEM),
                  pl.BlockSpec((tm, d), lambda i: (i, 0)),
                  pl.BlockSpec((tm, 128), lambda i: (i, 0)),
                  pl.BlockSpec(mod.shape, lambda i: (0, 0)),
                  pl.BlockSpec(memory_space=pl.ANY)],
        out_specs=pl.BlockSpec((tm, d), lambda i: (i, 0)),
        out_shape=jax.ShapeDtypeStruct((t, d), F32),
        scratch_shapes=[pltpu.VMEM((tm * TOP_K,) + ys.shape[1:], F32), pltpu.SemaphoreType.DMA(())],
        compiler_params=_cparams(("arbitrary",)),
        name="moe_combine",
    )(dest_tiles, x1, gw, mod, ys)


def _moe_plan(gi, rk, counts):
    t = gi.shape[0]
    top_i = gi[:, :TOP_K]
    rank = rk[:, :TOP_K]
    cnt = counts[0, :N_EXPERTS].astype(jnp.int32)
    padded = (cnt + MOE_BLOCK - 1) // MOE_BLOCK * MOE_BLOCK
    pad_ends = jnp.cumsum(padded)
    pad_starts = pad_ends - padded
    dest = pad_starts[top_i] + rank
    n_rows = t * TOP_K + N_EXPERTS * MOE_BLOCK
    blk_start = jnp.arange(n_rows // MOE_BLOCK, dtype=jnp.int32) * MOE_BLOCK
    blk_e = jnp.minimum(jnp.sum(blk_start[:, None] >= pad_ends[None, :], axis=1), N_EXPERTS - 1).astype(jnp.int32)
    n_used = (pad_ends[-1:] // MOE_BLOCK).astype(jnp.int32)
    dest_tiles = dest.reshape(t // ROW_TILE, ROW_TILE, TOP_K).transpose(0, 2, 1).reshape(-1).astype(jnp.int32)
    return dest_tiles, blk_e, n_used, n_rows


def _rope_tables(n_lat, n_ctx):
    pos = jnp.arange(n_lat)
    row = (pos // GRID_W).astype(F32)
    col = (pos % GRID_W).astype(F32)
    freq = ROPE_BASE ** (-jnp.arange(ROPE_PAIRS, dtype=F32) / ROPE_PAIRS)
    ra = row[:, None] * freq
    ca = col[:, None] * freq
    cos64 = jnp.concatenate([jnp.cos(ra), jnp.cos(ra), jnp.cos(ca), jnp.cos(ca)], axis=1)
    sin64 = jnp.concatenate([-jnp.sin(ra), jnp.sin(ra), -jnp.sin(ca), jnp.sin(ca)], axis=1)
    cos_t = jnp.concatenate([jnp.tile(cos64, (1, DA_W // DA_DH)), jnp.ones((n_ctx, DA_W), F32)], axis=0)
    sin_t = jnp.concatenate([jnp.tile(sin64, (1, DA_W // DA_DH)), jnp.zeros((n_ctx, DA_W), F32)], axis=0)
    return cos_t, sin_t


def kernel(x, c, ctx, c_ctx, ada_w, ada_b, norm1_g, norm2_g, w_in, qn_g, kn_g, lam_qk, subln_g, hg_lb,
           hg_norm_g, w_branch_a, w_branch_b, w_out, router_w, router_b, w_gu, b_gu, w_down, b_down):
    bsz, n_lat, d = x.shape
    n_ctx = ctx.shape[1]
    depth = ada_w.shape[0]
    assert bsz == 1 and d == D_MODEL and n_lat % 512 == 0 and n_ctx % ROW_TILE == 0
    t = n_lat + n_ctx

    xx = jnp.concatenate([x[0], ctx[0]], axis=0)
    cc = jnp.zeros((8, d), F32).at[0].set(c[0]).at[1].set(c_ctx)
    mods = _modulation(cc, ada_w, ada_b)

    cos_t, sin_t = _rope_tables(n_lat, n_ctx)
    lane = jnp.arange(DA_W)
    seg64 = (lane[:, None] // DA_DH == lane[None, :] // DA_DH).astype(BF16)
    cs = jnp.cumsum(jax.nn.softmax(hg_lb.astype(F32), axis=1), axis=1)
    lb_all = cs - cs[:, :1]
    rw_pad = jnp.zeros((depth, d, 128), F32).at[:, :, :N_EXPERTS].set(router_w)
    rb_pad = jnp.zeros((depth, 1, 128), F32).at[:, 0, :N_EXPERTS].set(router_b)

    tk = 1280 if t % 1280 == 0 else ROW_TILE
    for l in range(depth):
        lam_init = 0.8 - 0.6 * math.exp(-0.3 * l)
        mod = mods[l]
        p = _inproj(xx, norm1_g[l][None], mod, w_in[l].astype(BF16), n_lat)
        k_r, q_t, v_t, qn2, kn2_tiles = _prep(p, cos_t, sin_t, seg64, jnp.tile(qn_g[l], DA_W // DA_DH)[None],
                                              jnp.tile(kn_g[l], DA_W // DA_DH)[None])
        oa = _attention_all(q_t, k_r, v_t, qn2, kn2_tiles, lam_qk[l], subln_g[l][:, None], lam_init,
                            n_lat, n_ctx, tk)
        o_f, o_b = _hgrn(p, lb_all[0, l][None], lb_all[1, l][None], n_lat)
        x1, h2, gw, gi, rk, counts = _post(
            xx, oa, o_f, o_b, p, mod, jnp.tile(hg_norm_g[l], HG_HEADS)[None], norm2_g[l][None],
            w_branch_a[l].astype(BF16), w_branch_b[l].astype(BF16), w_out[l].astype(BF16),
            rw_pad[l], rb_pad[l], n_lat)
        dest_tiles, blk_e, n_used, n_rows = _moe_plan(gi, rk, counts)
        xs = _dispatch(h2, dest_tiles, jnp.zeros((n_rows,) + h2.shape[1:], F32))
        ys = _experts(xs, blk_e, n_used, w_gu, b_gu[:, :, None, :], w_down, b_down[:, :, None, :], l)
        xx = _combine(x1, gw, mod, ys, dest_tiles, n_lat)
    return xx[:n_lat][None]
```

```python
import functools
import math

import jax
import jax.numpy as jnp
from jax import lax
from jax.experimental import pallas as pl
from jax.experimental.pallas import tpu as pltpu

F32 = jnp.float32
BF16 = jnp.bfloat16

D_MODEL = 1024
GRID_W = 64
EPS = 1e-6

DA_HEADS = 4
DA_DH = 64
DA_DV = 2 * DA_DH
DA_W = DA_HEADS * DA_DV
DA_SCALE = DA_DH ** -0.5
ROPE_PAIRS = DA_DH // 4
ROPE_BASE = 10000.0

HG_HEADS = 4
HG_DK = 128
HG_W = HG_HEADS * HG_DK
HG_CHUNK = 64

N_EXPERTS = 32
TOP_K = 4
D_FF = 1024
SWIGLU_ALPHA = 1.702
SWIGLU_LIMIT = 7.0
MOE_BLOCK = 256

IN_COLS = 6144
ROW_TILE = 256
VT_ROWS = DA_DV + 16
NEG_BIG = -1e30
LOG2E = 1.4426950408889634

VMEM_LIMIT = 56 * 1024 * 1024
ATT_TQ = 512
DMA_QUEUES = 2
SCORE_BOUND_MAX = 40.0
SCORE_BOUND_SLACK = 1.01


def _cparams(sem):
    return pltpu.CompilerParams(dimension_semantics=sem, vmem_limit_bytes=VMEM_LIMIT)


def _split3(x):
    a = x.astype(BF16)
    r = x - a.astype(F32)
    b = r.astype(BF16)
    c = (r - b.astype(F32)).astype(BF16)
    return a, b, c


def _dot(a, b):
    return jnp.dot(a, b, preferred_element_type=F32)


def _dot_nt(a, b):
    return lax.dot_general(a, b, (((1,), (1,)), ((), ())), preferred_element_type=F32)


def _dot_tn(a, b):
    return lax.dot_general(a, b, (((0,), (0,)), ((), ())), preferred_element_type=F32)


def _sigmoid(x):
    return 1.0 / (1.0 + jnp.exp(-x))


def _store_row_tiles(ref, val):
    for s in range(ref.shape[1]):
        ref[:, s, :] = val[:, s * 128:(s + 1) * 128]


def _load_row_tiles(ref, lo, n):
    return jnp.concatenate([ref[lo:lo + n, s, :] for s in range(ref.shape[1])], axis=1)


def _mod_kernel(c_ref, w_ref, b_ref, o_ref):
    cv = c_ref[...]
    a = cv * _sigmoid(cv)
    w = w_ref[0]
    a1, a2, a3 = _split3(a)
    w1, w2, w3 = _split3(w)
    acc = _dot(a1, w1) + (_dot(a1, w2) + _dot(a2, w1)) + (_dot(a2, w2) + _dot(a1, w3) + _dot(a3, w1))
    o_ref[0] = acc + b_ref[0]


def _modulation(cc, ada_w, ada_b):
    depth, d, n = ada_w.shape
    tn = 1536
    return pl.pallas_call(
        _mod_kernel,
        grid=(depth, n // tn),
        in_specs=[
            pl.BlockSpec((8, d), lambda l, j: (0, 0)),
            pl.BlockSpec((1, d, tn), lambda l, j: (l, 0, j)),
            pl.BlockSpec((1, 1, tn), lambda l, j: (l, 0, j)),
        ],
        out_specs=pl.BlockSpec((1, 8, tn), lambda l, j: (l, 0, j)),
        out_shape=jax.ShapeDtypeStruct((depth, 8, n), F32),
        compiler_params=_cparams(("parallel", "parallel")),
        name="adaln_mod",
    )(cc, ada_w, ada_b.reshape(depth, 1, n))


def _inproj_kernel(x_ref, g_ref, mod_ref, w_ref, o_ref, h_scr, *, n_lat, tm):
    i = pl.program_id(0)
    j = pl.program_id(1)
    d = x_ref.shape[1]

    @pl.when(j == 0)
    def _():
        x = x_ref[...]
        y = x * lax.rsqrt(jnp.mean(x * x, axis=-1, keepdims=True) + EPS) * g_ref[...]
        row = i * tm + lax.broadcasted_iota(jnp.int32, (tm, 1), 0)
        lat = row < n_lat
        sh = jnp.where(lat, mod_ref[0:1, 0:d], mod_ref[1:2, 0:d])
        sc = jnp.where(lat, mod_ref[0:1, d:2 * d], mod_ref[1:2, d:2 * d])
        h_scr[...] = (y * (1.0 + sc) + sh).astype(BF16)

    o_ref[...] = _dot(h_scr[...], w_ref[...])


def _inproj(x, g, mod, w_bf, n_lat):
    t, d = x.shape
    n = w_bf.shape[1]
    tm = 1280 if t % 1280 == 0 else ROW_TILE
    tn = 1536
    return pl.pallas_call(
        functools.partial(_inproj_kernel, n_lat=n_lat, tm=tm),
        grid=(t // tm, n // tn),
        in_specs=[
            pl.BlockSpec((tm, d), lambda i, j: (i, 0)),
            pl.BlockSpec((1, d), lambda i, j: (0, 0)),
            pl.BlockSpec((8, mod.shape[1]), lambda i, j: (0, 0)),
            pl.BlockSpec((d, tn), lambda i, j: (0, j)),
        ],
        out_specs=pl.BlockSpec((tm, tn), lambda i, j: (i, j)),
        out_shape=jax.ShapeDtypeStruct((t, n), F32),
        scratch_shapes=[pltpu.VMEM((tm, d), BF16)],
        compiler_params=_cparams(("parallel", "arbitrary")),
        name="inproj",
    )(x, g, mod, w_bf)


def _segment_mean_sq(x, seg_ref, width):
    x2 = x * x
    hi = x2.astype(BF16)
    lo = (x2 - hi.astype(F32)).astype(BF16)
    seg = seg_ref[...]
    return (_dot(hi, seg) + _dot(lo, seg)) * (1.0 / width)


def _prep_kernel(q_ref, k_ref, v_ref, cos_ref, sin_ref, seg_ref, qg_ref, kg_ref,
                 k_out, qt_out, vt_out, qn_out, kn_out):
    tm = q_ref.shape[0]
    cos = cos_ref[...]
    sin = sin_ref[...]
    lane = lax.broadcasted_iota(jnp.int32, (tm, DA_W), 1)
    first_half = (lane & 31) < 16

    def norm_rope(x, g):
        y = x * lax.rsqrt(_segment_mean_sq(x, seg_ref, DA_DH) + EPS) * g
        fwd = pltpu.roll(y, DA_W - 16, 1)
        bwd = pltpu.roll(y, 16, 1)
        partner = jnp.where(first_half, fwd, bwd)
        return y * cos + partner * sin

    k_b = norm_rope(k_ref[...], kg_ref[...]).astype(BF16)
    k_out[...] = k_b
    k_f = k_b.astype(F32)
    k_n2 = _dot((k_f * k_f).astype(BF16), seg_ref[...])
    kn_out[...] = jnp.broadcast_to(jnp.max(k_n2, axis=0, keepdims=True), kn_out.shape)

    q = norm_rope(q_ref[...], qg_ref[...]) * (DA_SCALE * LOG2E)
    lane_h = lax.broadcasted_iota(jnp.int32, (tm, DA_DV), 1)
    v = v_ref[...]
    ones = jnp.ones((VT_ROWS - DA_DV, tm), BF16)
    for h in range(DA_HEADS):
        qh = q[:, h * DA_DV:(h + 1) * DA_DV]
        for m in range(2):
            keep = (lane_h < DA_DH) if m == 0 else (lane_h >= DA_DH)
            r = 2 * h + m
            qt_b = jnp.where(keep, qh, 0.0).T.astype(BF16)
            qt_out[r * DA_DV:(r + 1) * DA_DV, :] = qt_b
            qt_f = qt_b.astype(F32)
            qn_out[r:r + 1, :] = jnp.sum(qt_f * qt_f, axis=0, keepdims=True)
        vt_out[h * VT_ROWS:h * VT_ROWS + DA_DV, :] = v[:, h * DA_DV:(h + 1) * DA_DV].T.astype(BF16)
        vt_out[h * VT_ROWS + DA_DV:(h + 1) * VT_ROWS, :] = ones


def _prep(p, cos_t, sin_t, seg64, qg, kg):
    t = p.shape[0]
    tm = ROW_TILE
    return pl.pallas_call(
        _prep_kernel,
        grid=(t // tm,),
        in_specs=[
            pl.BlockSpec((tm, DA_W), lambda i: (i, 0)),
            pl.BlockSpec((tm, DA_W), lambda i: (i, 1)),
            pl.BlockSpec((tm, DA_W), lambda i: (i, 2)),
            pl.BlockSpec((tm, DA_W), lambda i: (i, 0)),
            pl.BlockSpec((tm, DA_W), lambda i: (i, 0)),
            pl.BlockSpec((DA_W, DA_W), lambda i: (0, 0)),
            pl.BlockSpec((1, DA_W), lambda i: (0, 0)),
            pl.BlockSpec((1, DA_W), lambda i: (0, 0)),
        ],
        out_specs=[
            pl.BlockSpec((tm, DA_W), lambda i: (i, 0)),
            pl.BlockSpec((2 * DA_HEADS * DA_DV, tm), lambda i: (0, i)),
            pl.BlockSpec((DA_HEADS * VT_ROWS, tm), lambda i: (0, i)),
            pl.BlockSpec((2 * DA_HEADS, tm), lambda i: (0, i)),
            pl.BlockSpec((8, DA_W), lambda i: (i, 0)),
        ],
        out_shape=[
            jax.ShapeDtypeStruct((t, DA_W), BF16),
            jax.ShapeDtypeStruct((2 * DA_HEADS * DA_DV, t), BF16),
            jax.ShapeDtypeStruct((DA_HEADS * VT_ROWS, t), BF16),
            jax.ShapeDtypeStruct((2 * DA_HEADS, t), F32),
            jax.ShapeDtypeStruct((8 * (t // tm), DA_W), F32),
        ],
        compiler_params=_cparams(("parallel",)),
        name="qkv_prep",
    )(p, p, p, cos_t, sin_t, seg64, qg, kg)


def _attn_finalize(lam_ref, sg_ref, o_ref, acc_scr, lam_init):
    lq = lam_ref[...]
    lam = (jnp.exp(jnp.sum(lq[0:1] * lq[1:2], axis=-1, keepdims=True))
           - jnp.exp(jnp.sum(lq[2:3] * lq[3:4], axis=-1, keepdims=True)) + lam_init)
    for h in range(DA_HEADS):
        a1 = acc_scr[2 * h]
        a2 = acc_scr[2 * h + 1]
        o = a1[0:DA_DV] / a1[DA_DV:DA_DV + 1] - lam * (a2[0:DA_DV] / a2[DA_DV:DA_DV + 1])
        o = o * lax.rsqrt(jnp.mean(o * o, axis=0, keepdims=True) + EPS)
        o = o * sg_ref[...] * (1.0 - lam_init)
        o_ref[:, h * DA_DV:(h + 1) * DA_DV] = o.T.astype(o_ref.dtype)


def _attn_kernel(qt_ref, k_ref, vt_ref, mref_ref, lam_ref, sg_ref, o_ref, m_scr, acc_scr, *, lam_init):
    del mref_ref
    kk = pl.program_id(1)
    nk = pl.num_programs(1)

    @pl.when(kk == 0)
    def _():
        m_scr[...] = jnp.full(m_scr.shape, NEG_BIG, F32)
        acc_scr[...] = jnp.zeros(acc_scr.shape, F32)

    for h in range(DA_HEADS):
        k_h = k_ref[:, h * DA_DV:(h + 1) * DA_DV]
        vt_h = vt_ref[h * VT_ROWS:(h + 1) * VT_ROWS, :]
        for m in range(2):
            r = 2 * h + m
            s = _dot(k_h, qt_ref[r * DA_DV:(r + 1) * DA_DV, :])
            m_prev = m_scr[r:r + 1, :]
            m_new = jnp.maximum(m_prev, jnp.max(s, axis=0, keepdims=True))
            alpha = jnp.exp2(m_prev - m_new)
            p = jnp.exp2((s - m_new).astype(BF16))
            acc_scr[r] = alpha * acc_scr[r] + _dot(vt_h, p)
            m_scr[r:r + 1, :] = m_new

    @pl.when(kk == nk - 1)
    def _():
        _attn_finalize(lam_ref, sg_ref, o_ref, acc_scr, lam_init)


def _attn_fixed_kernel(qt_ref, k_ref, vt_ref, mref_ref, lam_ref, sg_ref, o_ref, acc_scr, *, lam_init):
    kk = pl.program_id(1)
    nk = pl.num_programs(1)

    @pl.when(kk == 0)
    def _():
        acc_scr[...] = jnp.zeros(acc_scr.shape, F32)

    for h in range(DA_HEADS):
        k_h = k_ref[:, h * DA_DV:(h + 1) * DA_DV]
        vt_h = vt_ref[h * VT_ROWS:(h + 1) * VT_ROWS, :]
        for m in range(2):
            r = 2 * h + m
            s = _dot(k_h, qt_ref[r * DA_DV:(r + 1) * DA_DV, :])
            p = jnp.exp2((s - mref_ref[r:r + 1, :]).astype(BF16))
            acc_scr[r] += _dot(vt_h, p)

    @pl.when(kk == nk - 1)
    def _():
        _attn_finalize(lam_ref, sg_ref, o_ref, acc_scr, lam_init)


def _attention(qt, k, vt, mref, lam_qk, subln_col, lam_init, fixed_ref, *, q_start, n_q, k_start, n_k, tq, tk):
    qo, ko = q_start // tq, k_start // tk
    acc = pltpu.VMEM((2 * DA_HEADS, VT_ROWS, tq), F32)
    if fixed_ref:
        body, scratch, name = _attn_fixed_kernel, [acc], "diff_attn_fixed"
    else:
        body, scratch, name = _attn_kernel, [pltpu.VMEM((2 * DA_HEADS, tq), F32), acc], "diff_attn_online"
    return pl.pallas_call(
        functools.partial(body, lam_init=lam_init),
        grid=(n_q // tq, n_k // tk),
        in_specs=[
            pl.BlockSpec((qt.shape[0], tq), lambda i, j: (0, i + qo)),
            pl.BlockSpec((tk, DA_W), lambda i, j: (j + ko, 0)),
            pl.BlockSpec((vt.shape[0], tk), lambda i, j: (0, j + ko)),
            pl.BlockSpec((2 * DA_HEADS, tq), lambda i, j: (0, i + qo)),
            pl.BlockSpec((4, DA_DH), lambda i, j: (0, 0)),
            pl.BlockSpec((DA_DV, 1), lambda i, j: (0, 0)),
        ],
        out_specs=pl.BlockSpec((tq, DA_W), lambda i, j: (i, 0)),
        out_shape=jax.ShapeDtypeStruct((n_q, DA_W), BF16),
        scratch_shapes=scratch,
        compiler_params=_cparams(("parallel", "arbitrary")),
        name=name,
    )(qt, k, vt, mref, lam_qk, subln_col)


def _attention_all(qt, k, vt, qn2, kn2_tiles, lam_qk, subln_col, lam_init, n_lat, n_ctx, tk):
    t = n_lat + n_ctx
    n_maps = 2 * DA_HEADS
    kmax2 = jnp.max(kn2_tiles, axis=0).reshape(n_maps, DA_DH)[:, 0]
    mref = jnp.sqrt(qn2) * jnp.sqrt(kmax2)[:, None] * SCORE_BOUND_SLACK
    bounded = jnp.max(mref) <= SCORE_BOUND_MAX

    def run(fixed_ref):
        def f(args):
            lat = _attention(*args, lam_init, fixed_ref,
                             q_start=0, n_q=n_lat, k_start=0, n_k=t, tq=ATT_TQ, tk=tk)
            ctx = _attention(*args, lam_init, fixed_ref,
                             q_start=n_lat, n_q=n_ctx, k_start=n_lat, n_k=n_ctx, tq=ROW_TILE, tk=ROW_TILE)
            return jnp.concatenate([lat, ctx], axis=0)
        return f

    return lax.cond(bounded, run(True), run(False), (qt, k, vt, mref, lam_qk, subln_col))


def _hg_chunk(q, z, v, lb, st_ref, rev):
    c, w = q.shape
    row = lax.broadcasted_iota(jnp.int32, (c, 1), 0)

    u = jnp.exp(-jnp.abs(z))
    sig_abs = 1.0 / (1.0 + u)
    y = jnp.log(1.0 - lb) + jnp.minimum(z, 0.0) - jnp.log(1.0 + u)
    log_lb = jnp.log(lb)
    log_f = jnp.maximum(log_lb, y) + jnp.log(1.0 + jnp.exp(-jnp.abs(log_lb - y)))
    key = (1.0 - lb) * jnp.where(z >= 0.0, u * sig_abs, sig_abs)

    ri = lax.broadcasted_iota(jnp.int32, (c, c), 0)
    ci = lax.broadcasted_iota(jnp.int32, (c, c), 1)
    tri = jnp.where((ci >= ri) if rev else (ci <= ri), 1.0, 0.0).astype(BF16)
    g1, g2, g3 = _split3(log_f)
    b = _dot(tri, g1) + _dot(tri, g2) + _dot(tri, g3)
    end = 0 if rev else c - 1
    b_end = b[end:end + 1, :]

    q_in = (q * jnp.exp(b)).astype(BF16)
    k_out = (key * jnp.exp(b_end - b)).astype(BF16)
    decay = jnp.exp(b_end)

    up = pltpu.roll(log_f, 1, 0)
    dn = pltpu.roll(log_f, c - 1, 0)
    q_lv, k_lv, masks = [], [], []
    lvl = c // 2
    while lvl >= 1:
        half = (row // lvl) & 1
        q_rows = (half == 0) if rev else (half == 1)
        if lvl >= 4:
            pos = lvl if rev else lvl - 1
            ref = jnp.broadcast_to(b.reshape(c // (2 * lvl), 2 * lvl, w)[:, pos:pos + 1, :],
                                   (c // (2 * lvl), 2 * lvl, w)).reshape(c, w)
            eq, ek = b - ref, ref - b
        elif lvl == 2:
            j = row & 3
            if rev:
                eq = jnp.where(j == 1, log_f, log_f + dn)
                ek = jnp.where(j == 2, 0.0, up)
            else:
                eq = jnp.where(j == 2, log_f, log_f + up)
                ek = jnp.where(j == 1, 0.0, dn)
        else:
            eq, ek = log_f, jnp.zeros_like(log_f)
        q_lv.append((q * jnp.exp(jnp.where(q_rows, eq, NEG_BIG))).astype(BF16))
        k_lv.append((key * jnp.exp(jnp.where(q_rows, NEG_BIG, ek))).astype(BF16))
        masks.append(None if 2 * lvl == c else
                     jnp.where((ri // (2 * lvl)) == (ci // (2 * lvl)), 1.0, 0.0))
        lvl //= 2
    qk = q * key
    on_diag = ri == ci

    outs = []
    for h in range(HG_HEADS):
        sl = slice(h * HG_DK, (h + 1) * HG_DK)
        a = jnp.where(on_diag, jnp.sum(qk[:, sl], axis=-1, keepdims=True), 0.0)
        for q_l, k_l, msk in zip(q_lv, k_lv, masks):
            a_l = _dot_nt(q_l[:, sl], k_l[:, sl])
            a = a + (a_l if msk is None else a_l * msk)
        st = st_ref[h]
        v_h = v[:, sl].astype(BF16)
        outs.append(_dot_nt(q_in[:, sl], st.astype(BF16)) + _dot(a.astype(BF16), v_h))
        st_ref[h] = st * decay[:, sl] + _dot_tn(v_h, k_out[:, sl])
    return jnp.concatenate(outs, axis=1)


def _hgrn_kernel(qf_ref, zf_ref, vf_ref, qb_ref, zb_ref, vb_ref, lbf_ref, lbb_ref,
                 of_ref, ob_ref, sf_scr, sb_scr):
    @pl.when(pl.program_id(0) == 0)
    def _():
        sf_scr[...] = jnp.zeros(sf_scr.shape, F32)
        sb_scr[...] = jnp.zeros(sb_scr.shape, F32)

    n_chunks = qf_ref.shape[0] // HG_CHUNK

    def body(ci, carry):
        rf = pl.ds(pl.multiple_of(ci * HG_CHUNK, HG_CHUNK), HG_CHUNK)
        of_ref[rf, :] = _hg_chunk(qf_ref[rf, :], zf_ref[rf, :], vf_ref[rf, :], lbf_ref[...], sf_scr, False)
        rb = pl.ds(pl.multiple_of((n_chunks - 1 - ci) * HG_CHUNK, HG_CHUNK), HG_CHUNK)
        ob_ref[rb, :] = _hg_chunk(qb_ref[rb, :], zb_ref[rb, :], vb_ref[rb, :], lbb_ref[...], sb_scr, True)
        return carry

    lax.fori_loop(0, n_chunks, body, 0)


def _hgrn(p, lb_f, lb_b, n_lat):
    t = p.shape[0]
    tm = ROW_TILE
    nb = t // tm
    n_lat_b = n_lat // tm
    n_ctx_b = nb - n_lat_b

    def jf(i):
        return jnp.where(i < n_ctx_b, n_lat_b + i, i - n_ctx_b)

    def jb(i):
        return jnp.where(i < n_ctx_b, nb - 1 - i, nb - 1 - i)

    def spec(order, colblk):
        return pl.BlockSpec((tm, HG_W), lambda i: (order(i), colblk))

    return pl.pallas_call(
        _hgrn_kernel,
        grid=(nb,),
        in_specs=[spec(jf, 3), spec(jf, 4), spec(jf, 6), spec(jb, 3), spec(jb, 5), spec(jb, 6),
                  pl.BlockSpec((1, HG_W), lambda i: (0, 0)), pl.BlockSpec((1, HG_W), lambda i: (0, 0))],
        out_specs=[pl.BlockSpec((tm, HG_W), lambda i: (jf(i), 0)),
                   pl.BlockSpec((tm, HG_W), lambda i: (jb(i), 0))],
        out_shape=[jax.ShapeDtypeStruct((t, HG_W), F32), jax.ShapeDtypeStruct((t, HG_W), F32)],
        scratch_shapes=[pltpu.VMEM((HG_HEADS, HG_DK, HG_DK), F32), pltpu.VMEM((HG_HEADS, HG_DK, HG_DK), F32)],
        compiler_params=_cparams(("arbitrary",)),
        name="hgrn2",
    )(p, p, p, p, p, p, lb_f, lb_b)


def _post_kernel(x_ref, oa_ref, of_ref, ob_ref, gh_ref, ga_ref, gb_ref, mod_ref, hg_ref, n2_ref,
                 wa_ref, wb_ref, wo_ref, rw_ref, rb_ref,
                 x1_ref, h2_ref, gw_ref, gi_ref, rk_ref, cnt_ref, run_scr, *, n_lat, tm):
    i = pl.program_id(0)
    d = x_ref.shape[1]

    @pl.when(i == 0)
    def _():
        run_scr[...] = jnp.zeros(run_scr.shape, F32)

    lat = i * tm < n_lat

    def mod(c):
        return jnp.where(lat, mod_ref[0:1, c * d:(c + 1) * d], mod_ref[1:2, c * d:(c + 1) * d])

    o = of_ref[...] + ob_ref[...]
    parts = []
    for h in range(HG_HEADS):
        oh = o[:, h * HG_DK:(h + 1) * HG_DK]
        parts.append(oh * lax.rsqrt(jnp.mean(oh * oh, axis=-1, keepdims=True) + EPS))
    gh = gh_ref[...]
    ob = jnp.concatenate(parts, axis=1) * hg_ref[...] * (gh * _sigmoid(gh))
    ya = _dot(oa_ref[...], wa_ref[...])
    yb = _dot(ob.astype(BF16), wb_ref[...])
    mix = _sigmoid(ga_ref[...]) * ya + _sigmoid(gb_ref[...]) * yb
    x1 = x_ref[...] + mod(2) * _dot(mix.astype(BF16), wo_ref[...])
    x1_ref[...] = x1

    h2 = x1 * lax.rsqrt(jnp.mean(x1 * x1, axis=-1, keepdims=True) + EPS) * n2_ref[...]
    h2 = h2 * (1.0 + mod(4)) + mod(3)
    _store_row_tiles(h2_ref, h2)

    h_hi = h2.astype(BF16)
    h_lo = (h2 - h_hi.astype(F32)).astype(BF16)
    rw = rw_ref[...]
    r_hi = rw.astype(BF16)
    r_lo = (rw - r_hi.astype(F32)).astype(BF16)
    logits = _dot(h_hi, r_hi) + (_dot(h_hi, r_lo) + _dot(h_lo, r_hi)) + rb_ref[...]

    lane = lax.broadcasted_iota(jnp.int32, logits.shape, 1)
    work = jnp.where(lane < N_EXPERTS, logits, -jnp.inf)
    sel_any = jnp.zeros(logits.shape, F32)
    vals, idxs = [], []
    for _ in range(TOP_K):
        mx = jnp.max(work, axis=-1, keepdims=True)
        idx = jnp.min(jnp.where(work == mx, lane, 2 * N_EXPERTS), axis=-1, keepdims=True)
        hit = lane == idx
        vals.append(mx)
        idxs.append(idx)
        sel_any = jnp.where(hit, 1.0, sel_any)
        work = jnp.where(hit, -jnp.inf, work)
    es = [jnp.exp(vk - vals[0]) for vk in vals]
    denom = es[0] + es[1] + es[2] + es[3]

    ri = lax.broadcasted_iota(jnp.int32, (tm, tm), 0)
    ci = lax.broadcasted_iota(jnp.int32, (tm, tm), 1)
    below = jnp.where(ci < ri, 1.0, 0.0).astype(BF16)
    rank_e = run_scr[0:1, :] + _dot(below, sel_any.astype(BF16))
    gw = jnp.zeros(logits.shape, F32)
    gi = jnp.zeros(logits.shape, jnp.int32)
    rk = jnp.zeros(logits.shape, jnp.int32)
    for k in range(TOP_K):
        at_k = lane == k
        gw = jnp.where(at_k, es[k] / denom, gw)
        gi = jnp.where(at_k, idxs[k], gi)
        rnk = jnp.sum(jnp.where(lane == idxs[k], rank_e, 0.0), axis=-1, keepdims=True)
        rk = jnp.where(at_k, rnk.astype(jnp.int32), rk)
    gw_ref[...] = gw
    gi_ref[...] = gi
    rk_ref[...] = rk
    run = run_scr[0:1, :] + jnp.sum(sel_any, axis=0, keepdims=True)
    run_scr[...] = jnp.broadcast_to(run, run_scr.shape)
    cnt_ref[...] = jnp.broadcast_to(run, cnt_ref.shape)


def _post(x, oa, o_f, o_b, p, mod, hg_g, n2_g, wa, wb, wo, rw_pad, rb_pad, n_lat):
    t, d = x.shape
    tm = ROW_TILE

    def rows(width, colblk=0):
        return pl.BlockSpec((tm, width), lambda i: (i, colblk))

    def whole(a):
        return pl.BlockSpec(a.shape, lambda i: (0,) * a.ndim)

    return pl.pallas_call(
        functools.partial(_post_kernel, n_lat=n_lat, tm=tm),
        grid=(t // tm,),
        in_specs=[rows(d), rows(DA_W), rows(HG_W), rows(HG_W),
                  rows(HG_W, 7), rows(d, 4), rows(d, 5),
                  whole(mod), whole(hg_g), whole(n2_g), whole(wa), whole(wb), whole(wo),
                  whole(rw_pad), whole(rb_pad)],
        out_specs=[rows(d), pl.BlockSpec((tm, d // 128, 128), lambda i: (i, 0, 0)), rows(128), rows(128), rows(128),
                   pl.BlockSpec((8, 128), lambda i: (0, 0))],
        out_shape=[jax.ShapeDtypeStruct((t, d), F32), jax.ShapeDtypeStruct((t, d // 128, 128), F32),
                   jax.ShapeDtypeStruct((t, 128), F32), jax.ShapeDtypeStruct((t, 128), jnp.int32),
                   jax.ShapeDtypeStruct((t, 128), jnp.int32), jax.ShapeDtypeStruct((8, 128), F32)],
        scratch_shapes=[pltpu.VMEM((8, 128), F32)],
        compiler_params=_cparams(("arbitrary",)),
        name="merge_router",
    )(x, oa, o_f, o_b, p, p, p, mod, hg_g, n2_g, wa, wb, wo, rw_pad, rb_pad)


def _dispatch_kernel(dest_ref, h_ref, xs_in, xs_out, sem):
    del xs_in
    n = dest_ref.shape[0]
    tm = h_ref.shape[0]

    def start(j, c):
        for q in range(DMA_QUEUES):
            r = DMA_QUEUES * j + q
            pltpu.make_async_copy(h_ref.at[r & (tm - 1)], xs_out.at[dest_ref[r]], sem).start(priority=q)
        return c

    lax.fori_loop(0, n // DMA_QUEUES, start, 0, unroll=4)
    pltpu.make_async_copy(xs_out.at[pl.ds(0, n)], xs_out.at[pl.ds(0, n)], sem).wait()


def _dispatch(h2, dest_tiles, xs_zero):
    t = h2.shape[0]
    tm = ROW_TILE
    return pl.pallas_call(
        _dispatch_kernel,
        grid=(t // tm,),
        in_specs=[pl.BlockSpec((tm * TOP_K,), lambda i: (i,), memory_space=pltpu.SMEM),
                  pl.BlockSpec((tm,) + h2.shape[1:], lambda i: (i, 0, 0)),
                  pl.BlockSpec(memory_space=pl.ANY)],
        out_specs=pl.BlockSpec(memory_space=pl.ANY),
        out_shape=jax.ShapeDtypeStruct(xs_zero.shape, xs_zero.dtype),
        scratch_shapes=[pltpu.SemaphoreType.DMA(())],
        input_output_aliases={2: 0},
        compiler_params=_cparams(("arbitrary",)),
        name="moe_dispatch",
    )(dest_tiles, h2, xs_zero)


def _expert_kernel(be_ref, nx_ref, sl_ref, nu_ref, xs_ref, wgu_hbm, bgu_ref, wd_hbm, bd_ref, ys_ref,
                   wgu_f32, wd_f32, wgu_bf, wd_bf, sems, *, layer):
    i = pl.program_id(0)
    used = i < nu_ref[0]

    def weight_copies(e, slot):
        return (pltpu.make_async_copy(wgu_hbm.at[layer, e], wgu_f32.at[slot], sems.at[0, slot]),
                pltpu.make_async_copy(wd_hbm.at[layer, e], wd_f32.at[slot], sems.at[1, slot]))

    @pl.when(i == 0)
    def _():
        for cp in weight_copies(be_ref[0], 0):
            cp.start()

    @pl.when(jnp.logical_and(used, jnp.logical_or(i == 0, be_ref[i] != be_ref[jnp.maximum(i - 1, 0)])))
    def _():
        slot = sl_ref[i]
        for cp in weight_copies(be_ref[i], slot):
            cp.wait()

        @pl.when(nx_ref[i] >= 0)
        def _():
            for cp in weight_copies(nx_ref[i], 1 - slot):
                cp.start()

        wgu_bf[...] = wgu_f32[slot].astype(BF16)
        wd_bf[...] = wd_f32[slot].astype(BF16)

    @pl.when(used)
    def _():
        x = _load_row_tiles(xs_ref, 0, xs_ref.shape[0]).astype(BF16)
        gu = _dot(x, wgu_bf[...]) + bgu_ref[0, 0]
        gate = jnp.minimum(gu[:, :D_FF], SWIGLU_LIMIT)
        up = jnp.clip(gu[:, D_FF:], -SWIGLU_LIMIT, SWIGLU_LIMIT)
        hdn = (up + 1.0) * gate * _sigmoid(SWIGLU_ALPHA * gate)
        _store_row_tiles(ys_ref, _dot(hdn.astype(BF16), wd_bf[...]) + bd_ref[0, 0])

    @pl.when(jnp.logical_not(used))
    def _():
        ys_ref[...] = jnp.zeros(ys_ref.shape, F32)


def _experts(xs, plan, wgu, bgu, wd, bd, layer):
    blk_e, next_e, slot, n_used = plan
    n_rows = xs.shape[0]
    d = xs.shape[1] * xs.shape[2]
    nblk = n_rows // MOE_BLOCK

    def blk(i, nu):
        return jnp.minimum(i, nu[0] - 1)

    def bias(i, be, nx, sl, nu):
        return (layer, be[blk(i, nu)], 0, 0)

    grid_spec = pltpu.PrefetchScalarGridSpec(
        num_scalar_prefetch=4,
        grid=(nblk,),
        in_specs=[
            pl.BlockSpec((MOE_BLOCK,) + xs.shape[1:], lambda i, be, nx, sl, nu: (blk(i, nu), 0, 0)),
            pl.BlockSpec(memory_space=pl.ANY),
            pl.BlockSpec((1, 1, 1, 2 * D_FF), bias),
            pl.BlockSpec(memory_space=pl.ANY),
            pl.BlockSpec((1, 1, 1, d), bias),
        ],
        out_specs=pl.BlockSpec((MOE_BLOCK,) + xs.shape[1:], lambda i, be, nx, sl, nu: (i, 0, 0)),
        scratch_shapes=[pltpu.VMEM((2, d, 2 * D_FF), F32), pltpu.VMEM((2, D_FF, d), F32),
                        pltpu.VMEM((d, 2 * D_FF), BF16), pltpu.VMEM((D_FF, d), BF16),
                        pltpu.SemaphoreType.DMA((2, 2))],
    )
    return pl.pallas_call(
        functools.partial(_expert_kernel, layer=layer),
        grid_spec=grid_spec,
        out_shape=jax.ShapeDtypeStruct(xs.shape, F32),
        compiler_params=_cparams(("arbitrary",)),
        name="moe_experts",
    )(blk_e, next_e, slot, n_used, xs, wgu, bgu, wd, bd)


def _combine_kernel(dest_ref, x1_ref, gw_ref, mod_ref, ys_ref, o_ref, buf, sem, *, n_lat, tm):
    i = pl.program_id(0)
    d = x1_ref.shape[1]
    n = dest_ref.shape[0]

    def start(j, c):
        for q in range(DMA_QUEUES):
            r = DMA_QUEUES * j + q
            pltpu.make_async_copy(ys_ref.at[dest_ref[r]], buf.at[r], sem).start(priority=q)
        return c

    lax.fori_loop(0, n // DMA_QUEUES, start, 0, unroll=4)
    pltpu.make_async_copy(ys_ref.at[pl.ds(0, n)], buf, sem).wait()

    lat = i * tm < n_lat
    g2 = jnp.where(lat, mod_ref[0:1, 5 * d:6 * d], mod_ref[1:2, 5 * d:6 * d])
    gw = gw_ref[...]
    m = gw[:, 0:1] * _load_row_tiles(buf, 0, tm)
    for k in range(1, TOP_K):
        m = m + gw[:, k:k + 1] * _load_row_tiles(buf, k * tm, tm)
    o_ref[...] = x1_ref[...] + g2 * m


def _combine(x1, gw, mod, ys, dest_tiles, n_lat):
    t, d = x1.shape
    tm = ROW_TILE
    return pl.pallas_call(
        functools.partial(_combine_kernel, n_lat=n_lat, tm=tm),
        grid=(t // tm,),
        in_specs=[pl.BlockSpec((tm * TOP_K,), lambda i: (i,), memory_space=pltpu.SMEM),
                  pl.BlockSpec((tm, d), lambda i: (i, 0)),
                  pl.BlockSpec((tm, 128), lambda i: (i, 0)),
                  pl.BlockSpec(mod.shape, lambda i: (0, 0)),
                  pl.BlockSpec(memory_space=pl.ANY)],
        out_specs=pl.BlockSpec((tm, d), lambda i: (i, 0)),
        out_shape=jax.ShapeDtypeStruct((t, d), F32),
        scratch_shapes=[pltpu.VMEM((tm * TOP_K,) + ys.shape[1:], F32), pltpu.SemaphoreType.DMA(())],
        compiler_params=_cparams(("arbitrary",)),
        name="moe_combine",
    )(dest_tiles, x1, gw, mod, ys)


def _moe_plan(gi, rk, counts):
    t = gi.shape[0]
    top_i = gi[:, :TOP_K]
    rank = rk[:, :TOP_K]
    cnt = counts[0, :N_EXPERTS].astype(jnp.int32)
    padded = (cnt + MOE_BLOCK - 1) // MOE_BLOCK * MOE_BLOCK
    pad_ends = jnp.cumsum(padded)
    pad_starts = pad_ends - padded
    dest = pad_starts[top_i] + rank
    n_rows = t * TOP_K + N_EXPERTS * MOE_BLOCK
    blk_start = jnp.arange(n_rows // MOE_BLOCK, dtype=jnp.int32) * MOE_BLOCK
    blk_e = jnp.minimum(jnp.sum(blk_start[:, None] >= pad_ends[None, :], axis=1), N_EXPERTS - 1).astype(jnp.int32)
    n_used = (pad_ends[-1:] // MOE_BLOCK).astype(jnp.int32)
    has_rows = cnt > 0
    later = jnp.where(has_rows[None, :] & (jnp.arange(N_EXPERTS)[None, :] > jnp.arange(N_EXPERTS)[:, None]),
                      jnp.arange(N_EXPERTS)[None, :], N_EXPERTS)
    next_of = jnp.min(later, axis=1)
    next_of = jnp.where(next_of < N_EXPERTS, next_of, -1).astype(jnp.int32)
    run_of = (jnp.cumsum(has_rows.astype(jnp.int32)) - 1).astype(jnp.int32)
    next_e = next_of[blk_e]
    slot = run_of[blk_e] & 1
    dest_tiles = dest.reshape(t // ROW_TILE, ROW_TILE, TOP_K).transpose(0, 2, 1).reshape(-1).astype(jnp.int32)
    return dest_tiles, (blk_e, next_e, slot, n_used), n_rows


def _rope_tables(n_lat, n_ctx):
    pos = jnp.arange(n_lat)
    row = (pos // GRID_W).astype(F32)
    col = (pos % GRID_W).astype(F32)
    freq = ROPE_BASE ** (-jnp.arange(ROPE_PAIRS, dtype=F32) / ROPE_PAIRS)
    ra = row[:, None] * freq
    ca = col[:, None] * freq
    cos64 = jnp.concatenate([jnp.cos(ra), jnp.cos(ra), jnp.cos(ca), jnp.cos(ca)], axis=1)
    sin64 = jnp.concatenate([-jnp.sin(ra), jnp.sin(ra), -jnp.sin(ca), jnp.sin(ca)], axis=1)
    cos_t = jnp.concatenate([jnp.tile(cos64, (1, DA_W // DA_DH)), jnp.ones((n_ctx, DA_W), F32)], axis=0)
    sin_t = jnp.concatenate([jnp.tile(sin64, (1, DA_W // DA_DH)), jnp.zeros((n_ctx, DA_W), F32)], axis=0)
    return cos_t, sin_t


def kernel(x, c, ctx, c_ctx, ada_w, ada_b, norm1_g, norm2_g, w_in, qn_g, kn_g, lam_qk, subln_g, hg_lb,
           hg_norm_g, w_branch_a, w_branch_b, w_out, router_w, router_b, w_gu, b_gu, w_down, b_down):
    bsz, n_lat, d = x.shape
    n_ctx = ctx.shape[1]
    depth = ada_w.shape[0]
    assert bsz == 1 and d == D_MODEL and n_lat % 512 == 0 and n_ctx % ROW_TILE == 0
    t = n_lat + n_ctx

    xx = jnp.concatenate([x[0], ctx[0]], axis=0)
    cc = jnp.zeros((8, d), F32).at[0].set(c[0]).at[1].set(c_ctx)
    mods = _modulation(cc, ada_w, ada_b)

    cos_t, sin_t = _rope_tables(n_lat, n_ctx)
    lane = jnp.arange(DA_W)
    seg64 = (lane[:, None] // DA_DH == lane[None, :] // DA_DH).astype(BF16)
    cs = jnp.cumsum(jax.nn.softmax(hg_lb.astype(F32), axis=1), axis=1)
    lb_all = cs - cs[:, :1]
    rw_pad = jnp.zeros((depth, d, 128), F32).at[:, :, :N_EXPERTS].set(router_w)
    rb_pad = jnp.zeros((depth, 1, 128), F32).at[:, 0, :N_EXPERTS].set(router_b)

    tk = 1280 if t % 1280 == 0 else ROW_TILE
    xs = None
    for l in range(depth):
        lam_init = 0.8 - 0.6 * math.exp(-0.3 * l)
        mod = mods[l]
        p = _inproj(xx, norm1_g[l][None], mod, w_in[l].astype(BF16), n_lat)
        k_r, q_t, v_t, qn2, kn2_tiles = _prep(p, cos_t, sin_t, seg64, jnp.tile(qn_g[l], DA_W // DA_DH)[None],
                                              jnp.tile(kn_g[l], DA_W // DA_DH)[None])
        oa = _attention_all(q_t, k_r, v_t, qn2, kn2_tiles, lam_qk[l], subln_g[l][:, None], lam_init,
                            n_lat, n_ctx, tk)
        o_f, o_b = _hgrn(p, lb_all[0, l][None], lb_all[1, l][None], n_lat)
        x1, h2, gw, gi, rk, counts = _post(
            xx, oa, o_f, o_b, p, mod, jnp.tile(hg_norm_g[l], HG_HEADS)[None], norm2_g[l][None],
            w_branch_a[l].astype(BF16), w_branch_b[l].astype(BF16), w_out[l].astype(BF16),
            rw_pad[l], rb_pad[l], n_lat)
        dest_tiles, plan, n_rows = _moe_plan(gi, rk, counts)
        xs = _dispatch(h2, dest_tiles, jnp.zeros((n_rows,) + h2.shape[1:], F32) if xs is None else xs)
        ys = _experts(xs, plan, w_gu, b_gu[:, :, None, :], w_down, b_down[:, :, None, :], l)
        xx = _combine(x1, gw, mod, ys, dest_tiles, n_lat)
    return xx[:n_lat][None]
```

```python
import functools
import math

import jax
import jax.numpy as jnp
from jax import lax
from jax.experimental import pallas as pl
from jax.experimental.pallas import tpu as pltpu

F32 = jnp.float32
BF16 = jnp.bfloat16
QK_DTYPE = jnp.float8_e4m3fn

D_MODEL = 1024
GRID_W = 64
EPS = 1e-6

DA_HEADS = 4
DA_DH = 64
DA_DV = 2 * DA_DH
DA_W = DA_HEADS * DA_DV
DA_SCALE = DA_DH ** -0.5
ROPE_PAIRS = DA_DH // 4
ROPE_BASE = 10000.0

HG_HEADS = 4
HG_DK = 128
HG_W = HG_HEADS * HG_DK
HG_CHUNK = 64

N_EXPERTS = 32
TOP_K = 4
D_FF = 1024
SWIGLU_ALPHA = 1.702
SWIGLU_LIMIT = 7.0
MOE_BLOCK = 256

IN_COLS = 6144
ROW_TILE = 256
VT_ROWS = DA_DV + 16
NEG_BIG = -1e30
LOG2E = 1.4426950408889634

VMEM_LIMIT = 56 * 1024 * 1024
ATT_TQ = 512
DMA_QUEUES = 2
SCORE_BOUND_MAX = 40.0
SCORE_BOUND_SLACK = 1.01


def _cparams(sem):
    return pltpu.CompilerParams(dimension_semantics=sem, vmem_limit_bytes=VMEM_LIMIT)


def _split3(x):
    a = x.astype(BF16)
    r = x - a.astype(F32)
    b = r.astype(BF16)
    c = (r - b.astype(F32)).astype(BF16)
    return a, b, c


def _dot(a, b):
    return jnp.dot(a, b, preferred_element_type=F32)


def _dot_nt(a, b):
    return lax.dot_general(a, b, (((1,), (1,)), ((), ())), preferred_element_type=F32)


def _dot_tn(a, b):
    return lax.dot_general(a, b, (((0,), (0,)), ((), ())), preferred_element_type=F32)


def _sigmoid(x):
    return 1.0 / (1.0 + jnp.exp(-x))


def _store_row_tiles(ref, val):
    for s in range(ref.shape[1]):
        ref[:, s, :] = val[:, s * 128:(s + 1) * 128]


def _load_row_tiles(ref, lo, n):
    return jnp.concatenate([ref[lo:lo + n, s, :] for s in range(ref.shape[1])], axis=1)


def _mod_kernel(c_ref, w_ref, b_ref, o_ref):
    cv = c_ref[...]
    a = cv * _sigmoid(cv)
    w = w_ref[0]
    a1, a2, a3 = _split3(a)
    w1, w2, w3 = _split3(w)
    acc = _dot(a1, w1) + (_dot(a1, w2) + _dot(a2, w1)) + (_dot(a2, w2) + _dot(a1, w3) + _dot(a3, w1))
    o_ref[0] = acc + b_ref[0]


def _modulation(cc, ada_w, ada_b):
    depth, d, n = ada_w.shape
    tn = 1536
    return pl.pallas_call(
        _mod_kernel,
        grid=(depth, n // tn),
        in_specs=[
            pl.BlockSpec((8, d), lambda l, j: (0, 0)),
            pl.BlockSpec((1, d, tn), lambda l, j: (l, 0, j)),
            pl.BlockSpec((1, 1, tn), lambda l, j: (l, 0, j)),
        ],
        out_specs=pl.BlockSpec((1, 8, tn), lambda l, j: (l, 0, j)),
        out_shape=jax.ShapeDtypeStruct((depth, 8, n), F32),
        compiler_params=_cparams(("parallel", "parallel")),
        name="adaln_mod",
    )(cc, ada_w, ada_b.reshape(depth, 1, n))


def _inproj_kernel(x_ref, g_ref, mod_ref, w_ref, o_ref, h_scr, *, n_lat, tm):
    i = pl.program_id(0)
    j = pl.program_id(1)
    d = x_ref.shape[1]

    @pl.when(j == 0)
    def _():
        x = x_ref[...]
        y = x * lax.rsqrt(jnp.mean(x * x, axis=-1, keepdims=True) + EPS) * g_ref[...]
        row = i * tm + lax.broadcasted_iota(jnp.int32, (tm, 1), 0)
        lat = row < n_lat
        sh = jnp.where(lat, mod_ref[0:1, 0:d], mod_ref[1:2, 0:d])
        sc = jnp.where(lat, mod_ref[0:1, d:2 * d], mod_ref[1:2, d:2 * d])
        h_scr[...] = (y * (1.0 + sc) + sh).astype(BF16)

    o_ref[...] = _dot(h_scr[...], w_ref[...])


def _inproj(x, g, mod, w_bf, n_lat):
    t, d = x.shape
    n = w_bf.shape[1]
    tm = 1280 if t % 1280 == 0 else ROW_TILE
    tn = 1536
    return pl.pallas_call(
        functools.partial(_inproj_kernel, n_lat=n_lat, tm=tm),
        grid=(t // tm, n // tn),
        in_specs=[
            pl.BlockSpec((tm, d), lambda i, j: (i, 0)),
            pl.BlockSpec((1, d), lambda i, j: (0, 0)),
            pl.BlockSpec((8, mod.shape[1]), lambda i, j: (0, 0)),
            pl.BlockSpec((d, tn), lambda i, j: (0, j)),
        ],
        out_specs=pl.BlockSpec((tm, tn), lambda i, j: (i, j)),
        out_shape=jax.ShapeDtypeStruct((t, n), F32),
        scratch_shapes=[pltpu.VMEM((tm, d), BF16)],
        compiler_params=_cparams(("parallel", "arbitrary")),
        name="inproj",
    )(x, g, mod, w_bf)


def _segment_mean_sq(x, seg_ref, width):
    x2 = x * x
    hi = x2.astype(BF16)
    lo = (x2 - hi.astype(F32)).astype(BF16)
    seg = seg_ref[...]
    return (_dot(hi, seg) + _dot(lo, seg)) * (1.0 / width)


def _prep_kernel(q_ref, k_ref, v_ref, cos_ref, sin_ref, seg_ref, qg_ref, kg_ref,
                 k_out, qt_out, vt_out, qn_out, kn_out):
    tm = q_ref.shape[0]
    cos = cos_ref[...]
    sin = sin_ref[...]
    lane = lax.broadcasted_iota(jnp.int32, (tm, DA_W), 1)
    first_half = (lane & 31) < 16

    def norm_rope(x, g):
        y = x * lax.rsqrt(_segment_mean_sq(x, seg_ref, DA_DH) + EPS) * g
        fwd = pltpu.roll(y, DA_W - 16, 1)
        bwd = pltpu.roll(y, 16, 1)
        partner = jnp.where(first_half, fwd, bwd)
        return y * cos + partner * sin

    k_b = norm_rope(k_ref[...], kg_ref[...]).astype(QK_DTYPE)
    k_out[...] = k_b
    k_f = k_b.astype(F32)
    k_n2 = _dot((k_f * k_f).astype(BF16), seg_ref[...])
    kn_out[...] = jnp.broadcast_to(jnp.max(k_n2, axis=0, keepdims=True), kn_out.shape)

    q = norm_rope(q_ref[...], qg_ref[...]) * (DA_SCALE * LOG2E)
    lane_h = lax.broadcasted_iota(jnp.int32, (tm, DA_DV), 1)
    v = v_ref[...]
    ones = jnp.ones((VT_ROWS - DA_DV, tm), BF16)
    for h in range(DA_HEADS):
        qh = q[:, h * DA_DV:(h + 1) * DA_DV]
        for m in range(2):
            keep = (lane_h < DA_DH) if m == 0 else (lane_h >= DA_DH)
            r = 2 * h + m
            qt_b = jnp.where(keep, qh, 0.0).T.astype(QK_DTYPE)
            qt_out[r * DA_DV:(r + 1) * DA_DV, :] = qt_b
            qt_f = qt_b.astype(F32)
            qn_out[r:r + 1, :] = jnp.sum(qt_f * qt_f, axis=0, keepdims=True)
        vt_out[h * VT_ROWS:h * VT_ROWS + DA_DV, :] = v[:, h * DA_DV:(h + 1) * DA_DV].T.astype(BF16)
        vt_out[h * VT_ROWS + DA_DV:(h + 1) * VT_ROWS, :] = ones


def _prep(p, cos_t, sin_t, seg64, qg, kg):
    t = p.shape[0]
    tm = ROW_TILE
    return pl.pallas_call(
        _prep_kernel,
        grid=(t // tm,),
        in_specs=[
            pl.BlockSpec((tm, DA_W), lambda i: (i, 0)),
            pl.BlockSpec((tm, DA_W), lambda i: (i, 1)),
            pl.BlockSpec((tm, DA_W), lambda i: (i, 2)),
            pl.BlockSpec((tm, DA_W), lambda i: (i, 0)),
            pl.BlockSpec((tm, DA_W), lambda i: (i, 0)),
            pl.BlockSpec((DA_W, DA_W), lambda i: (0, 0)),
            pl.BlockSpec((1, DA_W), lambda i: (0, 0)),
            pl.BlockSpec((1, DA_W), lambda i: (0, 0)),
        ],
        out_specs=[
            pl.BlockSpec((tm, DA_W), lambda i: (i, 0)),
            pl.BlockSpec((2 * DA_HEADS * DA_DV, tm), lambda i: (0, i)),
            pl.BlockSpec((DA_HEADS * VT_ROWS, tm), lambda i: (0, i)),
            pl.BlockSpec((2 * DA_HEADS, tm), lambda i: (0, i)),
            pl.BlockSpec((8, DA_W), lambda i: (i, 0)),
        ],
        out_shape=[
            jax.ShapeDtypeStruct((t, DA_W), QK_DTYPE),
            jax.ShapeDtypeStruct((2 * DA_HEADS * DA_DV, t), QK_DTYPE),
            jax.ShapeDtypeStruct((DA_HEADS * VT_ROWS, t), BF16),
            jax.ShapeDtypeStruct((2 * DA_HEADS, t), F32),
            jax.ShapeDtypeStruct((8 * (t // tm), DA_W), F32),
        ],
        compiler_params=_cparams(("parallel",)),
        name="qkv_prep",
    )(p, p, p, cos_t, sin_t, seg64, qg, kg)


def _attn_finalize(lam_ref, sg_ref, o_ref, acc_scr, lam_init):
    lq = lam_ref[...]
    lam = (jnp.exp(jnp.sum(lq[0:1] * lq[1:2], axis=-1, keepdims=True))
           - jnp.exp(jnp.sum(lq[2:3] * lq[3:4], axis=-1, keepdims=True)) + lam_init)
    for h in range(DA_HEADS):
        a1 = acc_scr[2 * h]
        a2 = acc_scr[2 * h + 1]
        o = a1[0:DA_DV] / a1[DA_DV:DA_DV + 1] - lam * (a2[0:DA_DV] / a2[DA_DV:DA_DV + 1])
        o = o * lax.rsqrt(jnp.mean(o * o, axis=0, keepdims=True) + EPS)
        o = o * sg_ref[...] * (1.0 - lam_init)
        o_ref[:, h * DA_DV:(h + 1) * DA_DV] = o.T.astype(o_ref.dtype)


def _attn_kernel(qt_ref, k_ref, vt_ref, mref_ref, lam_ref, sg_ref, o_ref, m_scr, acc_scr, *, lam_init):
    del mref_ref
    kk = pl.program_id(1)
    nk = pl.num_programs(1)

    @pl.when(kk == 0)
    def _():
        m_scr[...] = jnp.full(m_scr.shape, NEG_BIG, F32)
        acc_scr[...] = jnp.zeros(acc_scr.shape, F32)

    for h in range(DA_HEADS):
        k_h = k_ref[:, h * DA_DV:(h + 1) * DA_DV]
        vt_h = vt_ref[h * VT_ROWS:(h + 1) * VT_ROWS, :]
        for m in range(2):
            r = 2 * h + m
            s = _dot(k_h, qt_ref[r * DA_DV:(r + 1) * DA_DV, :])
            m_prev = m_scr[r:r + 1, :]
            m_new = jnp.maximum(m_prev, jnp.max(s, axis=0, keepdims=True))
            alpha = jnp.exp2(m_prev - m_new)
            p = jnp.exp2((s - m_new).astype(BF16))
            acc_scr[r] = alpha * acc_scr[r] + _dot(vt_h, p)
            m_scr[r:r + 1, :] = m_new

    @pl.when(kk == nk - 1)
    def _():
        _attn_finalize(lam_ref, sg_ref, o_ref, acc_scr, lam_init)


def _attn_fixed_kernel(qt_ref, k_ref, vt_ref, mref_ref, lam_ref, sg_ref, o_ref, acc_scr, *, lam_init):
    del mref_ref
    kk = pl.program_id(1)
    nk = pl.num_programs(1)

    @pl.when(kk == 0)
    def _():
        acc_scr[...] = jnp.zeros(acc_scr.shape, F32)

    for h in range(DA_HEADS):
        k_h = k_ref[:, h * DA_DV:(h + 1) * DA_DV]
        vt_h = vt_ref[h * VT_ROWS:(h + 1) * VT_ROWS, :]
        for m in range(2):
            r = 2 * h + m
            s = _dot(k_h, qt_ref[r * DA_DV:(r + 1) * DA_DV, :])
            acc_scr[r] += _dot(vt_h, jnp.exp2(s.astype(BF16)))

    @pl.when(kk == nk - 1)
    def _():
        _attn_finalize(lam_ref, sg_ref, o_ref, acc_scr, lam_init)


def _attention(qt, k, vt, mref, lam_qk, subln_col, lam_init, fixed_ref, *, q_start, n_q, k_start, n_k, tq, tk):
    qo, ko = q_start // tq, k_start // tk
    acc = pltpu.VMEM((2 * DA_HEADS, VT_ROWS, tq), F32)
    if fixed_ref:
        body, scratch, name = _attn_fixed_kernel, [acc], "diff_attn_fixed"
    else:
        body, scratch, name = _attn_kernel, [pltpu.VMEM((2 * DA_HEADS, tq), F32), acc], "diff_attn_online"
    return pl.pallas_call(
        functools.partial(body, lam_init=lam_init),
        grid=(n_q // tq, n_k // tk),
        in_specs=[
            pl.BlockSpec((qt.shape[0], tq), lambda i, j: (0, i + qo)),
            pl.BlockSpec((tk, DA_W), lambda i, j: (j + ko, 0)),
            pl.BlockSpec((vt.shape[0], tk), lambda i, j: (0, j + ko)),
            pl.BlockSpec((2 * DA_HEADS, tq), lambda i, j: (0, i + qo)),
            pl.BlockSpec((4, DA_DH), lambda i, j: (0, 0)),
            pl.BlockSpec((DA_DV, 1), lambda i, j: (0, 0)),
        ],
        out_specs=pl.BlockSpec((tq, DA_W), lambda i, j: (i, 0)),
        out_shape=jax.ShapeDtypeStruct((n_q, DA_W), BF16),
        scratch_shapes=scratch,
        compiler_params=_cparams(("parallel", "arbitrary")),
        name=name,
    )(qt, k, vt, mref, lam_qk, subln_col)


def _attention_all(qt, k, vt, qn2, kn2_tiles, lam_qk, subln_col, lam_init, n_lat, n_ctx, tk):
    t = n_lat + n_ctx
    n_maps = 2 * DA_HEADS
    kmax2 = jnp.max(kn2_tiles, axis=0).reshape(n_maps, DA_DH)[:, 0]
    mref = jnp.sqrt(qn2) * jnp.sqrt(kmax2)[:, None] * SCORE_BOUND_SLACK
    bounded = jnp.max(mref) <= SCORE_BOUND_MAX

    def run(fixed_ref):
        def f(args):
            lat = _attention(*args, lam_init, fixed_ref,
                             q_start=0, n_q=n_lat, k_start=0, n_k=t, tq=ATT_TQ, tk=tk)
            ctx = _attention(*args, lam_init, fixed_ref,
                             q_start=n_lat, n_q=n_ctx, k_start=n_lat, n_k=n_ctx, tq=ROW_TILE, tk=ROW_TILE)
            return jnp.concatenate([lat, ctx], axis=0)
        return f

    return lax.cond(bounded, run(True), run(False), (qt, k, vt, mref, lam_qk, subln_col))


def _hg_chunk(q, z, v, lb, st_ref, rev):
    c, w = q.shape
    row = lax.broadcasted_iota(jnp.int32, (c, 1), 0)

    u = jnp.exp(-jnp.abs(z))
    sig_abs = 1.0 / (1.0 + u)
    y = jnp.log(1.0 - lb) + jnp.minimum(z, 0.0) - jnp.log(1.0 + u)
    log_lb = jnp.log(lb)
    log_f = jnp.maximum(log_lb, y) + jnp.log(1.0 + jnp.exp(-jnp.abs(log_lb - y)))
    key = (1.0 - lb) * jnp.where(z >= 0.0, u * sig_abs, sig_abs)

    ri = lax.broadcasted_iota(jnp.int32, (c, c), 0)
    ci = lax.broadcasted_iota(jnp.int32, (c, c), 1)
    tri = jnp.where((ci >= ri) if rev else (ci <= ri), 1.0, 0.0).astype(BF16)
    g1, g2, g3 = _split3(log_f)
    b = _dot(tri, g1) + _dot(tri, g2) + _dot(tri, g3)
    end = 0 if rev else c - 1
    b_end = b[end:end + 1, :]

    q_in = (q * jnp.exp(b)).astype(BF16)
    k_out = (key * jnp.exp(b_end - b)).astype(BF16)
    decay = jnp.exp(b_end)

    up = pltpu.roll(log_f, 1, 0)
    dn = pltpu.roll(log_f, c - 1, 0)
    q_lv, k_lv, masks = [], [], []
    lvl = c // 2
    while lvl >= 1:
        half = (row // lvl) & 1
        q_rows = (half == 0) if rev else (half == 1)
        if lvl >= 4:
            pos = lvl if rev else lvl - 1
            ref = jnp.broadcast_to(b.reshape(c // (2 * lvl), 2 * lvl, w)[:, pos:pos + 1, :],
                                   (c // (2 * lvl), 2 * lvl, w)).reshape(c, w)
            eq, ek = b - ref, ref - b
        elif lvl == 2:
            j = row & 3
            if rev:
                eq = jnp.where(j == 1, log_f, log_f + dn)
                ek = jnp.where(j == 2, 0.0, up)
            else:
                eq = jnp.where(j == 2, log_f, log_f + up)
                ek = jnp.where(j == 1, 0.0, dn)
        else:
            eq, ek = log_f, jnp.zeros_like(log_f)
        q_lv.append((q * jnp.exp(jnp.where(q_rows, eq, NEG_BIG))).astype(BF16))
        k_lv.append((key * jnp.exp(jnp.where(q_rows, NEG_BIG, ek))).astype(BF16))
        masks.append(None if 2 * lvl == c else
                     jnp.where((ri // (2 * lvl)) == (ci // (2 * lvl)), 1.0, 0.0))
        lvl //= 2
    qk = q * key
    on_diag = ri == ci

    outs = []
    for h in range(HG_HEADS):
        sl = slice(h * HG_DK, (h + 1) * HG_DK)
        a = jnp.where(on_diag, jnp.sum(qk[:, sl], axis=-1, keepdims=True), 0.0)
        for q_l, k_l, msk in zip(q_lv, k_lv, masks):
            a_l = _dot_nt(q_l[:, sl], k_l[:, sl])
            a = a + (a_l if msk is None else a_l * msk)
        st = st_ref[h]
        v_h = v[:, sl].astype(BF16)
        outs.append(_dot_nt(q_in[:, sl], st.astype(BF16)) + _dot(a.astype(BF16), v_h))
        st_ref[h] = st * decay[:, sl] + _dot_tn(v_h, k_out[:, sl])
    return jnp.concatenate(outs, axis=1)


def _hgrn_kernel(qf_ref, zf_ref, vf_ref, qb_ref, zb_ref, vb_ref, lbf_ref, lbb_ref,
                 of_ref, ob_ref, sf_scr, sb_scr):
    @pl.when(pl.program_id(0) == 0)
    def _():
        sf_scr[...] = jnp.zeros(sf_scr.shape, F32)
        sb_scr[...] = jnp.zeros(sb_scr.shape, F32)

    n_chunks = qf_ref.shape[0] // HG_CHUNK

    def body(ci, carry):
        rf = pl.ds(pl.multiple_of(ci * HG_CHUNK, HG_CHUNK), HG_CHUNK)
        of_ref[rf, :] = _hg_chunk(qf_ref[rf, :], zf_ref[rf, :], vf_ref[rf, :], lbf_ref[...], sf_scr, False)
        rb = pl.ds(pl.multiple_of((n_chunks - 1 - ci) * HG_CHUNK, HG_CHUNK), HG_CHUNK)
        ob_ref[rb, :] = _hg_chunk(qb_ref[rb, :], zb_ref[rb, :], vb_ref[rb, :], lbb_ref[...], sb_scr, True)
        return carry

    lax.fori_loop(0, n_chunks, body, 0)


def _hgrn(p, lb_f, lb_b, n_lat):
    t = p.shape[0]
    tm = ROW_TILE
    nb = t // tm
    n_lat_b = n_lat // tm
    n_ctx_b = nb - n_lat_b

    def jf(i):
        return jnp.where(i < n_ctx_b, n_lat_b + i, i - n_ctx_b)

    def jb(i):
        return jnp.where(i < n_ctx_b, nb - 1 - i, nb - 1 - i)

    def spec(order, colblk):
        return pl.BlockSpec((tm, HG_W), lambda i: (order(i), colblk))

    return pl.pallas_call(
        _hgrn_kernel,
        grid=(nb,),
        in_specs=[spec(jf, 3), spec(jf, 4), spec(jf, 6), spec(jb, 3), spec(jb, 5), spec(jb, 6),
                  pl.BlockSpec((1, HG_W), lambda i: (0, 0)), pl.BlockSpec((1, HG_W), lambda i: (0, 0))],
        out_specs=[pl.BlockSpec((tm, HG_W), lambda i: (jf(i), 0)),
                   pl.BlockSpec((tm, HG_W), lambda i: (jb(i), 0))],
        out_shape=[jax.ShapeDtypeStruct((t, HG_W), F32), jax.ShapeDtypeStruct((t, HG_W), F32)],
        scratch_shapes=[pltpu.VMEM((HG_HEADS, HG_DK, HG_DK), F32), pltpu.VMEM((HG_HEADS, HG_DK, HG_DK), F32)],
        compiler_params=_cparams(("arbitrary",)),
        name="hgrn2",
    )(p, p, p, p, p, p, lb_f, lb_b)


def _post_kernel(x_ref, oa_ref, of_ref, ob_ref, gh_ref, ga_ref, gb_ref, mod_ref, hg_ref, n2_ref,
                 wa_ref, wb_ref, wo_ref, rw_ref, rb_ref,
                 x1_ref, h2_ref, gw_ref, gi_ref, rk_ref, cnt_ref, run_scr, *, n_lat, tm):
    i = pl.program_id(0)
    d = x_ref.shape[1]

    @pl.when(i == 0)
    def _():
        run_scr[...] = jnp.zeros(run_scr.shape, F32)

    lat = i * tm < n_lat

    def mod(c):
        return jnp.where(lat, mod_ref[0:1, c * d:(c + 1) * d], mod_ref[1:2, c * d:(c + 1) * d])

    o = of_ref[...] + ob_ref[...]
    parts = []
    for h in range(HG_HEADS):
        oh = o[:, h * HG_DK:(h + 1) * HG_DK]
        parts.append(oh * lax.rsqrt(jnp.mean(oh * oh, axis=-1, keepdims=True) + EPS))
    gh = gh_ref[...]
    ob = jnp.concatenate(parts, axis=1) * hg_ref[...] * (gh * _sigmoid(gh))
    ya = _dot(oa_ref[...], wa_ref[...])
    yb = _dot(ob.astype(BF16), wb_ref[...])
    mix = _sigmoid(ga_ref[...]) * ya + _sigmoid(gb_ref[...]) * yb
    x1 = x_ref[...] + mod(2) * _dot(mix.astype(BF16), wo_ref[...])
    x1_ref[...] = x1

    h2 = x1 * lax.rsqrt(jnp.mean(x1 * x1, axis=-1, keepdims=True) + EPS) * n2_ref[...]
    h2 = h2 * (1.0 + mod(4)) + mod(3)
    _store_row_tiles(h2_ref, h2)

    h_hi = h2.astype(BF16)
    h_lo = (h2 - h_hi.astype(F32)).astype(BF16)
    rw = rw_ref[...]
    r_hi = rw.astype(BF16)
    r_lo = (rw - r_hi.astype(F32)).astype(BF16)
    logits = _dot(h_hi, r_hi) + (_dot(h_hi, r_lo) + _dot(h_lo, r_hi)) + rb_ref[...]

    lane = lax.broadcasted_iota(jnp.int32, logits.shape, 1)
    work = jnp.where(lane < N_EXPERTS, logits, -jnp.inf)
    sel_any = jnp.zeros(logits.shape, F32)
    vals, idxs = [], []
    for _ in range(TOP_K):
        mx = jnp.max(work, axis=-1, keepdims=True)
        idx = jnp.min(jnp.where(work == mx, lane, 2 * N_EXPERTS), axis=-1, keepdims=True)
        hit = lane == idx
        vals.append(mx)
        idxs.append(idx)
        sel_any = jnp.where(hit, 1.0, sel_any)
        work = jnp.where(hit, -jnp.inf, work)
    es = [jnp.exp(vk - vals[0]) for vk in vals]
    denom = es[0] + es[1] + es[2] + es[3]

    ri = lax.broadcasted_iota(jnp.int32, (tm, tm), 0)
    ci = lax.broadcasted_iota(jnp.int32, (tm, tm), 1)
    below = jnp.where(ci < ri, 1.0, 0.0).astype(BF16)
    rank_e = run_scr[0:1, :] + _dot(below, sel_any.astype(BF16))
    gw = jnp.zeros(logits.shape, F32)
    gi = jnp.zeros(logits.shape, jnp.int32)
    rk = jnp.zeros(logits.shape, jnp.int32)
    for k in range(TOP_K):
        at_k = lane == k
        gw = jnp.where(at_k, es[k] / denom, gw)
        gi = jnp.where(at_k, idxs[k], gi)
        rnk = jnp.sum(jnp.where(lane == idxs[k], rank_e, 0.0), axis=-1, keepdims=True)
        rk = jnp.where(at_k, rnk.astype(jnp.int32), rk)
    gw_ref[...] = gw
    gi_ref[...] = gi
    rk_ref[...] = rk
    run = run_scr[0:1, :] + jnp.sum(sel_any, axis=0, keepdims=True)
    run_scr[...] = jnp.broadcast_to(run, run_scr.shape)
    cnt_ref[...] = jnp.broadcast_to(run, cnt_ref.shape)


def _post(x, oa, o_f, o_b, p, mod, hg_g, n2_g, wa, wb, wo, rw_pad, rb_pad, n_lat):
    t, d = x.shape
    tm = ROW_TILE

    def rows(width, colblk=0):
        return pl.BlockSpec((tm, width), lambda i: (i, colblk))

    def whole(a):
        return pl.BlockSpec(a.shape, lambda i: (0,) * a.ndim)

    return pl.pallas_call(
        functools.partial(_post_kernel, n_lat=n_lat, tm=tm),
        grid=(t // tm,),
        in_specs=[rows(d), rows(DA_W), rows(HG_W), rows(HG_W),
                  rows(HG_W, 7), rows(d, 4), rows(d, 5),
                  whole(mod), whole(hg_g), whole(n2_g), whole(wa), whole(wb), whole(wo),
                  whole(rw_pad), whole(rb_pad)],
        out_specs=[rows(d), pl.BlockSpec((tm, d // 128, 128), lambda i: (i, 0, 0)), rows(128), rows(128), rows(128),
                   pl.BlockSpec((8, 128), lambda i: (0, 0))],
        out_shape=[jax.ShapeDtypeStruct((t, d), F32), jax.ShapeDtypeStruct((t, d // 128, 128), F32),
                   jax.ShapeDtypeStruct((t, 128), F32), jax.ShapeDtypeStruct((t, 128), jnp.int32),
                   jax.ShapeDtypeStruct((t, 128), jnp.int32), jax.ShapeDtypeStruct((8, 128), F32)],
        scratch_shapes=[pltpu.VMEM((8, 128), F32)],
        compiler_params=_cparams(("arbitrary",)),
        name="merge_router",
    )(x, oa, o_f, o_b, p, p, p, mod, hg_g, n2_g, wa, wb, wo, rw_pad, rb_pad)


def _dispatch_kernel(dest_ref, h_ref, xs_in, xs_out, sem):
    del xs_in
    n = dest_ref.shape[0]
    tm = h_ref.shape[0]

    def start(j, c):
        for q in range(DMA_QUEUES):
            r = DMA_QUEUES * j + q
            pltpu.make_async_copy(h_ref.at[r & (tm - 1)], xs_out.at[dest_ref[r]], sem).start(priority=q)
        return c

    lax.fori_loop(0, n // DMA_QUEUES, start, 0, unroll=4)
    pltpu.make_async_copy(xs_out.at[pl.ds(0, n)], xs_out.at[pl.ds(0, n)], sem).wait()


def _dispatch(h2, dest_tiles, xs_zero):
    t = h2.shape[0]
    tm = ROW_TILE
    return pl.pallas_call(
        _dispatch_kernel,
        grid=(t // tm,),
        in_specs=[pl.BlockSpec((tm * TOP_K,), lambda i: (i,), memory_space=pltpu.SMEM),
                  pl.BlockSpec((tm,) + h2.shape[1:], lambda i: (i, 0, 0)),
                  pl.BlockSpec(memory_space=pl.ANY)],
        out_specs=pl.BlockSpec(memory_space=pl.ANY),
        out_shape=jax.ShapeDtypeStruct(xs_zero.shape, xs_zero.dtype),
        scratch_shapes=[pltpu.SemaphoreType.DMA(())],
        input_output_aliases={2: 0},
        compiler_params=_cparams(("arbitrary",)),
        name="moe_dispatch",
    )(dest_tiles, h2, xs_zero)


def _expert_kernel(be_ref, nx_ref, sl_ref, nu_ref, xs_ref, wgu_hbm, bgu_ref, wd_hbm, bd_ref, ys_ref,
                   wgu_f32, wd_f32, wgu_bf, wd_bf, sems, *, layer):
    i = pl.program_id(0)
    used = i < nu_ref[0]

    def weight_copies(e, slot):
        return (pltpu.make_async_copy(wgu_hbm.at[layer, e], wgu_f32.at[slot], sems.at[0, slot]),
                pltpu.make_async_copy(wd_hbm.at[layer, e], wd_f32.at[slot], sems.at[1, slot]))

    @pl.when(i == 0)
    def _():
        for cp in weight_copies(be_ref[0], 0):
            cp.start()

    @pl.when(jnp.logical_and(used, jnp.logical_or(i == 0, be_ref[i] != be_ref[jnp.maximum(i - 1, 0)])))
    def _():
        slot = sl_ref[i]
        for cp in weight_copies(be_ref[i], slot):
            cp.wait()

        @pl.when(nx_ref[i] >= 0)
        def _():
            for cp in weight_copies(nx_ref[i], 1 - slot):
                cp.start()

        wgu_bf[...] = wgu_f32[slot].astype(BF16)
        wd_bf[...] = wd_f32[slot].astype(BF16)

    @pl.when(used)
    def _():
        x = _load_row_tiles(xs_ref, 0, xs_ref.shape[0]).astype(BF16)
        gu = _dot(x, wgu_bf[...]) + bgu_ref[0, 0]
        gate = jnp.minimum(gu[:, :D_FF], SWIGLU_LIMIT)
        up = jnp.clip(gu[:, D_FF:], -SWIGLU_LIMIT, SWIGLU_LIMIT)
        hdn = (up + 1.0) * gate * _sigmoid(SWIGLU_ALPHA * gate)
        _store_row_tiles(ys_ref, _dot(hdn.astype(BF16), wd_bf[...]) + bd_ref[0, 0])

    @pl.when(jnp.logical_not(used))
    def _():
        ys_ref[...] = jnp.zeros(ys_ref.shape, F32)


def _experts(xs, plan, wgu, bgu, wd, bd, layer):
    blk_e, next_e, slot, n_used = plan
    n_rows = xs.shape[0]
    d = xs.shape[1] * xs.shape[2]
    nblk = n_rows // MOE_BLOCK

    def blk(i, nu):
        return jnp.minimum(i, nu[0] - 1)

    def bias(i, be, nx, sl, nu):
        return (layer, be[blk(i, nu)], 0, 0)

    grid_spec = pltpu.PrefetchScalarGridSpec(
        num_scalar_prefetch=4,
        grid=(nblk,),
        in_specs=[
            pl.BlockSpec((MOE_BLOCK,) + xs.shape[1:], lambda i, be, nx, sl, nu: (blk(i, nu), 0, 0)),
            pl.BlockSpec(memory_space=pl.ANY),
            pl.BlockSpec((1, 1, 1, 2 * D_FF), bias),
            pl.BlockSpec(memory_space=pl.ANY),
            pl.BlockSpec((1, 1, 1, d), bias),
        ],
        out_specs=pl.BlockSpec((MOE_BLOCK,) + xs.shape[1:], lambda i, be, nx, sl, nu: (i, 0, 0)),
        scratch_shapes=[pltpu.VMEM((2, d, 2 * D_FF), F32), pltpu.VMEM((2, D_FF, d), F32),
                        pltpu.VMEM((d, 2 * D_FF), BF16), pltpu.VMEM((D_FF, d), BF16),
                        pltpu.SemaphoreType.DMA((2, 2))],
    )
    return pl.pallas_call(
        functools.partial(_expert_kernel, layer=layer),
        grid_spec=grid_spec,
        out_shape=jax.ShapeDtypeStruct(xs.shape, F32),
        compiler_params=_cparams(("arbitrary",)),
        name="moe_experts",
    )(blk_e, next_e, slot, n_used, xs, wgu, bgu, wd, bd)


def _combine_kernel(dest_ref, x1_ref, gw_ref, mod_ref, ys_ref, o_ref, buf, sem, *, n_lat, tm):
    i = pl.program_id(0)
    d = x1_ref.shape[1]
    n = dest_ref.shape[0]

    def start(j, c):
        for q in range(DMA_QUEUES):
            r = DMA_QUEUES * j + q
            pltpu.make_async_copy(ys_ref.at[dest_ref[r]], buf.at[r], sem).start(priority=q)
        return c

    lax.fori_loop(0, n // DMA_QUEUES, start, 0, unroll=4)
    pltpu.make_async_copy(ys_ref.at[pl.ds(0, n)], buf, sem).wait()

    lat = i * tm < n_lat
    g2 = jnp.where(lat, mod_ref[0:1, 5 * d:6 * d], mod_ref[1:2, 5 * d:6 * d])
    gw = gw_ref[...]
    m = gw[:, 0:1] * _load_row_tiles(buf, 0, tm)
    for k in range(1, TOP_K):
        m = m + gw[:, k:k + 1] * _load_row_tiles(buf, k * tm, tm)
    o_ref[...] = x1_ref[...] + g2 * m


def _combine(x1, gw, mod, ys, dest_tiles, n_lat):
    t, d = x1.shape
    tm = ROW_TILE
    return pl.pallas_call(
        functools.partial(_combine_kernel, n_lat=n_lat, tm=tm),
        grid=(t // tm,),
        in_specs=[pl.BlockSpec((tm * TOP_K,), lambda i: (i,), memory_space=pltpu.SMEM),
                  pl.BlockSpec((tm, d), lambda i: (i, 0)),
                  pl.BlockSpec((tm, 128), lambda i: (i, 0)),
                  pl.BlockSpec(mod.shape, lambda i: (0, 0)),
                  pl.BlockSpec(memory_space=pl.ANY)],
        out_specs=pl.BlockSpec((tm, d), lambda i: (i, 0)),
        out_shape=jax.ShapeDtypeStruct((t, d), F32),
        scratch_shapes=[pltpu.VMEM((tm * TOP_K,) + ys.shape[1:], F32), pltpu.SemaphoreType.DMA(())],
        compiler_params=_cparams(("arbitrary",)),
        name="moe_combine",
    )(dest_tiles, x1, gw, mod, ys)


def _moe_plan(gi, rk, counts):
    t = gi.shape[0]
    top_i = gi[:, :TOP_K]
    rank = rk[:, :TOP_K]
    cnt = counts[0, :N_EXPERTS].astype(jnp.int32)
    padded = (cnt + MOE_BLOCK - 1) // MOE_BLOCK * MOE_BLOCK
    pad_ends = jnp.cumsum(padded)
    pad_starts = pad_ends - padded
    dest = pad_starts[top_i] + rank
    n_rows = t * TOP_K + N_EXPERTS * MOE_BLOCK
    blk_start = jnp.arange(n_rows // MOE_BLOCK, dtype=jnp.int32) * MOE_BLOCK
    blk_e = jnp.minimum(jnp.sum(blk_start[:, None] >= pad_ends[None, :], axis=1), N_EXPERTS - 1).astype(jnp.int32)
    n_used = (pad_ends[-1:] // MOE_BLOCK).astype(jnp.int32)
    has_rows = cnt > 0
    later = jnp.where(has_rows[None, :] & (jnp.arange(N_EXPERTS)[None, :] > jnp.arange(N_EXPERTS)[:, None]),
                      jnp.arange(N_EXPERTS)[None, :], N_EXPERTS)
    next_of = jnp.min(later, axis=1)
    next_of = jnp.where(next_of < N_EXPERTS, next_of, -1).astype(jnp.int32)
    run_of = (jnp.cumsum(has_rows.astype(jnp.int32)) - 1).astype(jnp.int32)
    next_e = next_of[blk_e]
    slot = run_of[blk_e] & 1
    dest_tiles = dest.reshape(t // ROW_TILE, ROW_TILE, TOP_K).transpose(0, 2, 1).reshape(-1).astype(jnp.int32)
    return dest_tiles, (blk_e, next_e, slot, n_used), n_rows


def _rope_tables(n_lat, n_ctx):
    pos = jnp.arange(n_lat)
    row = (pos // GRID_W).astype(F32)
    col = (pos % GRID_W).astype(F32)
    freq = ROPE_BASE ** (-jnp.arange(ROPE_PAIRS, dtype=F32) / ROPE_PAIRS)
    ra = row[:, None] * freq
    ca = col[:, None] * freq
    cos64 = jnp.concatenate([jnp.cos(ra), jnp.cos(ra), jnp.cos(ca), jnp.cos(ca)], axis=1)
    sin64 = jnp.concatenate([-jnp.sin(ra), jnp.sin(ra), -jnp.sin(ca), jnp.sin(ca)], axis=1)
    cos_t = jnp.concatenate([jnp.tile(cos64, (1, DA_W // DA_DH)), jnp.ones((n_ctx, DA_W), F32)], axis=0)
    sin_t = jnp.concatenate([jnp.tile(sin64, (1, DA_W // DA_DH)), jnp.zeros((n_ctx, DA_W), F32)], axis=0)
    return cos_t, sin_t


def kernel(x, c, ctx, c_ctx, ada_w, ada_b, norm1_g, norm2_g, w_in, qn_g, kn_g, lam_qk, subln_g, hg_lb,
           hg_norm_g, w_branch_a, w_branch_b, w_out, router_w, router_b, w_gu, b_gu, w_down, b_down):
    bsz, n_lat, d = x.shape
    n_ctx = ctx.shape[1]
    depth = ada_w.shape[0]
    assert bsz == 1 and d == D_MODEL and n_lat % 512 == 0 and n_ctx % ROW_TILE == 0
    t = n_lat + n_ctx

    xx = jnp.concatenate([x[0], ctx[0]], axis=0)
    cc = jnp.zeros((8, d), F32).at[0].set(c[0]).at[1].set(c_ctx)
    mods = _modulation(cc, ada_w, ada_b)

    cos_t, sin_t = _rope_tables(n_lat, n_ctx)
    lane = jnp.arange(DA_W)
    seg64 = (lane[:, None] // DA_DH == lane[None, :] // DA_DH).astype(BF16)
    cs = jnp.cumsum(jax.nn.softmax(hg_lb.astype(F32), axis=1), axis=1)
    lb_all = cs - cs[:, :1]
    rw_pad = jnp.zeros((depth, d, 128), F32).at[:, :, :N_EXPERTS].set(router_w)
    rb_pad = jnp.zeros((depth, 1, 128), F32).at[:, 0, :N_EXPERTS].set(router_b)

    tk = 1280 if t % 1280 == 0 else ROW_TILE
    xs = None
    for l in range(depth):
        lam_init = 0.8 - 0.6 * math.exp(-0.3 * l)
        mod = mods[l]
        p = _inproj(xx, norm1_g[l][None], mod, w_in[l].astype(BF16), n_lat)
        k_r, q_t, v_t, qn2, kn2_tiles = _prep(p, cos_t, sin_t, seg64, jnp.tile(qn_g[l], DA_W // DA_DH)[None],
                                              jnp.tile(kn_g[l], DA_W // DA_DH)[None])
        oa = _attention_all(q_t, k_r, v_t, qn2, kn2_tiles, lam_qk[l], subln_g[l][:, None], lam_init,
                            n_lat, n_ctx, tk)
        o_f, o_b = _hgrn(p, lb_all[0, l][None], lb_all[1, l][None], n_lat)
        x1, h2, gw, gi, rk, counts = _post(
            xx, oa, o_f, o_b, p, mod, jnp.tile(hg_norm_g[l], HG_HEADS)[None], norm2_g[l][None],
            w_branch_a[l].astype(BF16), w_branch_b[l].astype(BF16), w_out[l].astype(BF16),
            rw_pad[l], rb_pad[l], n_lat)
        dest_tiles, plan, n_rows = _moe_plan(gi, rk, counts)
        xs = _dispatch(h2, dest_tiles, jnp.zeros((n_rows,) + h2.shape[1:], F32) if xs is None else xs)
        ys = _experts(xs, plan, w_gu, b_gu[:, :, None, :], w_down, b_down[:, :, None, :], l)
        xx = _combine(x1, gw, mod, ys, dest_tiles, n_lat)
    return xx[:n_lat][None]
```

```python
import functools
import math

import jax
import jax.numpy as jnp
from jax import lax
from jax.experimental import pallas as pl
from jax.experimental.pallas import tpu as pltpu

F32 = jnp.float32
BF16 = jnp.bfloat16
QK_DTYPE = jnp.float8_e4m3fn

D_MODEL = 1024
GRID_W = 64
EPS = 1e-6

DA_HEADS = 4
DA_DH = 64
DA_DV = 2 * DA_DH
DA_W = DA_HEADS * DA_DV
DA_SCALE = DA_DH ** -0.5
ROPE_PAIRS = DA_DH // 4
ROPE_BASE = 10000.0

HG_HEADS = 4
HG_DK = 128
HG_W = HG_HEADS * HG_DK
HG_CHUNK = 64

N_EXPERTS = 32
TOP_K = 4
D_FF = 1024
SWIGLU_ALPHA = 1.702
SWIGLU_LIMIT = 7.0
MOE_BLOCK = 256

IN_COLS = 6144
ROW_TILE = 256
VT_ROWS = DA_DV + 16
NEG_BIG = -1e30
LOG2E = 1.4426950408889634

VMEM_LIMIT = 56 * 1024 * 1024
ATT_TQ = 256
ATT_KEY_CHUNKS = (3328, 1280, 256)
DMA_QUEUES = 2
SCORE_BOUND_MAX = 40.0
SCORE_BOUND_SLACK = 1.01


def _cparams(sem):
    return pltpu.CompilerParams(dimension_semantics=sem, vmem_limit_bytes=VMEM_LIMIT)


def _split3(x):
    a = x.astype(BF16)
    r = x - a.astype(F32)
    b = r.astype(BF16)
    c = (r - b.astype(F32)).astype(BF16)
    return a, b, c


def _dot(a, b):
    return jnp.dot(a, b, preferred_element_type=F32)


def _dot_nt(a, b):
    return lax.dot_general(a, b, (((1,), (1,)), ((), ())), preferred_element_type=F32)


def _dot_tn(a, b):
    return lax.dot_general(a, b, (((0,), (0,)), ((), ())), preferred_element_type=F32)


def _sigmoid(x):
    return 1.0 / (1.0 + jnp.exp(-x))


def _store_row_tiles(ref, val):
    for s in range(ref.shape[1]):
        ref[:, s, :] = val[:, s * 128:(s + 1) * 128]


def _load_row_tiles(ref, lo, n):
    return jnp.concatenate([ref[lo:lo + n, s, :] for s in range(ref.shape[1])], axis=1)


def _mod_kernel(c_ref, w_ref, b_ref, o_ref):
    cv = c_ref[...]
    a = cv * _sigmoid(cv)
    w = w_ref[0]
    a1, a2, a3 = _split3(a)
    w1, w2, w3 = _split3(w)
    acc = _dot(a1, w1) + (_dot(a1, w2) + _dot(a2, w1)) + (_dot(a2, w2) + _dot(a1, w3) + _dot(a3, w1))
    o_ref[0] = acc + b_ref[0]


def _modulation(cc, ada_w, ada_b):
    depth, d, n = ada_w.shape
    tn = 1536
    return pl.pallas_call(
        _mod_kernel,
        grid=(depth, n // tn),
        in_specs=[
            pl.BlockSpec((8, d), lambda l, j: (0, 0)),
            pl.BlockSpec((1, d, tn), lambda l, j: (l, 0, j)),
            pl.BlockSpec((1, 1, tn), lambda l, j: (l, 0, j)),
        ],
        out_specs=pl.BlockSpec((1, 8, tn), lambda l, j: (l, 0, j)),
        out_shape=jax.ShapeDtypeStruct((depth, 8, n), F32),
        compiler_params=_cparams(("parallel", "parallel")),
        name="adaln_mod",
    )(cc, ada_w, ada_b.reshape(depth, 1, n))


def _inproj_kernel(x_ref, g_ref, mod_ref, w_ref, o_ref, h_scr, *, n_lat, tm):
    i = pl.program_id(0)
    j = pl.program_id(1)
    d = x_ref.shape[1]

    @pl.when(j == 0)
    def _():
        x = x_ref[...]
        y = x * lax.rsqrt(jnp.mean(x * x, axis=-1, keepdims=True) + EPS) * g_ref[...]
        row = i * tm + lax.broadcasted_iota(jnp.int32, (tm, 1), 0)
        lat = row < n_lat
        sh = jnp.where(lat, mod_ref[0:1, 0:d], mod_ref[1:2, 0:d])
        sc = jnp.where(lat, mod_ref[0:1, d:2 * d], mod_ref[1:2, d:2 * d])
        h_scr[...] = (y * (1.0 + sc) + sh).astype(BF16)

    o_ref[...] = _dot(h_scr[...], w_ref[...])


def _inproj(x, g, mod, w_bf, n_lat):
    t, d = x.shape
    n = w_bf.shape[1]
    tm = 1280 if t % 1280 == 0 else ROW_TILE
    tn = 1536
    return pl.pallas_call(
        functools.partial(_inproj_kernel, n_lat=n_lat, tm=tm),
        grid=(t // tm, n // tn),
        in_specs=[
            pl.BlockSpec((tm, d), lambda i, j: (i, 0)),
            pl.BlockSpec((1, d), lambda i, j: (0, 0)),
            pl.BlockSpec((8, mod.shape[1]), lambda i, j: (0, 0)),
            pl.BlockSpec((d, tn), lambda i, j: (0, j)),
        ],
        out_specs=pl.BlockSpec((tm, tn), lambda i, j: (i, j)),
        out_shape=jax.ShapeDtypeStruct((t, n), F32),
        scratch_shapes=[pltpu.VMEM((tm, d), BF16)],
        compiler_params=_cparams(("parallel", "arbitrary")),
        name="inproj",
    )(x, g, mod, w_bf)


def _segment_mean_sq(x, seg_ref, width):
    x2 = x * x
    hi = x2.astype(BF16)
    lo = (x2 - hi.astype(F32)).astype(BF16)
    seg = seg_ref[...]
    return (_dot(hi, seg) + _dot(lo, seg)) * (1.0 / width)


def _prep_kernel(q_ref, k_ref, v_ref, cos_ref, sin_ref, seg_ref, qg_ref, kg_ref,
                 k_out, qt_out, vt_out, qn_out, kn_out):
    tm = q_ref.shape[0]
    cos = cos_ref[...]
    sin = sin_ref[...]
    lane = lax.broadcasted_iota(jnp.int32, (tm, DA_W), 1)
    first_half = (lane & 31) < 16

    def norm_rope(x, g):
        y = x * lax.rsqrt(_segment_mean_sq(x, seg_ref, DA_DH) + EPS) * g
        fwd = pltpu.roll(y, DA_W - 16, 1)
        bwd = pltpu.roll(y, 16, 1)
        partner = jnp.where(first_half, fwd, bwd)
        return y * cos + partner * sin

    k_b = norm_rope(k_ref[...], kg_ref[...]).astype(QK_DTYPE)
    k_out[...] = k_b
    k_f = k_b.astype(F32)
    k_n2 = _dot((k_f * k_f).astype(BF16), seg_ref[...])
    kn_out[...] = jnp.broadcast_to(jnp.max(k_n2, axis=0, keepdims=True), kn_out.shape)

    q = norm_rope(q_ref[...], qg_ref[...]) * (DA_SCALE * LOG2E)
    lane_h = lax.broadcasted_iota(jnp.int32, (tm, DA_DV), 1)
    v = v_ref[...]
    ones = jnp.ones((VT_ROWS - DA_DV, tm), BF16)
    for h in range(DA_HEADS):
        qh = q[:, h * DA_DV:(h + 1) * DA_DV]
        for m in range(2):
            keep = (lane_h < DA_DH) if m == 0 else (lane_h >= DA_DH)
            r = 2 * h + m
            qt_b = jnp.where(keep, qh, 0.0).T.astype(QK_DTYPE)
            qt_out[r * DA_DV:(r + 1) * DA_DV, :] = qt_b
            qt_f = qt_b.astype(F32)
            qn_out[r:r + 1, :] = jnp.sum(qt_f * qt_f, axis=0, keepdims=True)
        vt_out[0, h * VT_ROWS:h * VT_ROWS + DA_DV, :] = v[:, h * DA_DV:(h + 1) * DA_DV].T.astype(BF16)
        vt_out[0, h * VT_ROWS + DA_DV:(h + 1) * VT_ROWS, :] = ones


def _prep(p, cos_t, sin_t, seg64, qg, kg, chunk):
    t = p.shape[0]
    tm = ROW_TILE
    per_chunk = chunk // tm
    return pl.pallas_call(
        _prep_kernel,
        grid=(t // tm,),
        in_specs=[
            pl.BlockSpec((tm, DA_W), lambda i: (i, 0)),
            pl.BlockSpec((tm, DA_W), lambda i: (i, 1)),
            pl.BlockSpec((tm, DA_W), lambda i: (i, 2)),
            pl.BlockSpec((tm, DA_W), lambda i: (i, 0)),
            pl.BlockSpec((tm, DA_W), lambda i: (i, 0)),
            pl.BlockSpec((DA_W, DA_W), lambda i: (0, 0)),
            pl.BlockSpec((1, DA_W), lambda i: (0, 0)),
            pl.BlockSpec((1, DA_W), lambda i: (0, 0)),
        ],
        out_specs=[
            pl.BlockSpec((tm, DA_W), lambda i: (i, 0)),
            pl.BlockSpec((2 * DA_HEADS * DA_DV, tm), lambda i: (0, i)),
            pl.BlockSpec((1, DA_HEADS * VT_ROWS, tm), lambda i: (i // per_chunk, 0, i % per_chunk)),
            pl.BlockSpec((2 * DA_HEADS, tm), lambda i: (0, i)),
            pl.BlockSpec((8, DA_W), lambda i: (i, 0)),
        ],
        out_shape=[
            jax.ShapeDtypeStruct((t, DA_W), QK_DTYPE),
            jax.ShapeDtypeStruct((2 * DA_HEADS * DA_DV, t), QK_DTYPE),
            jax.ShapeDtypeStruct((t // chunk, DA_HEADS * VT_ROWS, chunk), BF16),
            jax.ShapeDtypeStruct((2 * DA_HEADS, t), F32),
            jax.ShapeDtypeStruct((8 * (t // tm), DA_W), F32),
        ],
        compiler_params=_cparams(("parallel",)),
        name="qkv_prep",
    )(p, p, p, cos_t, sin_t, seg64, qg, kg)


def _attn_finalize(lam_ref, sg_ref, o_ref, acc_scr, lam_init):
    lq = lam_ref[...]
    lam = (jnp.exp(jnp.sum(lq[0:1] * lq[1:2], axis=-1, keepdims=True))
           - jnp.exp(jnp.sum(lq[2:3] * lq[3:4], axis=-1, keepdims=True)) + lam_init)
    for h in range(DA_HEADS):
        a1 = acc_scr[2 * h]
        a2 = acc_scr[2 * h + 1]
        o = a1[0:DA_DV] / a1[DA_DV:DA_DV + 1] - lam * (a2[0:DA_DV] / a2[DA_DV:DA_DV + 1])
        o = o * lax.rsqrt(jnp.mean(o * o, axis=0, keepdims=True) + EPS)
        o = o * sg_ref[...] * (1.0 - lam_init)
        o_ref[:, h * DA_DV:(h + 1) * DA_DV] = o.T.astype(o_ref.dtype)


def _attn_kernel(qt_ref, k_hbm, vt_hbm, lam_ref, sg_ref, o_ref, k_scr, vt_scr, m_scr, acc_scr, sems,
                 *, lam_init, stabilised):
    @pl.when(pl.program_id(0) == 0)
    def _():
        copies = (pltpu.make_async_copy(k_hbm, k_scr, sems.at[0]), pltpu.make_async_copy(vt_hbm, vt_scr, sems.at[1]))
        for cp in copies:
            cp.start()
        for cp in copies:
            cp.wait()

    acc_scr[...] = jnp.zeros(acc_scr.shape, F32)
    if stabilised:
        m_scr[...] = jnp.full(m_scr.shape, NEG_BIG, F32)

    def chunk(c, carry):
        for h in range(DA_HEADS):
            k_h = k_scr[c, :, h * DA_DV:(h + 1) * DA_DV]
            vt_h = vt_scr[c, h * VT_ROWS:(h + 1) * VT_ROWS, :]
            for m in range(2):
                r = 2 * h + m
                s = _dot(k_h, qt_ref[r * DA_DV:(r + 1) * DA_DV, :])
                if stabilised:
                    m_prev = m_scr[r:r + 1, :]
                    m_new = jnp.maximum(m_prev, jnp.max(s, axis=0, keepdims=True))
                    p = jnp.exp2((s - m_new).astype(BF16))
                    acc_scr[r] = jnp.exp2(m_prev - m_new) * acc_scr[r] + _dot(vt_h, p)
                    m_scr[r:r + 1, :] = m_new
                else:
                    acc_scr[r] += _dot(vt_h, jnp.exp2(s.astype(BF16)))
        return carry

    lax.fori_loop(0, k_scr.shape[0], chunk, 0)
    _attn_finalize(lam_ref, sg_ref, o_ref, acc_scr, lam_init)


def _attention(qt, k3, vt3, lam_qk, subln_col, lam_init, stabilised, *, q_start, n_q, tq):
    qo = q_start // tq
    return pl.pallas_call(
        functools.partial(_attn_kernel, lam_init=lam_init, stabilised=stabilised),
        grid=(n_q // tq,),
        in_specs=[
            pl.BlockSpec((qt.shape[0], tq), lambda i: (0, i + qo)),
            pl.BlockSpec(memory_space=pl.ANY),
            pl.BlockSpec(memory_space=pl.ANY),
            pl.BlockSpec((4, DA_DH), lambda i: (0, 0)),
            pl.BlockSpec((DA_DV, 1), lambda i: (0, 0)),
        ],
        out_specs=pl.BlockSpec((tq, DA_W), lambda i: (i, 0)),
        out_shape=jax.ShapeDtypeStruct((n_q, DA_W), BF16),
        scratch_shapes=[
            pltpu.VMEM(k3.shape, k3.dtype),
            pltpu.VMEM(vt3.shape, vt3.dtype),
            pltpu.VMEM((2 * DA_HEADS, tq), F32),
            pltpu.VMEM((2 * DA_HEADS, VT_ROWS, tq), F32),
            pltpu.SemaphoreType.DMA((2,)),
        ],
        compiler_params=_cparams(("arbitrary",)),
        name="diff_attn_online" if stabilised else "diff_attn_plain",
    )(qt, k3, vt3, lam_qk, subln_col)


def _attention_all(qt, k, vt3, qn2, kn2_tiles, lam_qk, subln_col, lam_init, n_lat, n_ctx):
    n_chunks, _, chunk = vt3.shape
    n_maps = 2 * DA_HEADS
    kmax2 = jnp.max(kn2_tiles, axis=0).reshape(n_maps, DA_DH)[:, 0]
    bound = jnp.sqrt(jnp.max(qn2, axis=1)) * jnp.sqrt(kmax2) * SCORE_BOUND_SLACK
    bounded = jnp.max(bound) <= SCORE_BOUND_MAX
    k3 = k.reshape(n_chunks, chunk, DA_W)

    def run(stabilised):
        def f(args):
            qt_, k3_, vt3_, lam_, sg_ = args
            lat = _attention(qt_, k3_, vt3_, lam_, sg_, lam_init, stabilised, q_start=0, n_q=n_lat, tq=ATT_TQ)
            ctx = _attention(qt_, k3_[n_chunks - 1:, chunk - n_ctx:, :], vt3_[n_chunks - 1:, :, chunk - n_ctx:],
                             lam_, sg_, lam_init, stabilised, q_start=n_lat, n_q=n_ctx, tq=ATT_TQ)
            return jnp.concatenate([lat, ctx], axis=0)
        return f

    return lax.cond(bounded, run(False), run(True), (qt, k3, vt3, lam_qk, subln_col))


def _hg_chunk(q, z, v, lb, st_ref, rev):
    c, w = q.shape
    row = lax.broadcasted_iota(jnp.int32, (c, 1), 0)

    u = jnp.exp(-jnp.abs(z))
    sig_abs = 1.0 / (1.0 + u)
    y = jnp.log(1.0 - lb) + jnp.minimum(z, 0.0) - jnp.log(1.0 + u)
    log_lb = jnp.log(lb)
    log_f = jnp.maximum(log_lb, y) + jnp.log(1.0 + jnp.exp(-jnp.abs(log_lb - y)))
    key = (1.0 - lb) * jnp.where(z >= 0.0, u * sig_abs, sig_abs)

    ri = lax.broadcasted_iota(jnp.int32, (c, c), 0)
    ci = lax.broadcasted_iota(jnp.int32, (c, c), 1)
    tri = jnp.where((ci >= ri) if rev else (ci <= ri), 1.0, 0.0).astype(BF16)
    g1, g2, g3 = _split3(log_f)
    b = _dot(tri, g1) + _dot(tri, g2) + _dot(tri, g3)
    end = 0 if rev else c - 1
    b_end = b[end:end + 1, :]

    q_in = (q * jnp.exp(b)).astype(BF16)
    k_out = (key * jnp.exp(b_end - b)).astype(BF16)
    decay = jnp.exp(b_end)

    up = pltpu.roll(log_f, 1, 0)
    dn = pltpu.roll(log_f, c - 1, 0)
    q_lv, k_lv, masks = [], [], []
    lvl = c // 2
    while lvl >= 1:
        half = (row // lvl) & 1
        q_rows = (half == 0) if rev else (half == 1)
        if lvl >= 4:
            pos = lvl if rev else lvl - 1
            ref = jnp.broadcast_to(b.reshape(c // (2 * lvl), 2 * lvl, w)[:, pos:pos + 1, :],
                                   (c // (2 * lvl), 2 * lvl, w)).reshape(c, w)
            eq, ek = b - ref, ref - b
        elif lvl == 2:
            j = row & 3
            if rev:
                eq = jnp.where(j == 1, log_f, log_f + dn)
                ek = jnp.where(j == 2, 0.0, up)
            else:
                eq = jnp.where(j == 2, log_f, log_f + up)
                ek = jnp.where(j == 1, 0.0, dn)
        else:
            eq, ek = log_f, jnp.zeros_like(log_f)
        q_lv.append((q * jnp.exp(jnp.where(q_rows, eq, NEG_BIG))).astype(BF16))
        k_lv.append((key * jnp.exp(jnp.where(q_rows, NEG_BIG, ek))).astype(BF16))
        masks.append(None if 2 * lvl == c else
                     jnp.where((ri // (2 * lvl)) == (ci // (2 * lvl)), 1.0, 0.0))
        lvl //= 2
    qk = q * key
    on_diag = ri == ci

    outs = []
    for h in range(HG_HEADS):
        sl = slice(h * HG_DK, (h + 1) * HG_DK)
        a = jnp.where(on_diag, jnp.sum(qk[:, sl], axis=-1, keepdims=True), 0.0)
        for q_l, k_l, msk in zip(q_lv, k_lv, masks):
            a_l = _dot_nt(q_l[:, sl], k_l[:, sl])
            a = a + (a_l if msk is None else a_l * msk)
        st = st_ref[h]
        v_h = v[:, sl].astype(BF16)
        outs.append(_dot_nt(q_in[:, sl], st.astype(BF16)) + _dot(a.astype(BF16), v_h))
        st_ref[h] = st * decay[:, sl] + _dot_tn(v_h, k_out[:, sl])
    return jnp.concatenate(outs, axis=1)


def _hgrn_kernel(qf_ref, zf_ref, vf_ref, qb_ref, zb_ref, vb_ref, lbf_ref, lbb_ref,
                 of_ref, ob_ref, sf_scr, sb_scr):
    @pl.when(pl.program_id(0) == 0)
    def _():
        sf_scr[...] = jnp.zeros(sf_scr.shape, F32)
        sb_scr[...] = jnp.zeros(sb_scr.shape, F32)

    n_chunks = qf_ref.shape[0] // HG_CHUNK

    def body(ci, carry):
        rf = pl.ds(pl.multiple_of(ci * HG_CHUNK, HG_CHUNK), HG_CHUNK)
        of_ref[rf, :] = _hg_chunk(qf_ref[rf, :], zf_ref[rf, :], vf_ref[rf, :], lbf_ref[...], sf_scr, False)
        rb = pl.ds(pl.multiple_of((n_chunks - 1 - ci) * HG_CHUNK, HG_CHUNK), HG_CHUNK)
        ob_ref[rb, :] = _hg_chunk(qb_ref[rb, :], zb_ref[rb, :], vb_ref[rb, :], lbb_ref[...], sb_scr, True)
        return carry

    lax.fori_loop(0, n_chunks, body, 0)


def _hgrn(p, lb_f, lb_b, n_lat):
    t = p.shape[0]
    tm = ROW_TILE
    nb = t // tm
    n_lat_b = n_lat // tm
    n_ctx_b = nb - n_lat_b

    def jf(i):
        return jnp.where(i < n_ctx_b, n_lat_b + i, i - n_ctx_b)

    def jb(i):
        return jnp.where(i < n_ctx_b, nb - 1 - i, nb - 1 - i)

    def spec(order, colblk):
        return pl.BlockSpec((tm, HG_W), lambda i: (order(i), colblk))

    return pl.pallas_call(
        _hgrn_kernel,
        grid=(nb,),
        in_specs=[spec(jf, 3), spec(jf, 4), spec(jf, 6), spec(jb, 3), spec(jb, 5), spec(jb, 6),
                  pl.BlockSpec((1, HG_W), lambda i: (0, 0)), pl.BlockSpec((1, HG_W), lambda i: (0, 0))],
        out_specs=[pl.BlockSpec((tm, HG_W), lambda i: (jf(i), 0)),
                   pl.BlockSpec((tm, HG_W), lambda i: (jb(i), 0))],
        out_shape=[jax.ShapeDtypeStruct((t, HG_W), F32), jax.ShapeDtypeStruct((t, HG_W), F32)],
        scratch_shapes=[pltpu.VMEM((HG_HEADS, HG_DK, HG_DK), F32), pltpu.VMEM((HG_HEADS, HG_DK, HG_DK), F32)],
        compiler_params=_cparams(("arbitrary",)),
        name="hgrn2",
    )(p, p, p, p, p, p, lb_f, lb_b)


def _post_kernel(x_ref, oa_ref, of_ref, ob_ref, gh_ref, ga_ref, gb_ref, mod_ref, hg_ref, n2_ref,
                 wa_ref, wb_ref, wo_ref, rw_ref, rb_ref,
                 x1_ref, h2_ref, gw_ref, gi_ref, rk_ref, cnt_ref, run_scr, *, n_lat, tm):
    i = pl.program_id(0)
    d = x_ref.shape[1]

    @pl.when(i == 0)
    def _():
        run_scr[...] = jnp.zeros(run_scr.shape, F32)

    lat = i * tm < n_lat

    def mod(c):
        return jnp.where(lat, mod_ref[0:1, c * d:(c + 1) * d], mod_ref[1:2, c * d:(c + 1) * d])

    o = of_ref[...] + ob_ref[...]
    parts = []
    for h in range(HG_HEADS):
        oh = o[:, h * HG_DK:(h + 1) * HG_DK]
        parts.append(oh * lax.rsqrt(jnp.mean(oh * oh, axis=-1, keepdims=True) + EPS))
    gh = gh_ref[...]
    ob = jnp.concatenate(parts, axis=1) * hg_ref[...] * (gh * _sigmoid(gh))
    ya = _dot(oa_ref[...], wa_ref[...])
    yb = _dot(ob.astype(BF16), wb_ref[...])
    mix = _sigmoid(ga_ref[...]) * ya + _sigmoid(gb_ref[...]) * yb
    x1 = x_ref[...] + mod(2) * _dot(mix.astype(BF16), wo_ref[...])
    x1_ref[...] = x1

    h2 = x1 * lax.rsqrt(jnp.mean(x1 * x1, axis=-1, keepdims=True) + EPS) * n2_ref[...]
    h2 = h2 * (1.0 + mod(4)) + mod(3)
    _store_row_tiles(h2_ref, h2)

    h_hi = h2.astype(BF16)
    h_lo = (h2 - h_hi.astype(F32)).astype(BF16)
    rw = rw_ref[...]
    r_hi = rw.astype(BF16)
    r_lo = (rw - r_hi.astype(F32)).astype(BF16)
    logits = _dot(h_hi, r_hi) + (_dot(h_hi, r_lo) + _dot(h_lo, r_hi)) + rb_ref[...]

    lane = lax.broadcasted_iota(jnp.int32, logits.shape, 1)
    work = jnp.where(lane < N_EXPERTS, logits, -jnp.inf)
    sel_any = jnp.zeros(logits.shape, F32)
    vals, idxs = [], []
    for _ in range(TOP_K):
        mx = jnp.max(work, axis=-1, keepdims=True)
        idx = jnp.min(jnp.where(work == mx, lane, 2 * N_EXPERTS), axis=-1, keepdims=True)
        hit = lane == idx
        vals.append(mx)
        idxs.append(idx)
        sel_any = jnp.where(hit, 1.0, sel_any)
        work = jnp.where(hit, -jnp.inf, work)
    es = [jnp.exp(vk - vals[0]) for vk in vals]
    denom = es[0] + es[1] + es[2] + es[3]

    ri = lax.broadcasted_iota(jnp.int32, (tm, tm), 0)
    ci = lax.broadcasted_iota(jnp.int32, (tm, tm), 1)
    below = jnp.where(ci < ri, 1.0, 0.0).astype(BF16)
    rank_e = run_scr[0:1, :] + _dot(below, sel_any.astype(BF16))
    gw = jnp.zeros(logits.shape, F32)
    gi = jnp.zeros(logits.shape, jnp.int32)
    rk = jnp.zeros(logits.shape, jnp.int32)
    for k in range(TOP_K):
        at_k = lane == k
        gw = jnp.where(at_k, es[k] / denom, gw)
        gi = jnp.where(at_k, idxs[k], gi)
        rnk = jnp.sum(jnp.where(lane == idxs[k], rank_e, 0.0), axis=-1, keepdims=True)
        rk = jnp.where(at_k, rnk.astype(jnp.int32), rk)
    gw_ref[...] = gw
    gi_ref[...] = gi
    rk_ref[...] = rk
    run = run_scr[0:1, :] + jnp.sum(sel_any, axis=0, keepdims=True)
    run_scr[...] = jnp.broadcast_to(run, run_scr.shape)
    cnt_ref[...] = jnp.broadcast_to(run, cnt_ref.shape)


def _post(x, oa, o_f, o_b, p, mod, hg_g, n2_g, wa, wb, wo, rw_pad, rb_pad, n_lat):
    t, d = x.shape
    tm = ROW_TILE

    def rows(width, colblk=0):
        return pl.BlockSpec((tm, width), lambda i: (i, colblk))

    def whole(a):
        return pl.BlockSpec(a.shape, lambda i: (0,) * a.ndim)

    return pl.pallas_call(
        functools.partial(_post_kernel, n_lat=n_lat, tm=tm),
        grid=(t // tm,),
        in_specs=[rows(d), rows(DA_W), rows(HG_W), rows(HG_W),
                  rows(HG_W, 7), rows(d, 4), rows(d, 5),
                  whole(mod), whole(hg_g), whole(n2_g), whole(wa), whole(wb), whole(wo),
                  whole(rw_pad), whole(rb_pad)],
        out_specs=[rows(d), pl.BlockSpec((tm, d // 128, 128), lambda i: (i, 0, 0)), rows(128), rows(128), rows(128),
                   pl.BlockSpec((8, 128), lambda i: (0, 0))],
        out_shape=[jax.ShapeDtypeStruct((t, d), F32), jax.ShapeDtypeStruct((t, d // 128, 128), F32),
                   jax.ShapeDtypeStruct((t, 128), F32), jax.ShapeDtypeStruct((t, 128), jnp.int32),
                   jax.ShapeDtypeStruct((t, 128), jnp.int32), jax.ShapeDtypeStruct((8, 128), F32)],
        scratch_shapes=[pltpu.VMEM((8, 128), F32)],
        compiler_params=_cparams(("arbitrary",)),
        name="merge_router",
    )(x, oa, o_f, o_b, p, p, p, mod, hg_g, n2_g, wa, wb, wo, rw_pad, rb_pad)


def _dispatch_kernel(dest_ref, h_ref, xs_in, xs_out, sem):
    del xs_in
    n = dest_ref.shape[0]
    tm = h_ref.shape[0]

    def start(j, c):
        for q in range(DMA_QUEUES):
            r = DMA_QUEUES * j + q
            pltpu.make_async_copy(h_ref.at[r & (tm - 1)], xs_out.at[dest_ref[r]], sem).start(priority=q)
        return c

    lax.fori_loop(0, n // DMA_QUEUES, start, 0, unroll=4)
    pltpu.make_async_copy(xs_out.at[pl.ds(0, n)], xs_out.at[pl.ds(0, n)], sem).wait()


def _dispatch(h2, dest_tiles, xs_zero):
    t = h2.shape[0]
    tm = ROW_TILE
    return pl.pallas_call(
        _dispatch_kernel,
        grid=(t // tm,),
        in_specs=[pl.BlockSpec((tm * TOP_K,), lambda i: (i,), memory_space=pltpu.SMEM),
                  pl.BlockSpec((tm,) + h2.shape[1:], lambda i: (i, 0, 0)),
                  pl.BlockSpec(memory_space=pl.ANY)],
        out_specs=pl.BlockSpec(memory_space=pl.ANY),
        out_shape=jax.ShapeDtypeStruct(xs_zero.shape, xs_zero.dtype),
        scratch_shapes=[pltpu.SemaphoreType.DMA(())],
        input_output_aliases={2: 0},
        compiler_params=_cparams(("arbitrary",)),
        name="moe_dispatch",
    )(dest_tiles, h2, xs_zero)


def _expert_kernel(be_ref, nx_ref, sl_ref, nu_ref, xs_ref, wgu_hbm, bgu_ref, wd_hbm, bd_ref, ys_ref,
                   wgu_f32, wd_f32, wgu_bf, wd_bf, sems, *, layer):
    i = pl.program_id(0)
    used = i < nu_ref[0]

    def weight_copies(e, slot):
        return (pltpu.make_async_copy(wgu_hbm.at[layer, e], wgu_f32.at[slot], sems.at[0, slot]),
                pltpu.make_async_copy(wd_hbm.at[layer, e], wd_f32.at[slot], sems.at[1, slot]))

    @pl.when(i == 0)
    def _():
        for cp in weight_copies(be_ref[0], 0):
            cp.start()

    @pl.when(jnp.logical_and(used, jnp.logical_or(i == 0, be_ref[i] != be_ref[jnp.maximum(i - 1, 0)])))
    def _():
        slot = sl_ref[i]
        for cp in weight_copies(be_ref[i], slot):
            cp.wait()

        @pl.when(nx_ref[i] >= 0)
        def _():
            for cp in weight_copies(nx_ref[i], 1 - slot):
                cp.start()

        wgu_bf[...] = wgu_f32[slot].astype(BF16)
        wd_bf[...] = wd_f32[slot].astype(BF16)

    @pl.when(used)
    def _():
        x = _load_row_tiles(xs_ref, 0, xs_ref.shape[0]).astype(BF16)
        gu = _dot(x, wgu_bf[...]) + bgu_ref[0, 0]
        gate = jnp.minimum(gu[:, :D_FF], SWIGLU_LIMIT)
        up = jnp.clip(gu[:, D_FF:], -SWIGLU_LIMIT, SWIGLU_LIMIT)
        hdn = (up + 1.0) * gate * _sigmoid(SWIGLU_ALPHA * gate)
        _store_row_tiles(ys_ref, _dot(hdn.astype(BF16), wd_bf[...]) + bd_ref[0, 0])

    @pl.when(jnp.logical_not(used))
    def _():
        ys_ref[...] = jnp.zeros(ys_ref.shape, F32)


def _experts(xs, plan, wgu, bgu, wd, bd, layer):
    blk_e, next_e, slot, n_used = plan
    n_rows = xs.shape[0]
    d = xs.shape[1] * xs.shape[2]
    nblk = n_rows // MOE_BLOCK

    def blk(i, nu):
        return jnp.minimum(i, nu[0] - 1)

    def bias(i, be, nx, sl, nu):
        return (layer, be[blk(i, nu)], 0, 0)

    grid_spec = pltpu.PrefetchScalarGridSpec(
        num_scalar_prefetch=4,
        grid=(nblk,),
        in_specs=[
            pl.BlockSpec((MOE_BLOCK,) + xs.shape[1:], lambda i, be, nx, sl, nu: (blk(i, nu), 0, 0)),
            pl.BlockSpec(memory_space=pl.ANY),
            pl.BlockSpec((1, 1, 1, 2 * D_FF), bias),
            pl.BlockSpec(memory_space=pl.ANY),
            pl.BlockSpec((1, 1, 1, d), bias),
        ],
        out_specs=pl.BlockSpec((MOE_BLOCK,) + xs.shape[1:], lambda i, be, nx, sl, nu: (i, 0, 0)),
        scratch_shapes=[pltpu.VMEM((2, d, 2 * D_FF), F32), pltpu.VMEM((2, D_FF, d), F32),
                        pltpu.VMEM((d, 2 * D_FF), BF16), pltpu.VMEM((D_FF, d), BF16),
                        pltpu.SemaphoreType.DMA((2, 2))],
    )
    return pl.pallas_call(
        functools.partial(_expert_kernel, layer=layer),
        grid_spec=grid_spec,
        out_shape=jax.ShapeDtypeStruct(xs.shape, F32),
        compiler_params=_cparams(("arbitrary",)),
        name="moe_experts",
    )(blk_e, next_e, slot, n_used, xs, wgu, bgu, wd, bd)


def _combine_kernel(dest_ref, x1_ref, gw_ref, mod_ref, ys_ref, o_ref, buf, sem, *, n_lat, tm):
    i = pl.program_id(0)
    d = x1_ref.shape[1]
    n = dest_ref.shape[0]

    def start(j, c):
        for q in range(DMA_QUEUES):
            r = DMA_QUEUES * j + q
            pltpu.make_async_copy(ys_ref.at[dest_ref[r]], buf.at[r], sem).start(priority=q)
        return c

    lax.fori_loop(0, n // DMA_QUEUES, start, 0, unroll=4)
    pltpu.make_async_copy(ys_ref.at[pl.ds(0, n)], buf, sem).wait()

    lat = i * tm < n_lat
    g2 = jnp.where(lat, mod_ref[0:1, 5 * d:6 * d], mod_ref[1:2, 5 * d:6 * d])
    gw = gw_ref[...]
    m = gw[:, 0:1] * _load_row_tiles(buf, 0, tm)
    for k in range(1, TOP_K):
        m = m + gw[:, k:k + 1] * _load_row_tiles(buf, k * tm, tm)
    o_ref[...] = x1_ref[...] + g2 * m


def _combine(x1, gw, mod, ys, dest_tiles, n_lat):
    t, d = x1.shape
    tm = ROW_TILE
    return pl.pallas_call(
        functools.partial(_combine_kernel, n_lat=n_lat, tm=tm),
        grid=(t // tm,),
        in_specs=[pl.BlockSpec((tm * TOP_K,), lambda i: (i,), memory_space=pltpu.SMEM),
                  pl.BlockSpec((tm, d), lambda i: (i, 0)),
                  pl.BlockSpec((tm, 128), lambda i: (i, 0)),
                  pl.BlockSpec(mod.shape, lambda i: (0, 0)),
                  pl.BlockSpec(memory_space=pl.ANY)],
        out_specs=pl.BlockSpec((tm, d), lambda i: (i, 0)),
        out_shape=jax.ShapeDtypeStruct((t, d), F32),
        scratch_shapes=[pltpu.VMEM((tm * TOP_K,) + ys.shape[1:], F32), pltpu.SemaphoreType.DMA(())],
        compiler_params=_cparams(("arbitrary",)),
        name="moe_combine",
    )(dest_tiles, x1, gw, mod, ys)


def _moe_plan(gi, rk, counts):
    t = gi.shape[0]
    top_i = gi[:, :TOP_K]
    rank = rk[:, :TOP_K]
    cnt = counts[0, :N_EXPERTS].astype(jnp.int32)
    padded = (cnt + MOE_BLOCK - 1) // MOE_BLOCK * MOE_BLOCK
    pad_ends = jnp.cumsum(padded)
    pad_starts = pad_ends - padded
    dest = pad_starts[top_i] + rank
    n_rows = t * TOP_K + N_EXPERTS * MOE_BLOCK
    blk_start = jnp.arange(n_rows // MOE_BLOCK, dtype=jnp.int32) * MOE_BLOCK
    blk_e = jnp.minimum(jnp.sum(blk_start[:, None] >= pad_ends[None, :], axis=1), N_EXPERTS - 1).astype(jnp.int32)
    n_used = (pad_ends[-1:] // MOE_BLOCK).astype(jnp.int32)
    has_rows = cnt > 0
    later = jnp.where(has_rows[None, :] & (jnp.arange(N_EXPERTS)[None, :] > jnp.arange(N_EXPERTS)[:, None]),
                      jnp.arange(N_EXPERTS)[None, :], N_EXPERTS)
    next_of = jnp.min(later, axis=1)
    next_of = jnp.where(next_of < N_EXPERTS, next_of, -1).astype(jnp.int32)
    run_of = (jnp.cumsum(has_rows.astype(jnp.int32)) - 1).astype(jnp.int32)
    next_e = next_of[blk_e]
    slot = run_of[blk_e] & 1
    dest_tiles = dest.reshape(t // ROW_TILE, ROW_TILE, TOP_K).transpose(0, 2, 1).reshape(-1).astype(jnp.int32)
    return dest_tiles, (blk_e, next_e, slot, n_used), n_rows


def _rope_tables(n_lat, n_ctx):
    pos = jnp.arange(n_lat)
    row = (pos // GRID_W).astype(F32)
    col = (pos % GRID_W).astype(F32)
    freq = ROPE_BASE ** (-jnp.arange(ROPE_PAIRS, dtype=F32) / ROPE_PAIRS)
    ra = row[:, None] * freq
    ca = col[:, None] * freq
    cos64 = jnp.concatenate([jnp.cos(ra), jnp.cos(ra), jnp.cos(ca), jnp.cos(ca)], axis=1)
    sin64 = jnp.concatenate([-jnp.sin(ra), jnp.sin(ra), -jnp.sin(ca), jnp.sin(ca)], axis=1)
    cos_t = jnp.concatenate([jnp.tile(cos64, (1, DA_W // DA_DH)), jnp.ones((n_ctx, DA_W), F32)], axis=0)
    sin_t = jnp.concatenate([jnp.tile(sin64, (1, DA_W // DA_DH)), jnp.zeros((n_ctx, DA_W), F32)], axis=0)
    return cos_t, sin_t


def kernel(x, c, ctx, c_ctx, ada_w, ada_b, norm1_g, norm2_g, w_in, qn_g, kn_g, lam_qk, subln_g, hg_lb,
           hg_norm_g, w_branch_a, w_branch_b, w_out, router_w, router_b, w_gu, b_gu, w_down, b_down):
    bsz, n_lat, d = x.shape
    n_ctx = ctx.shape[1]
    depth = ada_w.shape[0]
    assert bsz == 1 and d == D_MODEL and n_lat % 512 == 0 and n_ctx % ROW_TILE == 0
    t = n_lat + n_ctx

    xx = jnp.concatenate([x[0], ctx[0]], axis=0)
    cc = jnp.zeros((8, d), F32).at[0].set(c[0]).at[1].set(c_ctx)
    mods = _modulation(cc, ada_w, ada_b)

    cos_t, sin_t = _rope_tables(n_lat, n_ctx)
    lane = jnp.arange(DA_W)
    seg64 = (lane[:, None] // DA_DH == lane[None, :] // DA_DH).astype(BF16)
    cs = jnp.cumsum(jax.nn.softmax(hg_lb.astype(F32), axis=1), axis=1)
    lb_all = cs - cs[:, :1]
    rw_pad = jnp.zeros((depth, d, 128), F32).at[:, :, :N_EXPERTS].set(router_w)
    rb_pad = jnp.zeros((depth, 1, 128), F32).at[:, 0, :N_EXPERTS].set(router_b)

    key_chunk = next(c for c in ATT_KEY_CHUNKS if t % c == 0 and n_ctx <= c)
    xs = None
    for l in range(depth):
        lam_init = 0.8 - 0.6 * math.exp(-0.3 * l)
        mod = mods[l]
        p = _inproj(xx, norm1_g[l][None], mod, w_in[l].astype(BF16), n_lat)
        k_r, q_t, v_t3, qn2, kn2_tiles = _prep(p, cos_t, sin_t, seg64, jnp.tile(qn_g[l], DA_W // DA_DH)[None],
                                               jnp.tile(kn_g[l], DA_W // DA_DH)[None], key_chunk)
        oa = _attention_all(q_t, k_r, v_t3, qn2, kn2_tiles, lam_qk[l], subln_g[l][:, None], lam_init,
                            n_lat, n_ctx)
        o_f, o_b = _hgrn(p, lb_all[0, l][None], lb_all[1, l][None], n_lat)
        x1, h2, gw, gi, rk, counts = _post(
            xx, oa, o_f, o_b, p, mod, jnp.tile(hg_norm_g[l], HG_HEADS)[None], norm2_g[l][None],
            w_branch_a[l].astype(BF16), w_branch_b[l].astype(BF16), w_out[l].astype(BF16),
            rw_pad[l], rb_pad[l], n_lat)
        dest_tiles, plan, n_rows = _moe_plan(gi, rk, counts)
        xs = _dispatch(h2, dest_tiles, jnp.zeros((n_rows,) + h2.shape[1:], F32) if xs is None else xs)
        ys = _experts(xs, plan, w_gu, b_gu[:, :, None, :], w_down, b_down[:, :, None, :], l)
        xx = _combine(x1, gw, mod, ys, dest_tiles, n_lat)
    return xx[:n_lat][None]
```

```python
import functools
import math

import jax
import jax.numpy as jnp
from jax import lax
from jax.experimental import pallas as pl
from jax.experimental.pallas import tpu as pltpu

F32 = jnp.float32
BF16 = jnp.bfloat16
QK_DTYPE = jnp.float8_e4m3fn

D_MODEL = 1024
GRID_W = 64
EPS = 1e-6

DA_HEADS = 4
DA_DH = 64
DA_DV = 2 * DA_DH
DA_W = DA_HEADS * DA_DV
DA_SCALE = DA_DH ** -0.5
ROPE_PAIRS = DA_DH // 4
ROPE_BASE = 10000.0

HG_HEADS = 4
HG_DK = 128
HG_W = HG_HEADS * HG_DK
HG_CHUNK = 128

N_EXPERTS = 32
TOP_K = 4
D_FF = 1024
SWIGLU_ALPHA = 1.702
SWIGLU_LIMIT = 7.0
MOE_BLOCK = 256

IN_COLS = 6144
ROW_TILE = 256
VT_ROWS = DA_DV + 16
NEG_BIG = -1e30
LOG2E = 1.4426950408889634

VMEM_LIMIT = 56 * 1024 * 1024
ATT_TQ = 256
ATT_KEY_CHUNKS = (3328, 1280, 256)
DMA_QUEUES = 2
SCORE_BOUND_MAX = 40.0
SCORE_BOUND_SLACK = 1.01


def _cparams(sem):
    return pltpu.CompilerParams(dimension_semantics=sem, vmem_limit_bytes=VMEM_LIMIT)


def _split3(x):
    a = x.astype(BF16)
    r = x - a.astype(F32)
    b = r.astype(BF16)
    c = (r - b.astype(F32)).astype(BF16)
    return a, b, c


def _dot(a, b):
    return jnp.dot(a, b, preferred_element_type=F32)


def _dot_nt(a, b):
    return lax.dot_general(a, b, (((1,), (1,)), ((), ())), preferred_element_type=F32)


def _dot_tn(a, b):
    return lax.dot_general(a, b, (((0,), (0,)), ((), ())), preferred_element_type=F32)


def _sigmoid(x):
    return 1.0 / (1.0 + jnp.exp(-x))


def _tile_transpose(x):
    g = x.shape[0]
    sub = lax.broadcasted_iota(jnp.int32, (1, 1, 1, 1, 8, 128), 4)
    x = x.reshape(g, 2, 2, 2, 8, 128)
    for axis, k in ((1, 4), (2, 2), (3, 1)):
        lo = (sub & k) == 0
        a = lax.index_in_dim(x, 0, axis, keepdims=True)
        b = lax.index_in_dim(x, 1, axis, keepdims=True)
        a2 = jnp.where(lo, a, pltpu.roll(b, k, 4))
        b2 = jnp.where(lo, pltpu.roll(a, 8 - k, 4), b)
        x = jnp.concatenate([a2, b2], axis=axis)
    return x.reshape(g, 8, 8, 128)


def _store_row_tiles(ref, val):
    n = val.shape[0]
    z = jnp.stack([val[:, s * 128:(s + 1) * 128].reshape(n // 8, 8, 128) for s in range(8)], axis=1)
    ref[...] = _tile_transpose(z).reshape(n, 8, 128)


def _load_row_tiles(ref, lo, n):
    y = _tile_transpose(ref[lo:lo + n].reshape(n // 8, 8, 8, 128))
    return jnp.concatenate([y[:, s].reshape(n, 128) for s in range(8)], axis=1)


def _mod_kernel(c_ref, w_ref, b_ref, o_ref):
    cv = c_ref[...]
    a = cv * _sigmoid(cv)
    w = w_ref[0]
    a1, a2, a3 = _split3(a)
    w1, w2, w3 = _split3(w)
    acc = _dot(a1, w1) + (_dot(a1, w2) + _dot(a2, w1)) + (_dot(a2, w2) + _dot(a1, w3) + _dot(a3, w1))
    o_ref[0] = acc + b_ref[0]


def _modulation(cc, ada_w, ada_b):
    depth, d, n = ada_w.shape
    tn = 1536
    return pl.pallas_call(
        _mod_kernel,
        grid=(depth, n // tn),
        in_specs=[
            pl.BlockSpec((8, d), lambda l, j: (0, 0)),
            pl.BlockSpec((1, d, tn), lambda l, j: (l, 0, j)),
            pl.BlockSpec((1, 1, tn), lambda l, j: (l, 0, j)),
        ],
        out_specs=pl.BlockSpec((1, 8, tn), lambda l, j: (l, 0, j)),
        out_shape=jax.ShapeDtypeStruct((depth, 8, n), F32),
        compiler_params=_cparams(("parallel", "parallel")),
        name="adaln_mod",
    )(cc, ada_w, ada_b.reshape(depth, 1, n))


def _inproj_kernel(x_ref, g_ref, mod_ref, w_ref, o_ref, h_scr, *, n_lat, tm):
    i = pl.program_id(0)
    j = pl.program_id(1)
    d = x_ref.shape[1]

    @pl.when(j == 0)
    def _():
        x = x_ref[...]
        y = x * lax.rsqrt(jnp.mean(x * x, axis=-1, keepdims=True) + EPS) * g_ref[...]
        row = i * tm + lax.broadcasted_iota(jnp.int32, (tm, 1), 0)
        lat = row < n_lat
        sh = jnp.where(lat, mod_ref[0:1, 0:d], mod_ref[1:2, 0:d])
        sc = jnp.where(lat, mod_ref[0:1, d:2 * d], mod_ref[1:2, d:2 * d])
        h_scr[...] = (y * (1.0 + sc) + sh).astype(BF16)

    o_ref[...] = _dot(h_scr[...], w_ref[...])


def _inproj(x, g, mod, w_bf, n_lat):
    t, d = x.shape
    n = w_bf.shape[1]
    tm = 1280 if t % 1280 == 0 else ROW_TILE
    tn = 1536
    return pl.pallas_call(
        functools.partial(_inproj_kernel, n_lat=n_lat, tm=tm),
        grid=(t // tm, n // tn),
        in_specs=[
            pl.BlockSpec((tm, d), lambda i, j: (i, 0)),
            pl.BlockSpec((1, d), lambda i, j: (0, 0)),
            pl.BlockSpec((8, mod.shape[1]), lambda i, j: (0, 0)),
            pl.BlockSpec((d, tn), lambda i, j: (0, j)),
        ],
        out_specs=pl.BlockSpec((tm, tn), lambda i, j: (i, j)),
        out_shape=jax.ShapeDtypeStruct((t, n), F32),
        scratch_shapes=[pltpu.VMEM((tm, d), BF16)],
        compiler_params=_cparams(("parallel", "arbitrary")),
        name="inproj",
    )(x, g, mod, w_bf)


def _segment_mean_sq(x, seg_ref, width):
    x2 = x * x
    hi = x2.astype(BF16)
    lo = (x2 - hi.astype(F32)).astype(BF16)
    seg = seg_ref[...]
    return (_dot(hi, seg) + _dot(lo, seg)) * (1.0 / width)


def _prep_kernel(q_ref, k_ref, v_ref, cos_ref, sin_ref, seg_ref, qg_ref, kg_ref,
                 k_out, qt_out, vt_out, qn_out, kn_out):
    tm = q_ref.shape[0]
    cos = cos_ref[...]
    sin = sin_ref[...]
    lane = lax.broadcasted_iota(jnp.int32, (tm, DA_W), 1)
    first_half = (lane & 31) < 16

    def norm_rope(x, g):
        y = x * lax.rsqrt(_segment_mean_sq(x, seg_ref, DA_DH) + EPS) * g
        fwd = pltpu.roll(y, DA_W - 16, 1)
        bwd = pltpu.roll(y, 16, 1)
        partner = jnp.where(first_half, fwd, bwd)
        return y * cos + partner * sin

    k_b = norm_rope(k_ref[...], kg_ref[...]).astype(QK_DTYPE)
    k_out[...] = k_b
    k_f = k_b.astype(F32)
    k_n2 = _dot((k_f * k_f).astype(BF16), seg_ref[...])
    kn_out[...] = jnp.broadcast_to(jnp.max(k_n2, axis=0, keepdims=True), kn_out.shape)

    q = norm_rope(q_ref[...], qg_ref[...]) * (DA_SCALE * LOG2E)
    lane_h = lax.broadcasted_iota(jnp.int32, (tm, DA_DV), 1)
    v = v_ref[...]
    ones = jnp.ones((VT_ROWS - DA_DV, tm), BF16)
    for h in range(DA_HEADS):
        qh = q[:, h * DA_DV:(h + 1) * DA_DV]
        for m in range(2):
            keep = (lane_h < DA_DH) if m == 0 else (lane_h >= DA_DH)
            r = 2 * h + m
            qt_b = jnp.where(keep, qh, 0.0).T.astype(QK_DTYPE)
            qt_out[r * DA_DV:(r + 1) * DA_DV, :] = qt_b
            qt_f = qt_b.astype(F32)
            qn_out[r:r + 1, :] = jnp.sum(qt_f * qt_f, axis=0, keepdims=True)
        vt_out[0, h * VT_ROWS:h * VT_ROWS + DA_DV, :] = v[:, h * DA_DV:(h + 1) * DA_DV].T.astype(BF16)
        vt_out[0, h * VT_ROWS + DA_DV:(h + 1) * VT_ROWS, :] = ones


def _prep(p, cos_t, sin_t, seg64, qg, kg, chunk):
    t = p.shape[0]
    tm = ROW_TILE
    per_chunk = chunk // tm
    return pl.pallas_call(
        _prep_kernel,
        grid=(t // tm,),
        in_specs=[
            pl.BlockSpec((tm, DA_W), lambda i: (i, 0)),
            pl.BlockSpec((tm, DA_W), lambda i: (i, 1)),
            pl.BlockSpec((tm, DA_W), lambda i: (i, 2)),
            pl.BlockSpec((tm, DA_W), lambda i: (i, 0)),
            pl.BlockSpec((tm, DA_W), lambda i: (i, 0)),
            pl.BlockSpec((DA_W, DA_W), lambda i: (0, 0)),
            pl.BlockSpec((1, DA_W), lambda i: (0, 0)),
            pl.BlockSpec((1, DA_W), lambda i: (0, 0)),
        ],
        out_specs=[
            pl.BlockSpec((tm, DA_W), lambda i: (i, 0)),
            pl.BlockSpec((2 * DA_HEADS * DA_DV, tm), lambda i: (0, i)),
            pl.BlockSpec((1, DA_HEADS * VT_ROWS, tm), lambda i: (i // per_chunk, 0, i % per_chunk)),
            pl.BlockSpec((2 * DA_HEADS, tm), lambda i: (0, i)),
            pl.BlockSpec((8, DA_W), lambda i: (i, 0)),
        ],
        out_shape=[
            jax.ShapeDtypeStruct((t, DA_W), QK_DTYPE),
            jax.ShapeDtypeStruct((2 * DA_HEADS * DA_DV, t), QK_DTYPE),
            jax.ShapeDtypeStruct((t // chunk, DA_HEADS * VT_ROWS, chunk), BF16),
            jax.ShapeDtypeStruct((2 * DA_HEADS, t), F32),
            jax.ShapeDtypeStruct((8 * (t // tm), DA_W), F32),
        ],
        compiler_params=_cparams(("parallel",)),
        name="qkv_prep",
    )(p, p, p, cos_t, sin_t, seg64, qg, kg)


def _attn_finalize(lam_ref, sg_ref, o_ref, acc_scr, lam_init):
    lq = lam_ref[...]
    lam = (jnp.exp(jnp.sum(lq[0:1] * lq[1:2], axis=-1, keepdims=True))
           - jnp.exp(jnp.sum(lq[2:3] * lq[3:4], axis=-1, keepdims=True)) + lam_init)
    for h in range(DA_HEADS):
        a1 = acc_scr[2 * h]
        a2 = acc_scr[2 * h + 1]
        o = a1[0:DA_DV] / a1[DA_DV:DA_DV + 1] - lam * (a2[0:DA_DV] / a2[DA_DV:DA_DV + 1])
        o = o * lax.rsqrt(jnp.mean(o * o, axis=0, keepdims=True) + EPS)
        o = o * sg_ref[...] * (1.0 - lam_init)
        o_ref[:, h * DA_DV:(h + 1) * DA_DV] = o.T.astype(o_ref.dtype)


def _attn_kernel(qt_ref, k_hbm, vt_hbm, lam_ref, sg_ref, o_ref, k_scr, vt_scr, m_scr, acc_scr, sems,
                 *, lam_init, stabilised):
    @pl.when(pl.program_id(0) == 0)
    def _():
        copies = (pltpu.make_async_copy(k_hbm, k_scr, sems.at[0]), pltpu.make_async_copy(vt_hbm, vt_scr, sems.at[1]))
        for cp in copies:
            cp.start()
        for cp in copies:
            cp.wait()

    acc_scr[...] = jnp.zeros(acc_scr.shape, F32)
    if stabilised:
        m_scr[...] = jnp.full(m_scr.shape, NEG_BIG, F32)

    def chunk(c, carry):
        for h in range(DA_HEADS):
            k_h = k_scr[c, :, h * DA_DV:(h + 1) * DA_DV]
            vt_h = vt_scr[c, h * VT_ROWS:(h + 1) * VT_ROWS, :]
            for m in range(2):
                r = 2 * h + m
                s = _dot(k_h, qt_ref[r * DA_DV:(r + 1) * DA_DV, :])
                if stabilised:
                    m_prev = m_scr[r:r + 1, :]
                    m_new = jnp.maximum(m_prev, jnp.max(s, axis=0, keepdims=True))
                    p = jnp.exp2((s - m_new).astype(BF16))
                    acc_scr[r] = jnp.exp2(m_prev - m_new) * acc_scr[r] + _dot(vt_h, p)
                    m_scr[r:r + 1, :] = m_new
                else:
                    acc_scr[r] += _dot(vt_h, jnp.exp2(s.astype(BF16)))
        return carry

    lax.fori_loop(0, k_scr.shape[0], chunk, 0)
    _attn_finalize(lam_ref, sg_ref, o_ref, acc_scr, lam_init)


def _attention(qt, k3, vt3, lam_qk, subln_col, lam_init, stabilised, *, q_start, n_q, tq):
    qo = q_start // tq
    return pl.pallas_call(
        functools.partial(_attn_kernel, lam_init=lam_init, stabilised=stabilised),
        grid=(n_q // tq,),
        in_specs=[
            pl.BlockSpec((qt.shape[0], tq), lambda i: (0, i + qo)),
            pl.BlockSpec(memory_space=pl.ANY),
            pl.BlockSpec(memory_space=pl.ANY),
            pl.BlockSpec((4, DA_DH), lambda i: (0, 0)),
            pl.BlockSpec((DA_DV, 1), lambda i: (0, 0)),
        ],
        out_specs=pl.BlockSpec((tq, DA_W), lambda i: (i, 0)),
        out_shape=jax.ShapeDtypeStruct((n_q, DA_W), BF16),
        scratch_shapes=[
            pltpu.VMEM(k3.shape, k3.dtype),
            pltpu.VMEM(vt3.shape, vt3.dtype),
            pltpu.VMEM((2 * DA_HEADS, tq), F32),
            pltpu.VMEM((2 * DA_HEADS, VT_ROWS, tq), F32),
            pltpu.SemaphoreType.DMA((2,)),
        ],
        compiler_params=_cparams(("arbitrary",)),
        name="diff_attn_online" if stabilised else "diff_attn_plain",
    )(qt, k3, vt3, lam_qk, subln_col)


def _attention_all(qt, k, vt3, qn2, kn2_tiles, lam_qk, subln_col, lam_init, n_lat, n_ctx):
    n_chunks, _, chunk = vt3.shape
    n_maps = 2 * DA_HEADS
    kmax2 = jnp.max(kn2_tiles, axis=0).reshape(n_maps, DA_DH)[:, 0]
    bound = jnp.sqrt(jnp.max(qn2, axis=1)) * jnp.sqrt(kmax2) * SCORE_BOUND_SLACK
    bounded = jnp.max(bound) <= SCORE_BOUND_MAX
    k3 = k.reshape(n_chunks, chunk, DA_W)

    def run(stabilised):
        def f(args):
            qt_, k3_, vt3_, lam_, sg_ = args
            lat = _attention(qt_, k3_, vt3_, lam_, sg_, lam_init, stabilised, q_start=0, n_q=n_lat, tq=ATT_TQ)
            ctx = _attention(qt_, k3_[n_chunks - 1:, chunk - n_ctx:, :], vt3_[n_chunks - 1:, :, chunk - n_ctx:],
                             lam_, sg_, lam_init, stabilised, q_start=n_lat, n_q=n_ctx, tq=ATT_TQ)
            return jnp.concatenate([lat, ctx], axis=0)
        return f

    return lax.cond(bounded, run(False), run(True), (qt, k3, vt3, lam_qk, subln_col))


def _hg_chunk(q, z, v, lb, st_ref, rev):
    c, w = q.shape
    row = lax.broadcasted_iota(jnp.int32, (c, 1), 0)

    u = jnp.exp(-jnp.abs(z))
    sig_abs = 1.0 / (1.0 + u)
    y = jnp.log(1.0 - lb) + jnp.minimum(z, 0.0) - jnp.log(1.0 + u)
    log_lb = jnp.log(lb)
    log_f = jnp.maximum(log_lb, y) + jnp.log(1.0 + jnp.exp(-jnp.abs(log_lb - y)))
    key = (1.0 - lb) * jnp.where(z >= 0.0, u * sig_abs, sig_abs)

    ri = lax.broadcasted_iota(jnp.int32, (c, c), 0)
    ci = lax.broadcasted_iota(jnp.int32, (c, c), 1)
    tri = jnp.where((ci >= ri) if rev else (ci <= ri), 1.0, 0.0).astype(BF16)
    g1, g2, g3 = _split3(log_f)
    b = _dot(tri, g1) + _dot(tri, g2) + _dot(tri, g3)
    end = 0 if rev else c - 1
    b_end = b[end:end + 1, :]

    q_in = (q * jnp.exp(b)).astype(BF16)
    k_out = (key * jnp.exp(b_end - b)).astype(BF16)
    decay = jnp.exp(b_end)

    up = pltpu.roll(log_f, 1, 0)
    dn = pltpu.roll(log_f, c - 1, 0)
    q_lv, k_lv, masks = [], [], []
    lvl = c // 2
    while lvl >= 1:
        half = (row // lvl) & 1
        q_rows = (half == 0) if rev else (half == 1)
        if lvl >= 4:
            pos = lvl if rev else lvl - 1
            ref = jnp.broadcast_to(b.reshape(c // (2 * lvl), 2 * lvl, w)[:, pos:pos + 1, :],
                                   (c // (2 * lvl), 2 * lvl, w)).reshape(c, w)
            eq, ek = b - ref, ref - b
        elif lvl == 2:
            j = row & 3
            if rev:
                eq = jnp.where(j == 1, log_f, log_f + dn)
                ek = jnp.where(j == 2, 0.0, up)
            else:
                eq = jnp.where(j == 2, log_f, log_f + up)
                ek = jnp.where(j == 1, 0.0, dn)
        else:
            eq, ek = log_f, jnp.zeros_like(log_f)
        q_lv.append((q * jnp.exp(jnp.where(q_rows, eq, NEG_BIG))).astype(BF16))
        k_lv.append((key * jnp.exp(jnp.where(q_rows, NEG_BIG, ek))).astype(BF16))
        masks.append(None if 2 * lvl == c else
                     jnp.where((ri // (2 * lvl)) == (ci // (2 * lvl)), 1.0, 0.0))
        lvl //= 2
    qk = q * key
    on_diag = ri == ci

    outs = []
    for h in range(HG_HEADS):
        sl = slice(h * HG_DK, (h + 1) * HG_DK)
        a = jnp.where(on_diag, jnp.sum(qk[:, sl], axis=-1, keepdims=True), 0.0)
        for q_l, k_l, msk in zip(q_lv, k_lv, masks):
            a_l = _dot_nt(q_l[:, sl], k_l[:, sl])
            a = a + (a_l if msk is None else a_l * msk)
        st = st_ref[h]
        v_h = v[:, sl].astype(BF16)
        outs.append(_dot_nt(q_in[:, sl], st.astype(BF16)) + _dot(a.astype(BF16), v_h))
        st_ref[h] = st * decay[:, sl] + _dot_tn(v_h, k_out[:, sl])
    return jnp.concatenate(outs, axis=1)


def _hgrn_kernel(qf_ref, zf_ref, vf_ref, qb_ref, zb_ref, vb_ref, lbf_ref, lbb_ref,
                 of_ref, ob_ref, sf_scr, sb_scr):
    @pl.when(pl.program_id(0) == 0)
    def _():
        sf_scr[...] = jnp.zeros(sf_scr.shape, F32)
        sb_scr[...] = jnp.zeros(sb_scr.shape, F32)

    n_chunks = qf_ref.shape[0] // HG_CHUNK

    def body(ci, carry):
        rf = pl.ds(pl.multiple_of(ci * HG_CHUNK, HG_CHUNK), HG_CHUNK)
        of_ref[rf, :] = _hg_chunk(qf_ref[rf, :], zf_ref[rf, :], vf_ref[rf, :], lbf_ref[...], sf_scr, False)
        rb = pl.ds(pl.multiple_of((n_chunks - 1 - ci) * HG_CHUNK, HG_CHUNK), HG_CHUNK)
        ob_ref[rb, :] = _hg_chunk(qb_ref[rb, :], zb_ref[rb, :], vb_ref[rb, :], lbb_ref[...], sb_scr, True)
        return carry

    lax.fori_loop(0, n_chunks, body, 0)


def _hgrn(p, lb_f, lb_b, n_lat):
    t = p.shape[0]
    tm = ROW_TILE
    nb = t // tm
    n_lat_b = n_lat // tm
    n_ctx_b = nb - n_lat_b

    def jf(i):
        return jnp.where(i < n_ctx_b, n_lat_b + i, i - n_ctx_b)

    def jb(i):
        return jnp.where(i < n_ctx_b, nb - 1 - i, nb - 1 - i)

    def spec(order, colblk):
        return pl.BlockSpec((tm, HG_W), lambda i: (order(i), colblk))

    return pl.pallas_call(
        _hgrn_kernel,
        grid=(nb,),
        in_specs=[spec(jf, 3), spec(jf, 4), spec(jf, 6), spec(jb, 3), spec(jb, 5), spec(jb, 6),
                  pl.BlockSpec((1, HG_W), lambda i: (0, 0)), pl.BlockSpec((1, HG_W), lambda i: (0, 0))],
        out_specs=[pl.BlockSpec((tm, HG_W), lambda i: (jf(i), 0)),
                   pl.BlockSpec((tm, HG_W), lambda i: (jb(i), 0))],
        out_shape=[jax.ShapeDtypeStruct((t, HG_W), F32), jax.ShapeDtypeStruct((t, HG_W), F32)],
        scratch_shapes=[pltpu.VMEM((HG_HEADS, HG_DK, HG_DK), F32), pltpu.VMEM((HG_HEADS, HG_DK, HG_DK), F32)],
        compiler_params=_cparams(("arbitrary",)),
        name="hgrn2",
    )(p, p, p, p, p, p, lb_f, lb_b)


def _post_kernel(x_ref, oa_ref, of_ref, ob_ref, gh_ref, ga_ref, gb_ref, mod_ref, hg_ref, n2_ref,
                 wa_ref, wb_ref, wo_ref, rw_ref, rb_ref,
                 x1_ref, h2_ref, gw_ref, gi_ref, rk_ref, cnt_ref, run_scr, *, n_lat, tm):
    i = pl.program_id(0)
    d = x_ref.shape[1]

    @pl.when(i == 0)
    def _():
        run_scr[...] = jnp.zeros(run_scr.shape, F32)

    lat = i * tm < n_lat

    def mod(c):
        return jnp.where(lat, mod_ref[0:1, c * d:(c + 1) * d], mod_ref[1:2, c * d:(c + 1) * d])

    o = of_ref[...] + ob_ref[...]
    parts = []
    for h in range(HG_HEADS):
        oh = o[:, h * HG_DK:(h + 1) * HG_DK]
        parts.append(oh * lax.rsqrt(jnp.mean(oh * oh, axis=-1, keepdims=True) + EPS))
    gh = gh_ref[...]
    ob = jnp.concatenate(parts, axis=1) * hg_ref[...] * (gh * _sigmoid(gh))
    ya = _dot(oa_ref[...], wa_ref[...])
    yb = _dot(ob.astype(BF16), wb_ref[...])
    mix = _sigmoid(ga_ref[...]) * ya + _sigmoid(gb_ref[...]) * yb
    x1 = x_ref[...] + mod(2) * _dot(mix.astype(BF16), wo_ref[...])
    x1_ref[...] = x1

    h2 = x1 * lax.rsqrt(jnp.mean(x1 * x1, axis=-1, keepdims=True) + EPS) * n2_ref[...]
    h2 = h2 * (1.0 + mod(4)) + mod(3)
    _store_row_tiles(h2_ref, h2)

    h_hi = h2.astype(BF16)
    h_lo = (h2 - h_hi.astype(F32)).astype(BF16)
    rw = rw_ref[...]
    r_hi = rw.astype(BF16)
    r_lo = (rw - r_hi.astype(F32)).astype(BF16)
    logits = _dot(h_hi, r_hi) + (_dot(h_hi, r_lo) + _dot(h_lo, r_hi)) + rb_ref[...]

    lane = lax.broadcasted_iota(jnp.int32, logits.shape, 1)
    work = jnp.where(lane < N_EXPERTS, logits, -jnp.inf)
    sel_any = jnp.zeros(logits.shape, F32)
    vals, idxs = [], []
    for _ in range(TOP_K):
        mx = jnp.max(work, axis=-1, keepdims=True)
        idx = jnp.min(jnp.where(work == mx, lane, 2 * N_EXPERTS), axis=-1, keepdims=True)
        hit = lane == idx
        vals.append(mx)
        idxs.append(idx)
        sel_any = jnp.where(hit, 1.0, sel_any)
        work = jnp.where(hit, -jnp.inf, work)
    es = [jnp.exp(vk - vals[0]) for vk in vals]
    denom = es[0] + es[1] + es[2] + es[3]

    ri = lax.broadcasted_iota(jnp.int32, (tm, tm), 0)
    ci = lax.broadcasted_iota(jnp.int32, (tm, tm), 1)
    below = jnp.where(ci < ri, 1.0, 0.0).astype(BF16)
    rank_e = run_scr[0:1, :] + _dot(below, sel_any.astype(BF16))
    gw = jnp.zeros(logits.shape, F32)
    gi = jnp.zeros(logits.shape, jnp.int32)
    rk = jnp.zeros(logits.shape, jnp.int32)
    for k in range(TOP_K):
        at_k = lane == k
        gw = jnp.where(at_k, es[k] / denom, gw)
        gi = jnp.where(at_k, idxs[k], gi)
        rnk = jnp.sum(jnp.where(lane == idxs[k], rank_e, 0.0), axis=-1, keepdims=True)
        rk = jnp.where(at_k, rnk.astype(jnp.int32), rk)
    gw_ref[...] = gw
    gi_ref[...] = gi
    rk_ref[...] = rk
    run = run_scr[0:1, :] + jnp.sum(sel_any, axis=0, keepdims=True)
    run_scr[...] = jnp.broadcast_to(run, run_scr.shape)
    cnt_ref[...] = jnp.broadcast_to(run, cnt_ref.shape)


def _post(x, oa, o_f, o_b, p, mod, hg_g, n2_g, wa, wb, wo, rw_pad, rb_pad, n_lat):
    t, d = x.shape
    tm = ROW_TILE

    def rows(width, colblk=0):
        return pl.BlockSpec((tm, width), lambda i: (i, colblk))

    def whole(a):
        return pl.BlockSpec(a.shape, lambda i: (0,) * a.ndim)

    return pl.pallas_call(
        functools.partial(_post_kernel, n_lat=n_lat, tm=tm),
        grid=(t // tm,),
        in_specs=[rows(d), rows(DA_W), rows(HG_W), rows(HG_W),
                  rows(HG_W, 7), rows(d, 4), rows(d, 5),
                  whole(mod), whole(hg_g), whole(n2_g), whole(wa), whole(wb), whole(wo),
                  whole(rw_pad), whole(rb_pad)],
        out_specs=[rows(d), pl.BlockSpec((tm, d // 128, 128), lambda i: (i, 0, 0)), rows(128), rows(128), rows(128),
                   pl.BlockSpec((8, 128), lambda i: (0, 0))],
        out_shape=[jax.ShapeDtypeStruct((t, d), F32), jax.ShapeDtypeStruct((t, d // 128, 128), F32),
                   jax.ShapeDtypeStruct((t, 128), F32), jax.ShapeDtypeStruct((t, 128), jnp.int32),
                   jax.ShapeDtypeStruct((t, 128), jnp.int32), jax.ShapeDtypeStruct((8, 128), F32)],
        scratch_shapes=[pltpu.VMEM((8, 128), F32)],
        compiler_params=_cparams(("arbitrary",)),
        name="merge_router",
    )(x, oa, o_f, o_b, p, p, p, mod, hg_g, n2_g, wa, wb, wo, rw_pad, rb_pad)


def _dispatch_kernel(dest_ref, h_ref, xs_in, xs_out, sem):
    del xs_in
    n = dest_ref.shape[0]
    tm = h_ref.shape[0]

    def start(j, c):
        for q in range(DMA_QUEUES):
            r = DMA_QUEUES * j + q
            pltpu.make_async_copy(h_ref.at[r & (tm - 1)], xs_out.at[dest_ref[r]], sem).start(priority=q)
        return c

    lax.fori_loop(0, n // DMA_QUEUES, start, 0, unroll=4)
    pltpu.make_async_copy(xs_out.at[pl.ds(0, n)], xs_out.at[pl.ds(0, n)], sem).wait()


def _dispatch(h2, dest_tiles, xs_zero):
    t = h2.shape[0]
    tm = ROW_TILE
    return pl.pallas_call(
        _dispatch_kernel,
        grid=(t // tm,),
        in_specs=[pl.BlockSpec((tm * TOP_K,), lambda i: (i,), memory_space=pltpu.SMEM),
                  pl.BlockSpec((tm,) + h2.shape[1:], lambda i: (i, 0, 0)),
                  pl.BlockSpec(memory_space=pl.ANY)],
        out_specs=pl.BlockSpec(memory_space=pl.ANY),
        out_shape=jax.ShapeDtypeStruct(xs_zero.shape, xs_zero.dtype),
        scratch_shapes=[pltpu.SemaphoreType.DMA(())],
        input_output_aliases={2: 0},
        compiler_params=_cparams(("arbitrary",)),
        name="moe_dispatch",
    )(dest_tiles, h2, xs_zero)


def _expert_kernel(be_ref, nx_ref, sl_ref, nu_ref, xs_ref, wgu_hbm, bgu_ref, wd_hbm, bd_ref, ys_ref,
                   wgu_f32, wd_f32, wgu_bf, wd_bf, sems, *, layer):
    i = pl.program_id(0)
    used = i < nu_ref[0]

    def weight_copies(e, slot):
        return (pltpu.make_async_copy(wgu_hbm.at[layer, e], wgu_f32.at[slot], sems.at[0, slot]),
                pltpu.make_async_copy(wd_hbm.at[layer, e], wd_f32.at[slot], sems.at[1, slot]))

    @pl.when(i == 0)
    def _():
        for cp in weight_copies(be_ref[0], 0):
            cp.start()

    @pl.when(jnp.logical_and(used, jnp.logical_or(i == 0, be_ref[i] != be_ref[jnp.maximum(i - 1, 0)])))
    def _():
        slot = sl_ref[i]
        for cp in weight_copies(be_ref[i], slot):
            cp.wait()

        @pl.when(nx_ref[i] >= 0)
        def _():
            for cp in weight_copies(nx_ref[i], 1 - slot):
                cp.start()

        wgu_bf[...] = wgu_f32[slot].astype(BF16)
        wd_bf[...] = wd_f32[slot].astype(BF16)

    @pl.when(used)
    def _():
        x = _load_row_tiles(xs_ref, 0, xs_ref.shape[0]).astype(BF16)
        gu = _dot(x, wgu_bf[...]) + bgu_ref[0, 0]
        gate = jnp.minimum(gu[:, :D_FF], SWIGLU_LIMIT)
        up = jnp.clip(gu[:, D_FF:], -SWIGLU_LIMIT, SWIGLU_LIMIT)
        hdn = (up + 1.0) * gate * _sigmoid(SWIGLU_ALPHA * gate)
        _store_row_tiles(ys_ref, _dot(hdn.astype(BF16), wd_bf[...]) + bd_ref[0, 0])

    @pl.when(jnp.logical_not(used))
    def _():
        ys_ref[...] = jnp.zeros(ys_ref.shape, F32)


def _experts(xs, plan, wgu, bgu, wd, bd, layer):
    blk_e, next_e, slot, n_used = plan
    n_rows = xs.shape[0]
    d = xs.shape[1] * xs.shape[2]
    nblk = n_rows // MOE_BLOCK

    def blk(i, nu):
        return jnp.minimum(i, nu[0] - 1)

    def bias(i, be, nx, sl, nu):
        return (layer, be[blk(i, nu)], 0, 0)

    grid_spec = pltpu.PrefetchScalarGridSpec(
        num_scalar_prefetch=4,
        grid=(nblk,),
        in_specs=[
            pl.BlockSpec((MOE_BLOCK,) + xs.shape[1:], lambda i, be, nx, sl, nu: (blk(i, nu), 0, 0)),
            pl.BlockSpec(memory_space=pl.ANY),
            pl.BlockSpec((1, 1, 1, 2 * D_FF), bias),
            pl.BlockSpec(memory_space=pl.ANY),
            pl.BlockSpec((1, 1, 1, d), bias),
        ],
        out_specs=pl.BlockSpec((MOE_BLOCK,) + xs.shape[1:], lambda i, be, nx, sl, nu: (i, 0, 0)),
        scratch_shapes=[pltpu.VMEM((2, d, 2 * D_FF), F32), pltpu.VMEM((2, D_FF, d), F32),
                        pltpu.VMEM((d, 2 * D_FF), BF16), pltpu.VMEM((D_FF, d), BF16),
                        pltpu.SemaphoreType.DMA((2, 2))],
    )
    return pl.pallas_call(
        functools.partial(_expert_kernel, layer=layer),
        grid_spec=grid_spec,
        out_shape=jax.ShapeDtypeStruct(xs.shape, F32),
        compiler_params=_cparams(("arbitrary",)),
        name="moe_experts",
    )(blk_e, next_e, slot, n_used, xs, wgu, bgu, wd, bd)


def _combine_kernel(dest_ref, x1_ref, gw_ref, mod_ref, ys_ref, o_ref, buf, sem, *, n_lat, tm):
    i = pl.program_id(0)
    d = x1_ref.shape[1]
    n = dest_ref.shape[0]

    def start(j, c):
        for q in range(DMA_QUEUES):
            r = DMA_QUEUES * j + q
            pltpu.make_async_copy(ys_ref.at[dest_ref[r]], buf.at[r], sem).start(priority=q)
        return c

    lax.fori_loop(0, n // DMA_QUEUES, start, 0, unroll=4)
    pltpu.make_async_copy(ys_ref.at[pl.ds(0, n)], buf, sem).wait()

    lat = i * tm < n_lat
    g2 = jnp.where(lat, mod_ref[0:1, 5 * d:6 * d], mod_ref[1:2, 5 * d:6 * d])
    gw = gw_ref[...]
    m = gw[:, 0:1] * _load_row_tiles(buf, 0, tm)
    for k in range(1, TOP_K):
        m = m + gw[:, k:k + 1] * _load_row_tiles(buf, k * tm, tm)
    o_ref[...] = x1_ref[...] + g2 * m


def _combine(x1, gw, mod, ys, dest_tiles, n_lat):
    t, d = x1.shape
    tm = ROW_TILE
    return pl.pallas_call(
        functools.partial(_combine_kernel, n_lat=n_lat, tm=tm),
        grid=(t // tm,),
        in_specs=[pl.BlockSpec((tm * TOP_K,), lambda i: (i,), memory_space=pltpu.SMEM),
                  pl.BlockSpec((tm, d), lambda i: (i, 0)),
                  pl.BlockSpec((tm, 128), lambda i: (i, 0)),
                  pl.BlockSpec(mod.shape, lambda i: (0, 0)),
                  pl.BlockSpec(memory_space=pl.ANY)],
        out_specs=pl.BlockSpec((tm, d), lambda i: (i, 0)),
        out_shape=jax.ShapeDtypeStruct((t, d), F32),
        scratch_shapes=[pltpu.VMEM((tm * TOP_K,) + ys.shape[1:], F32), pltpu.SemaphoreType.DMA(())],
        compiler_params=_cparams(("arbitrary",)),
        name="moe_combine",
    )(dest_tiles, x1, gw, mod, ys)


def _moe_plan(gi, rk, counts):
    t = gi.shape[0]
    top_i = gi[:, :TOP_K]
    rank = rk[:, :TOP_K]
    cnt = counts[0, :N_EXPERTS].astype(jnp.int32)
    padded = (cnt + MOE_BLOCK - 1) // MOE_BLOCK * MOE_BLOCK
    pad_ends = jnp.cumsum(padded)
    pad_starts = pad_ends - padded
    dest = pad_starts[top_i] + rank
    n_rows = t * TOP_K + N_EXPERTS * MOE_BLOCK
    blk_start = jnp.arange(n_rows // MOE_BLOCK, dtype=jnp.int32) * MOE_BLOCK
    blk_e = jnp.minimum(jnp.sum(blk_start[:, None] >= pad_ends[None, :], axis=1), N_EXPERTS - 1).astype(jnp.int32)
    n_used = (pad_ends[-1:] // MOE_BLOCK).astype(jnp.int32)
    has_rows = cnt > 0
    later = jnp.where(has_rows[None, :] & (jnp.arange(N_EXPERTS)[None, :] > jnp.arange(N_EXPERTS)[:, None]),
                      jnp.arange(N_EXPERTS)[None, :], N_EXPERTS)
    next_of = jnp.min(later, axis=1)
    next_of = jnp.where(next_of < N_EXPERTS, next_of, -1).astype(jnp.int32)
    run_of = (jnp.cumsum(has_rows.astype(jnp.int32)) - 1).astype(jnp.int32)
    next_e = next_of[blk_e]
    slot = run_of[blk_e] & 1
    dest_tiles = dest.reshape(t // ROW_TILE, ROW_TILE, TOP_K).transpose(0, 2, 1).reshape(-1).astype(jnp.int32)
    return dest_tiles, (blk_e, next_e, slot, n_used), n_rows


def _rope_tables(n_lat, n_ctx):
    pos = jnp.arange(n_lat)
    row = (pos // GRID_W).astype(F32)
    col = (pos % GRID_W).astype(F32)
    freq = ROPE_BASE ** (-jnp.arange(ROPE_PAIRS, dtype=F32) / ROPE_PAIRS)
    ra = row[:, None] * freq
    ca = col[:, None] * freq
    cos64 = jnp.concatenate([jnp.cos(ra), jnp.cos(ra), jnp.cos(ca), jnp.cos(ca)], axis=1)
    sin64 = jnp.concatenate([-jnp.sin(ra), jnp.sin(ra), -jnp.sin(ca), jnp.sin(ca)], axis=1)
    cos_t = jnp.concatenate([jnp.tile(cos64, (1, DA_W // DA_DH)), jnp.ones((n_ctx, DA_W), F32)], axis=0)
    sin_t = jnp.concatenate([jnp.tile(sin64, (1, DA_W // DA_DH)), jnp.zeros((n_ctx, DA_W), F32)], axis=0)
    return cos_t, sin_t


def kernel(x, c, ctx, c_ctx, ada_w, ada_b, norm1_g, norm2_g, w_in, qn_g, kn_g, lam_qk, subln_g, hg_lb,
           hg_norm_g, w_branch_a, w_branch_b, w_out, router_w, router_b, w_gu, b_gu, w_down, b_down):
    bsz, n_lat, d = x.shape
    n_ctx = ctx.shape[1]
    depth = ada_w.shape[0]
    assert bsz == 1 and d == D_MODEL and n_lat % 512 == 0 and n_ctx % ROW_TILE == 0
    t = n_lat + n_ctx

    xx = jnp.concatenate([x[0], ctx[0]], axis=0)
    cc = jnp.zeros((8, d), F32).at[0].set(c[0]).at[1].set(c_ctx)
    mods = _modulation(cc, ada_w, ada_b)

    cos_t, sin_t = _rope_tables(n_lat, n_ctx)
    lane = jnp.arange(DA_W)
    seg64 = (lane[:, None] // DA_DH == lane[None, :] // DA_DH).astype(BF16)
    cs = jnp.cumsum(jax.nn.softmax(hg_lb.astype(F32), axis=1), axis=1)
    lb_all = cs - cs[:, :1]
    rw_pad = jnp.zeros((depth, d, 128), F32).at[:, :, :N_EXPERTS].set(router_w)
    rb_pad = jnp.zeros((depth, 1, 128), F32).at[:, 0, :N_EXPERTS].set(router_b)

    key_chunk = next(c for c in ATT_KEY_CHUNKS if t % c == 0 and n_ctx <= c)
    xs = None
    for l in range(depth):
        lam_init = 0.8 - 0.6 * math.exp(-0.3 * l)
        mod = mods[l]
        p = _inproj(xx, norm1_g[l][None], mod, w_in[l].astype(BF16), n_lat)
        k_r, q_t, v_t3, qn2, kn2_tiles = _prep(p, cos_t, sin_t, seg64, jnp.tile(qn_g[l], DA_W // DA_DH)[None],
                                               jnp.tile(kn_g[l], DA_W // DA_DH)[None], key_chunk)
        oa = _attention_all(q_t, k_r, v_t3, qn2, kn2_tiles, lam_qk[l], subln_g[l][:, None], lam_init,
                            n_lat, n_ctx)
        o_f, o_b = _hgrn(p, lb_all[0, l][None], lb_all[1, l][None], n_lat)
        x1, h2, gw, gi, rk, counts = _post(
            xx, oa, o_f, o_b, p, mod, jnp.tile(hg_norm_g[l], HG_HEADS)[None], norm2_g[l][None],
            w_branch_a[l].astype(BF16), w_branch_b[l].astype(BF16), w_out[l].astype(BF16),
            rw_pad[l], rb_pad[l], n_lat)
        dest_tiles, plan, n_rows = _moe_plan(gi, rk, counts)
        xs = _dispatch(h2, dest_tiles, jnp.zeros((n_rows,) + h2.shape[1:], F32) if xs is None else xs)
        ys = _experts(xs, plan, w_gu, b_gu[:, :, None, :], w_down, b_down[:, :, None, :], l)
        xx = _combine(x1, gw, mod, ys, dest_tiles, n_lat)
    return xx[:n_lat][None]
```

```python
import functools
import math

import jax
import jax.numpy as jnp
from jax import lax
from jax.experimental import pallas as pl
from jax.experimental.pallas import tpu as pltpu

F32 = jnp.float32
BF16 = jnp.bfloat16
QK_DTYPE = jnp.float8_e4m3fn

D_MODEL = 1024
GRID_W = 64
EPS = 1e-6

DA_HEADS = 4
DA_DH = 64
DA_DV = 2 * DA_DH
DA_W = DA_HEADS * DA_DV
DA_SCALE = DA_DH ** -0.5
ROPE_PAIRS = DA_DH // 4
ROPE_BASE = 10000.0

HG_HEADS = 4
HG_DK = 128
HG_W = HG_HEADS * HG_DK
HG_CHUNK = 128

N_EXPERTS = 32
TOP_K = 4
D_FF = 1024
SWIGLU_ALPHA = 1.702
SWIGLU_LIMIT = 7.0
MOE_BLOCK = 256

IN_COLS = 6144
ROW_TILE = 256
NEG_BIG = -1e30
LOG2E = 1.4426950408889634

VMEM_LIMIT = 56 * 1024 * 1024
ATT_TQ = 256
ATT_KEY_CHUNKS = (3328, 1280, 256)
DMA_QUEUES = 2
SCORE_BOUND_MAX = 40.0
SCORE_BOUND_SLACK = 1.01


def _cparams(sem):
    return pltpu.CompilerParams(dimension_semantics=sem, vmem_limit_bytes=VMEM_LIMIT)


def _split3(x):
    a = x.astype(BF16)
    r = x - a.astype(F32)
    b = r.astype(BF16)
    c = (r - b.astype(F32)).astype(BF16)
    return a, b, c


def _dot(a, b):
    return jnp.dot(a, b, preferred_element_type=F32)


def _dot_nt(a, b):
    return lax.dot_general(a, b, (((1,), (1,)), ((), ())), preferred_element_type=F32)


def _dot_tn(a, b):
    return lax.dot_general(a, b, (((0,), (0,)), ((), ())), preferred_element_type=F32)


def _sigmoid(x):
    return 1.0 / (1.0 + jnp.exp(-x))


def _tile_transpose(x):
    g = x.shape[0]
    sub = lax.broadcasted_iota(jnp.int32, (1, 1, 1, 1, 8, 128), 4)
    x = x.reshape(g, 2, 2, 2, 8, 128)
    for axis, k in ((1, 4), (2, 2), (3, 1)):
        lo = (sub & k) == 0
        a = lax.index_in_dim(x, 0, axis, keepdims=True)
        b = lax.index_in_dim(x, 1, axis, keepdims=True)
        a2 = jnp.where(lo, a, pltpu.roll(b, k, 4))
        b2 = jnp.where(lo, pltpu.roll(a, 8 - k, 4), b)
        x = jnp.concatenate([a2, b2], axis=axis)
    return x.reshape(g, 8, 8, 128)


def _store_row_tiles(ref, val):
    n = val.shape[0]
    z = jnp.stack([val[:, s * 128:(s + 1) * 128].reshape(n // 8, 8, 128) for s in range(8)], axis=1)
    ref[...] = _tile_transpose(z).reshape(n, 8, 128)


def _load_row_tiles(ref, lo, n):
    y = _tile_transpose(ref[lo:lo + n].reshape(n // 8, 8, 8, 128))
    return jnp.concatenate([y[:, s].reshape(n, 128) for s in range(8)], axis=1)


def _mod_kernel(c_ref, w_ref, b_ref, o_ref):
    cv = c_ref[...]
    a = cv * _sigmoid(cv)
    w = w_ref[0]
    a1, a2, a3 = _split3(a)
    w1, w2, w3 = _split3(w)
    acc = _dot(a1, w1) + (_dot(a1, w2) + _dot(a2, w1)) + (_dot(a2, w2) + _dot(a1, w3) + _dot(a3, w1))
    o_ref[0] = acc + b_ref[0]


def _modulation(cc, ada_w, ada_b):
    depth, d, n = ada_w.shape
    tn = 1536
    return pl.pallas_call(
        _mod_kernel,
        grid=(depth, n // tn),
        in_specs=[
            pl.BlockSpec((8, d), lambda l, j: (0, 0)),
            pl.BlockSpec((1, d, tn), lambda l, j: (l, 0, j)),
            pl.BlockSpec((1, 1, tn), lambda l, j: (l, 0, j)),
        ],
        out_specs=pl.BlockSpec((1, 8, tn), lambda l, j: (l, 0, j)),
        out_shape=jax.ShapeDtypeStruct((depth, 8, n), F32),
        compiler_params=_cparams(("parallel", "parallel")),
        name="adaln_mod",
    )(cc, ada_w, ada_b.reshape(depth, 1, n))


def _inproj_kernel(x_ref, g_ref, mod_ref, w_ref, o_ref, h_scr, *, n_lat, tm):
    i = pl.program_id(0)
    j = pl.program_id(1)
    d = x_ref.shape[1]

    @pl.when(j == 0)
    def _():
        x = x_ref[...]
        y = x * lax.rsqrt(jnp.mean(x * x, axis=-1, keepdims=True) + EPS) * g_ref[...]
        row = i * tm + lax.broadcasted_iota(jnp.int32, (tm, 1), 0)
        lat = row < n_lat
        sh = jnp.where(lat, mod_ref[0:1, 0:d], mod_ref[1:2, 0:d])
        sc = jnp.where(lat, mod_ref[0:1, d:2 * d], mod_ref[1:2, d:2 * d])
        h_scr[...] = (y * (1.0 + sc) + sh).astype(BF16)

    o_ref[...] = _dot(h_scr[...], w_ref[...])


def _inproj(x, g, mod, w_bf, n_lat):
    t, d = x.shape
    n = w_bf.shape[1]
    tm = 1280 if t % 1280 == 0 else ROW_TILE
    tn = 1536
    return pl.pallas_call(
        functools.partial(_inproj_kernel, n_lat=n_lat, tm=tm),
        grid=(t // tm, n // tn),
        in_specs=[
            pl.BlockSpec((tm, d), lambda i, j: (i, 0)),
            pl.BlockSpec((1, d), lambda i, j: (0, 0)),
            pl.BlockSpec((8, mod.shape[1]), lambda i, j: (0, 0)),
            pl.BlockSpec((d, tn), lambda i, j: (0, j)),
        ],
        out_specs=pl.BlockSpec((tm, tn), lambda i, j: (i, j)),
        out_shape=jax.ShapeDtypeStruct((t, n), F32),
        scratch_shapes=[pltpu.VMEM((tm, d), BF16)],
        compiler_params=_cparams(("parallel", "arbitrary")),
        name="inproj",
    )(x, g, mod, w_bf)


def _segment_mean_sq(x, seg_ref, width):
    x2 = x * x
    hi = x2.astype(BF16)
    lo = (x2 - hi.astype(F32)).astype(BF16)
    seg = seg_ref[...]
    return (_dot(hi, seg) + _dot(lo, seg)) * (1.0 / width)


def _prep_kernel(q_ref, k_ref, v_ref, cos_ref, sin_ref, seg_ref, qg_ref, kg_ref,
                 k_out, qt_out, vt_out, qn_out, kn_out):
    tm = q_ref.shape[0]
    cos = cos_ref[...]
    sin = sin_ref[...]
    lane = lax.broadcasted_iota(jnp.int32, (tm, DA_W), 1)
    first_half = (lane & 31) < 16

    def norm_rope(x, g):
        y = x * lax.rsqrt(_segment_mean_sq(x, seg_ref, DA_DH) + EPS) * g
        fwd = pltpu.roll(y, DA_W - 16, 1)
        bwd = pltpu.roll(y, 16, 1)
        partner = jnp.where(first_half, fwd, bwd)
        return y * cos + partner * sin

    k_b = norm_rope(k_ref[...], kg_ref[...]).astype(QK_DTYPE)
    k_out[...] = k_b
    k_f = k_b.astype(F32)
    k_n2 = _dot((k_f * k_f).astype(BF16), seg_ref[...])
    kn_out[...] = jnp.broadcast_to(jnp.max(k_n2, axis=0, keepdims=True), kn_out.shape)

    q = norm_rope(q_ref[...], qg_ref[...]) * (DA_SCALE * LOG2E)
    lane_h = lax.broadcasted_iota(jnp.int32, (tm, DA_DV), 1)
    v = v_ref[...]
    for h in range(DA_HEADS):
        qh = q[:, h * DA_DV:(h + 1) * DA_DV]
        for m in range(2):
            keep = (lane_h < DA_DH) if m == 0 else (lane_h >= DA_DH)
            r = 2 * h + m
            qt_b = jnp.where(keep, qh, 0.0).T.astype(QK_DTYPE)
            qt_out[r * DA_DV:(r + 1) * DA_DV, :] = qt_b
            qt_f = qt_b.astype(F32)
            qn_out[r:r + 1, :] = jnp.sum(qt_f * qt_f, axis=0, keepdims=True)
        vt_out[0, h * DA_DV:(h + 1) * DA_DV, :] = v[:, h * DA_DV:(h + 1) * DA_DV].T.astype(BF16)


def _prep(p, cos_t, sin_t, seg64, qg, kg, chunk):
    t = p.shape[0]
    tm = ROW_TILE
    per_chunk = chunk // tm
    return pl.pallas_call(
        _prep_kernel,
        grid=(t // tm,),
        in_specs=[
            pl.BlockSpec((tm, DA_W), lambda i: (i, 0)),
            pl.BlockSpec((tm, DA_W), lambda i: (i, 1)),
            pl.BlockSpec((tm, DA_W), lambda i: (i, 2)),
            pl.BlockSpec((tm, DA_W), lambda i: (i, 0)),
            pl.BlockSpec((tm, DA_W), lambda i: (i, 0)),
            pl.BlockSpec((DA_W, DA_W), lambda i: (0, 0)),
            pl.BlockSpec((1, DA_W), lambda i: (0, 0)),
            pl.BlockSpec((1, DA_W), lambda i: (0, 0)),
        ],
        out_specs=[
            pl.BlockSpec((tm, DA_W), lambda i: (i, 0)),
            pl.BlockSpec((2 * DA_HEADS * DA_DV, tm), lambda i: (0, i)),
            pl.BlockSpec((1, DA_W, tm), lambda i: (i // per_chunk, 0, i % per_chunk)),
            pl.BlockSpec((2 * DA_HEADS, tm), lambda i: (0, i)),
            pl.BlockSpec((8, DA_W), lambda i: (i, 0)),
        ],
        out_shape=[
            jax.ShapeDtypeStruct((t, DA_W), QK_DTYPE),
            jax.ShapeDtypeStruct((2 * DA_HEADS * DA_DV, t), QK_DTYPE),
            jax.ShapeDtypeStruct((t // chunk, DA_W, chunk), BF16),
            jax.ShapeDtypeStruct((2 * DA_HEADS, t), F32),
            jax.ShapeDtypeStruct((8 * (t // tm), DA_W), F32),
        ],
        compiler_params=_cparams(("parallel",)),
        name="qkv_prep",
    )(p, p, p, cos_t, sin_t, seg64, qg, kg)


def _attn_finalize(lam_ref, sg_ref, o_ref, acc_scr, l_scr, lam_init):
    lq = lam_ref[...]
    lam = (jnp.exp(jnp.sum(lq[0:1] * lq[1:2], axis=-1, keepdims=True))
           - jnp.exp(jnp.sum(lq[2:3] * lq[3:4], axis=-1, keepdims=True)) + lam_init)
    for h in range(DA_HEADS):
        r = 2 * h
        o = acc_scr[r] / l_scr[r:r + 1, :] - lam * (acc_scr[r + 1] / l_scr[r + 1:r + 2, :])
        o = o * lax.rsqrt(jnp.mean(o * o, axis=0, keepdims=True) + EPS)
        o = o * sg_ref[...] * (1.0 - lam_init)
        o_ref[:, h * DA_DV:(h + 1) * DA_DV] = o.T.astype(o_ref.dtype)


def _attn_kernel(qt_ref, k_hbm, vt_hbm, lam_ref, sg_ref, o_ref, k_scr, vt_scr, m_scr, l_scr, acc_scr, sems,
                 *, lam_init, stabilised):
    @pl.when(pl.program_id(0) == 0)
    def _():
        copies = (pltpu.make_async_copy(k_hbm, k_scr, sems.at[0]), pltpu.make_async_copy(vt_hbm, vt_scr, sems.at[1]))
        for cp in copies:
            cp.start()
        for cp in copies:
            cp.wait()

    acc_scr[...] = jnp.zeros(acc_scr.shape, F32)
    l_scr[...] = jnp.zeros(l_scr.shape, F32)
    if stabilised:
        m_scr[...] = jnp.full(m_scr.shape, NEG_BIG, F32)

    def chunk(c, carry):
        for h in range(DA_HEADS):
            k_h = k_scr[c, :, h * DA_DV:(h + 1) * DA_DV]
            vt_h = vt_scr[c, h * DA_DV:(h + 1) * DA_DV, :]
            for m in range(2):
                r = 2 * h + m
                s = _dot(k_h, qt_ref[r * DA_DV:(r + 1) * DA_DV, :])
                if stabilised:
                    m_prev = m_scr[r:r + 1, :]
                    m_new = jnp.maximum(m_prev, jnp.max(s, axis=0, keepdims=True))
                    alpha = jnp.exp2(m_prev - m_new)
                    p = jnp.exp2((s - m_new).astype(BF16))
                    acc_scr[r] = alpha * acc_scr[r] + _dot(vt_h, p)
                    l_scr[r:r + 1, :] = alpha * l_scr[r:r + 1, :] + jnp.sum(p.astype(F32), axis=0, keepdims=True)
                    m_scr[r:r + 1, :] = m_new
                else:
                    p = jnp.exp2(s.astype(BF16))
                    acc_scr[r] += _dot(vt_h, p)
                    l_scr[r:r + 1, :] += jnp.sum(p.astype(F32), axis=0, keepdims=True)
        return carry

    lax.fori_loop(0, k_scr.shape[0], chunk, 0)
    _attn_finalize(lam_ref, sg_ref, o_ref, acc_scr, l_scr, lam_init)


def _attention(qt, k3, vt3, lam_qk, subln_col, lam_init, stabilised, *, q_start, n_q, tq):
    qo = q_start // tq
    return pl.pallas_call(
        functools.partial(_attn_kernel, lam_init=lam_init, stabilised=stabilised),
        grid=(n_q // tq,),
        in_specs=[
            pl.BlockSpec((qt.shape[0], tq), lambda i: (0, i + qo)),
            pl.BlockSpec(memory_space=pl.ANY),
            pl.BlockSpec(memory_space=pl.ANY),
            pl.BlockSpec((4, DA_DH), lambda i: (0, 0)),
            pl.BlockSpec((DA_DV, 1), lambda i: (0, 0)),
        ],
        out_specs=pl.BlockSpec((tq, DA_W), lambda i: (i, 0)),
        out_shape=jax.ShapeDtypeStruct((n_q, DA_W), BF16),
        scratch_shapes=[
            pltpu.VMEM(k3.shape, k3.dtype),
            pltpu.VMEM(vt3.shape, vt3.dtype),
            pltpu.VMEM((2 * DA_HEADS, tq), F32),
            pltpu.VMEM((2 * DA_HEADS, tq), F32),
            pltpu.VMEM((2 * DA_HEADS, DA_DV, tq), F32),
            pltpu.SemaphoreType.DMA((2,)),
        ],
        compiler_params=_cparams(("arbitrary",)),
        name="diff_attn_online" if stabilised else "diff_attn_plain",
    )(qt, k3, vt3, lam_qk, subln_col)


def _attention_all(qt, k, vt3, qn2, kn2_tiles, lam_qk, subln_col, lam_init, n_lat, n_ctx):
    n_chunks, _, chunk = vt3.shape
    n_maps = 2 * DA_HEADS
    kmax2 = jnp.max(kn2_tiles, axis=0).reshape(n_maps, DA_DH)[:, 0]
    bound = jnp.sqrt(jnp.max(qn2, axis=1)) * jnp.sqrt(kmax2) * SCORE_BOUND_SLACK
    bounded = jnp.max(bound) <= SCORE_BOUND_MAX
    k3 = k.reshape(n_chunks, chunk, DA_W)

    def run(stabilised):
        def f(args):
            qt_, k3_, vt3_, lam_, sg_ = args
            lat = _attention(qt_, k3_, vt3_, lam_, sg_, lam_init, stabilised, q_start=0, n_q=n_lat, tq=ATT_TQ)
            ctx = _attention(qt_, k3_[n_chunks - 1:, chunk - n_ctx:, :], vt3_[n_chunks - 1:, :, chunk - n_ctx:],
                             lam_, sg_, lam_init, stabilised, q_start=n_lat, n_q=n_ctx, tq=ATT_TQ)
            return jnp.concatenate([lat, ctx], axis=0)
        return f

    return lax.cond(bounded, run(False), run(True), (qt, k3, vt3, lam_qk, subln_col))


def _hg_chunk(q, z, v, lb, st_ref, rev):
    c, w = q.shape
    row = lax.broadcasted_iota(jnp.int32, (c, 1), 0)

    u = jnp.exp(-jnp.abs(z))
    sig_abs = 1.0 / (1.0 + u)
    y = jnp.log(1.0 - lb) + jnp.minimum(z, 0.0) - jnp.log(1.0 + u)
    log_lb = jnp.log(lb)
    log_f = jnp.maximum(log_lb, y) + jnp.log(1.0 + jnp.exp(-jnp.abs(log_lb - y)))
    key = (1.0 - lb) * jnp.where(z >= 0.0, u * sig_abs, sig_abs)

    ri = lax.broadcasted_iota(jnp.int32, (c, c), 0)
    ci = lax.broadcasted_iota(jnp.int32, (c, c), 1)
    tri = jnp.where((ci >= ri) if rev else (ci <= ri), 1.0, 0.0).astype(BF16)
    g1, g2, g3 = _split3(log_f)
    b = _dot(tri, g1) + _dot(tri, g2) + _dot(tri, g3)
    end = 0 if rev else c - 1
    b_end = b[end:end + 1, :]

    q_in = (q * jnp.exp(b)).astype(BF16)
    k_out = (key * jnp.exp(b_end - b)).astype(BF16)
    decay = jnp.exp(b_end)

    up = pltpu.roll(log_f, 1, 0)
    dn = pltpu.roll(log_f, c - 1, 0)
    q_lv, k_lv, masks = [], [], []
    lvl = c // 2
    while lvl >= 1:
        half = (row // lvl) & 1
        q_rows = (half == 0) if rev else (half == 1)
        if lvl >= 4:
            pos = lvl if rev else lvl - 1
            ref = jnp.broadcast_to(b.reshape(c // (2 * lvl), 2 * lvl, w)[:, pos:pos + 1, :],
                                   (c // (2 * lvl), 2 * lvl, w)).reshape(c, w)
            eq, ek = b - ref, ref - b
        elif lvl == 2:
            j = row & 3
            if rev:
                eq = jnp.where(j == 1, log_f, log_f + dn)
                ek = jnp.where(j == 2, 0.0, up)
            else:
                eq = jnp.where(j == 2, log_f, log_f + up)
                ek = jnp.where(j == 1, 0.0, dn)
        else:
            eq, ek = log_f, jnp.zeros_like(log_f)
        q_lv.append((q * jnp.exp(jnp.where(q_rows, eq, NEG_BIG))).astype(BF16))
        k_lv.append((key * jnp.exp(jnp.where(q_rows, NEG_BIG, ek))).astype(BF16))
        masks.append(None if 2 * lvl == c else
                     jnp.where((ri // (2 * lvl)) == (ci // (2 * lvl)), 1.0, 0.0))
        lvl //= 2
    qk = q * key
    on_diag = ri == ci

    outs = []
    for h in range(HG_HEADS):
        sl = slice(h * HG_DK, (h + 1) * HG_DK)
        a = jnp.where(on_diag, jnp.sum(qk[:, sl], axis=-1, keepdims=True), 0.0)
        for q_l, k_l, msk in zip(q_lv, k_lv, masks):
            a_l = _dot_nt(q_l[:, sl], k_l[:, sl])
            a = a + (a_l if msk is None else a_l * msk)
        st = st_ref[h]
        v_h = v[:, sl].astype(BF16)
        outs.append(_dot_nt(q_in[:, sl], st.astype(BF16)) + _dot(a.astype(BF16), v_h))
        st_ref[h] = st * decay[:, sl] + _dot_tn(v_h, k_out[:, sl])
    return jnp.concatenate(outs, axis=1)


def _hgrn_kernel(qf_ref, zf_ref, vf_ref, qb_ref, zb_ref, vb_ref, lbf_ref, lbb_ref,
                 of_ref, ob_ref, sf_scr, sb_scr):
    @pl.when(pl.program_id(0) == 0)
    def _():
        sf_scr[...] = jnp.zeros(sf_scr.shape, F32)
        sb_scr[...] = jnp.zeros(sb_scr.shape, F32)

    n_chunks = qf_ref.shape[0] // HG_CHUNK

    def body(ci, carry):
        rf = pl.ds(pl.multiple_of(ci * HG_CHUNK, HG_CHUNK), HG_CHUNK)
        of_ref[rf, :] = _hg_chunk(qf_ref[rf, :], zf_ref[rf, :], vf_ref[rf, :], lbf_ref[...], sf_scr, False)
        rb = pl.ds(pl.multiple_of((n_chunks - 1 - ci) * HG_CHUNK, HG_CHUNK), HG_CHUNK)
        ob_ref[rb, :] = _hg_chunk(qb_ref[rb, :], zb_ref[rb, :], vb_ref[rb, :], lbb_ref[...], sb_scr, True)
        return carry

    lax.fori_loop(0, n_chunks, body, 0)


def _hgrn(p, lb_f, lb_b, n_lat):
    t = p.shape[0]
    tm = ROW_TILE
    nb = t // tm
    n_lat_b = n_lat // tm
    n_ctx_b = nb - n_lat_b

    def jf(i):
        return jnp.where(i < n_ctx_b, n_lat_b + i, i - n_ctx_b)

    def jb(i):
        return jnp.where(i < n_ctx_b, nb - 1 - i, nb - 1 - i)

    def spec(order, colblk):
        return pl.BlockSpec((tm, HG_W), lambda i: (order(i), colblk))

    return pl.pallas_call(
        _hgrn_kernel,
        grid=(nb,),
        in_specs=[spec(jf, 3), spec(jf, 4), spec(jf, 6), spec(jb, 3), spec(jb, 5), spec(jb, 6),
                  pl.BlockSpec((1, HG_W), lambda i: (0, 0)), pl.BlockSpec((1, HG_W), lambda i: (0, 0))],
        out_specs=[pl.BlockSpec((tm, HG_W), lambda i: (jf(i), 0)),
                   pl.BlockSpec((tm, HG_W), lambda i: (jb(i), 0))],
        out_shape=[jax.ShapeDtypeStruct((t, HG_W), F32), jax.ShapeDtypeStruct((t, HG_W), F32)],
        scratch_shapes=[pltpu.VMEM((HG_HEADS, HG_DK, HG_DK), F32), pltpu.VMEM((HG_HEADS, HG_DK, HG_DK), F32)],
        compiler_params=_cparams(("arbitrary",)),
        name="hgrn2",
    )(p, p, p, p, p, p, lb_f, lb_b)


def _post_kernel(x_ref, oa_ref, of_ref, ob_ref, gh_ref, ga_ref, gb_ref, mod_ref, hg_ref, n2_ref,
                 wa_ref, wb_ref, wo_ref, rw_ref, rb_ref,
                 x1_ref, h2_ref, gw_ref, gi_ref, rk_ref, cnt_ref, run_scr, *, n_lat, tm):
    i = pl.program_id(0)
    d = x_ref.shape[1]

    @pl.when(i == 0)
    def _():
        run_scr[...] = jnp.zeros(run_scr.shape, F32)

    lat = i * tm < n_lat

    def mod(c):
        return jnp.where(lat, mod_ref[0:1, c * d:(c + 1) * d], mod_ref[1:2, c * d:(c + 1) * d])

    o = of_ref[...] + ob_ref[...]
    parts = []
    for h in range(HG_HEADS):
        oh = o[:, h * HG_DK:(h + 1) * HG_DK]
        parts.append(oh * lax.rsqrt(jnp.mean(oh * oh, axis=-1, keepdims=True) + EPS))
    gh = gh_ref[...]
    ob = jnp.concatenate(parts, axis=1) * hg_ref[...] * (gh * _sigmoid(gh))
    ya = _dot(oa_ref[...], wa_ref[...])
    yb = _dot(ob.astype(BF16), wb_ref[...])
    mix = _sigmoid(ga_ref[...]) * ya + _sigmoid(gb_ref[...]) * yb
    x1 = x_ref[...] + mod(2) * _dot(mix.astype(BF16), wo_ref[...])
    x1_ref[...] = x1

    h2 = x1 * lax.rsqrt(jnp.mean(x1 * x1, axis=-1, keepdims=True) + EPS) * n2_ref[...]
    h2 = h2 * (1.0 + mod(4)) + mod(3)
    _store_row_tiles(h2_ref, h2)

    h_hi = h2.astype(BF16)
    h_lo = (h2 - h_hi.astype(F32)).astype(BF16)
    rw = rw_ref[...]
    r_hi = rw.astype(BF16)
    r_lo = (rw - r_hi.astype(F32)).astype(BF16)
    logits = _dot(h_hi, r_hi) + (_dot(h_hi, r_lo) + _dot(h_lo, r_hi)) + rb_ref[...]

    lane = lax.broadcasted_iota(jnp.int32, logits.shape, 1)
    work = jnp.where(lane < N_EXPERTS, logits, -jnp.inf)
    sel_any = jnp.zeros(logits.shape, F32)
    vals, idxs = [], []
    for _ in range(TOP_K):
        mx = jnp.max(work, axis=-1, keepdims=True)
        idx = jnp.min(jnp.where(work == mx, lane, 2 * N_EXPERTS), axis=-1, keepdims=True)
        hit = lane == idx
        vals.append(mx)
        idxs.append(idx)
        sel_any = jnp.where(hit, 1.0, sel_any)
        work = jnp.where(hit, -jnp.inf, work)
    es = [jnp.exp(vk - vals[0]) for vk in vals]
    denom = es[0] + es[1] + es[2] + es[3]

    ri = lax.broadcasted_iota(jnp.int32, (tm, tm), 0)
    ci = lax.broadcasted_iota(jnp.int32, (tm, tm), 1)
    below = jnp.where(ci < ri, 1.0, 0.0).astype(BF16)
    rank_e = run_scr[0:1, :] + _dot(below, sel_any.astype(BF16))
    gw = jnp.zeros(logits.shape, F32)
    gi = jnp.zeros(logits.shape, jnp.int32)
    rk = jnp.zeros(logits.shape, jnp.int32)
    for k in range(TOP_K):
        at_k = lane == k
        gw = jnp.where(at_k, es[k] / denom, gw)
        gi = jnp.where(at_k, idxs[k], gi)
        rnk = jnp.sum(jnp.where(lane == idxs[k], rank_e, 0.0), axis=-1, keepdims=True)
        rk = jnp.where(at_k, rnk.astype(jnp.int32), rk)
    gw_ref[...] = gw
    gi_ref[...] = gi
    rk_ref[...] = rk
    run = run_scr[0:1, :] + jnp.sum(sel_any, axis=0, keepdims=True)
    run_scr[...] = jnp.broadcast_to(run, run_scr.shape)
    cnt_ref[...] = jnp.broadcast_to(run, cnt_ref.shape)


def _post(x, oa, o_f, o_b, p, mod, hg_g, n2_g, wa, wb, wo, rw_pad, rb_pad, n_lat):
    t, d = x.shape
    tm = ROW_TILE

    def rows(width, colblk=0):
        return pl.BlockSpec((tm, width), lambda i: (i, colblk))

    def whole(a):
        return pl.BlockSpec(a.shape, lambda i: (0,) * a.ndim)

    return pl.pallas_call(
        functools.partial(_post_kernel, n_lat=n_lat, tm=tm),
        grid=(t // tm,),
        in_specs=[rows(d), rows(DA_W), rows(HG_W), rows(HG_W),
                  rows(HG_W, 7), rows(d, 4), rows(d, 5),
                  whole(mod), whole(hg_g), whole(n2_g), whole(wa), whole(wb), whole(wo),
                  whole(rw_pad), whole(rb_pad)],
        out_specs=[rows(d), pl.BlockSpec((tm, d // 128, 128), lambda i: (i, 0, 0)), rows(128), rows(128), rows(128),
                   pl.BlockSpec((8, 128), lambda i: (0, 0))],
        out_shape=[jax.ShapeDtypeStruct((t, d), F32), jax.ShapeDtypeStruct((t, d // 128, 128), F32),
                   jax.ShapeDtypeStruct((t, 128), F32), jax.ShapeDtypeStruct((t, 128), jnp.int32),
                   jax.ShapeDtypeStruct((t, 128), jnp.int32), jax.ShapeDtypeStruct((8, 128), F32)],
        scratch_shapes=[pltpu.VMEM((8, 128), F32)],
        compiler_params=_cparams(("arbitrary",)),
        name="merge_router",
    )(x, oa, o_f, o_b, p, p, p, mod, hg_g, n2_g, wa, wb, wo, rw_pad, rb_pad)


def _dispatch_kernel(dest_ref, h_ref, xs_in, xs_out, sem):
    del xs_in
    n = dest_ref.shape[0]
    tm = h_ref.shape[0]

    def start(j, c):
        for q in range(DMA_QUEUES):
            r = DMA_QUEUES * j + q
            pltpu.make_async_copy(h_ref.at[r & (tm - 1)], xs_out.at[dest_ref[r]], sem).start(priority=q)
        return c

    lax.fori_loop(0, n // DMA_QUEUES, start, 0, unroll=4)
    pltpu.make_async_copy(xs_out.at[pl.ds(0, n)], xs_out.at[pl.ds(0, n)], sem).wait()


def _dispatch(h2, dest_tiles, xs_zero):
    t = h2.shape[0]
    tm = ROW_TILE
    return pl.pallas_call(
        _dispatch_kernel,
        grid=(t // tm,),
        in_specs=[pl.BlockSpec((tm * TOP_K,), lambda i: (i,), memory_space=pltpu.SMEM),
                  pl.BlockSpec((tm,) + h2.shape[1:], lambda i: (i, 0, 0)),
                  pl.BlockSpec(memory_space=pl.ANY)],
        out_specs=pl.BlockSpec(memory_space=pl.ANY),
        out_shape=jax.ShapeDtypeStruct(xs_zero.shape, xs_zero.dtype),
        scratch_shapes=[pltpu.SemaphoreType.DMA(())],
        input_output_aliases={2: 0},
        compiler_params=_cparams(("arbitrary",)),
        name="moe_dispatch",
    )(dest_tiles, h2, xs_zero)


def _expert_kernel(be_ref, nx_ref, sl_ref, nu_ref, xs_ref, wgu_hbm, bgu_ref, wd_hbm, bd_ref, ys_ref,
                   wgu_f32, wd_f32, wgu_bf, wd_bf, sems, *, layer):
    i = pl.program_id(0)
    used = i < nu_ref[0]

    def weight_copies(e, slot):
        return (pltpu.make_async_copy(wgu_hbm.at[layer, e], wgu_f32.at[slot], sems.at[0, slot]),
                pltpu.make_async_copy(wd_hbm.at[layer, e], wd_f32.at[slot], sems.at[1, slot]))

    @pl.when(i == 0)
    def _():
        for cp in weight_copies(be_ref[0], 0):
            cp.start()

    @pl.when(jnp.logical_and(used, jnp.logical_or(i == 0, be_ref[i] != be_ref[jnp.maximum(i - 1, 0)])))
    def _():
        slot = sl_ref[i]
        for cp in weight_copies(be_ref[i], slot):
            cp.wait()

        @pl.when(nx_ref[i] >= 0)
        def _():
            for cp in weight_copies(nx_ref[i], 1 - slot):
                cp.start()

        wgu_bf[...] = wgu_f32[slot].astype(BF16)
        wd_bf[...] = wd_f32[slot].astype(BF16)

    @pl.when(used)
    def _():
        x = _load_row_tiles(xs_ref, 0, xs_ref.shape[0]).astype(BF16)
        gu = _dot(x, wgu_bf[...]) + bgu_ref[0, 0]
        gate = jnp.minimum(gu[:, :D_FF], SWIGLU_LIMIT)
        up = jnp.clip(gu[:, D_FF:], -SWIGLU_LIMIT, SWIGLU_LIMIT)
        hdn = (up + 1.0) * gate * _sigmoid(SWIGLU_ALPHA * gate)
        _store_row_tiles(ys_ref, _dot(hdn.astype(BF16), wd_bf[...]) + bd_ref[0, 0])

    @pl.when(jnp.logical_not(used))
    def _():
        ys_ref[...] = jnp.zeros(ys_ref.shape, F32)


def _experts(xs, plan, wgu, bgu, wd, bd, layer):
    blk_e, next_e, slot, n_used = plan
    n_rows = xs.shape[0]
    d = xs.shape[1] * xs.shape[2]
    nblk = n_rows // MOE_BLOCK

    def blk(i, nu):
        return jnp.minimum(i, nu[0] - 1)

    def bias(i, be, nx, sl, nu):
        return (layer, be[blk(i, nu)], 0, 0)

    grid_spec = pltpu.PrefetchScalarGridSpec(
        num_scalar_prefetch=4,
        grid=(nblk,),
        in_specs=[
            pl.BlockSpec((MOE_BLOCK,) + xs.shape[1:], lambda i, be, nx, sl, nu: (blk(i, nu), 0, 0)),
            pl.BlockSpec(memory_space=pl.ANY),
            pl.BlockSpec((1, 1, 1, 2 * D_FF), bias),
            pl.BlockSpec(memory_space=pl.ANY),
            pl.BlockSpec((1, 1, 1, d), bias),
        ],
        out_specs=pl.BlockSpec((MOE_BLOCK,) + xs.shape[1:], lambda i, be, nx, sl, nu: (i, 0, 0)),
        scratch_shapes=[pltpu.VMEM((2, d, 2 * D_FF), F32), pltpu.VMEM((2, D_FF, d), F32),
                        pltpu.VMEM((d, 2 * D_FF), BF16), pltpu.VMEM((D_FF, d), BF16),
                        pltpu.SemaphoreType.DMA((2, 2))],
    )
    return pl.pallas_call(
        functools.partial(_expert_kernel, layer=layer),
        grid_spec=grid_spec,
        out_shape=jax.ShapeDtypeStruct(xs.shape, F32),
        compiler_params=_cparams(("arbitrary",)),
        name="moe_experts",
    )(blk_e, next_e, slot, n_used, xs, wgu, bgu, wd, bd)


def _combine_kernel(dest_ref, x1_ref, gw_ref, mod_ref, ys_ref, o_ref, buf, sem, *, n_lat, tm):
    i = pl.program_id(0)
    d = x1_ref.shape[1]
    n = dest_ref.shape[0]

    def start(j, c):
        for q in range(DMA_QUEUES):
            r = DMA_QUEUES * j + q
            pltpu.make_async_copy(ys_ref.at[dest_ref[r]], buf.at[r], sem).start(priority=q)
        return c

    lax.fori_loop(0, n // DMA_QUEUES, start, 0, unroll=4)
    pltpu.make_async_copy(ys_ref.at[pl.ds(0, n)], buf, sem).wait()

    lat = i * tm < n_lat
    g2 = jnp.where(lat, mod_ref[0:1, 5 * d:6 * d], mod_ref[1:2, 5 * d:6 * d])
    gw = gw_ref[...]
    m = gw[:, 0:1] * _load_row_tiles(buf, 0, tm)
    for k in range(1, TOP_K):
        m = m + gw[:, k:k + 1] * _load_row_tiles(buf, k * tm, tm)
    o_ref[...] = x1_ref[...] + g2 * m


def _combine(x1, gw, mod, ys, dest_tiles, n_lat):
    t, d = x1.shape
    tm = ROW_TILE
    return pl.pallas_call(
        functools.partial(_combine_kernel, n_lat=n_lat, tm=tm),
        grid=(t // tm,),
        in_specs=[pl.BlockSpec((tm * TOP_K,), lambda i: (i,), memory_space=pltpu.SMEM),
                  pl.BlockSpec((tm, d), lambda i: (i, 0)),
                  pl.BlockSpec((tm, 128), lambda i: (i, 0)),
                  pl.BlockSpec(mod.shape, lambda i: (0, 0)),
                  pl.BlockSpec(memory_space=pl.ANY)],
        out_specs=pl.BlockSpec((tm, d), lambda i: (i, 0)),
        out_shape=jax.ShapeDtypeStruct((t, d), F32),
        scratch_shapes=[pltpu.VMEM((tm * TOP_K,) + ys.shape[1:], F32), pltpu.SemaphoreType.DMA(())],
        compiler_params=_cparams(("arbitrary",)),
        name="moe_combine",
    )(dest_tiles, x1, gw, mod, ys)


def _moe_plan(gi, rk, counts):
    t = gi.shape[0]
    top_i = gi[:, :TOP_K]
    rank = rk[:, :TOP_K]
    cnt = counts[0, :N_EXPERTS].astype(jnp.int32)
    padded = (cnt + MOE_BLOCK - 1) // MOE_BLOCK * MOE_BLOCK
    pad_ends = jnp.cumsum(padded)
    pad_starts = pad_ends - padded
    dest = pad_starts[top_i] + rank
    n_rows = t * TOP_K + N_EXPERTS * MOE_BLOCK
    blk_start = jnp.arange(n_rows // MOE_BLOCK, dtype=jnp.int32) * MOE_BLOCK
    blk_e = jnp.minimum(jnp.sum(blk_start[:, None] >= pad_ends[None, :], axis=1), N_EXPERTS - 1).astype(jnp.int32)
    n_used = (pad_ends[-1:] // MOE_BLOCK).astype(jnp.int32)
    has_rows = cnt > 0
    later = jnp.where(has_rows[None, :] & (jnp.arange(N_EXPERTS)[None, :] > jnp.arange(N_EXPERTS)[:, None]),
                      jnp.arange(N_EXPERTS)[None, :], N_EXPERTS)
    next_of = jnp.min(later, axis=1)
    next_of = jnp.where(next_of < N_EXPERTS, next_of, -1).astype(jnp.int32)
    run_of = (jnp.cumsum(has_rows.astype(jnp.int32)) - 1).astype(jnp.int32)
    next_e = next_of[blk_e]
    slot = run_of[blk_e] & 1
    dest_tiles = dest.reshape(t // ROW_TILE, ROW_TILE, TOP_K).transpose(0, 2, 1).reshape(-1).astype(jnp.int32)
    return dest_tiles, (blk_e, next_e, slot, n_used), n_rows


def _rope_tables(n_lat, n_ctx):
    pos = jnp.arange(n_lat)
    row = (pos // GRID_W).astype(F32)
    col = (pos % GRID_W).astype(F32)
    freq = ROPE_BASE ** (-jnp.arange(ROPE_PAIRS, dtype=F32) / ROPE_PAIRS)
    ra = row[:, None] * freq
    ca = col[:, None] * freq
    cos64 = jnp.concatenate([jnp.cos(ra), jnp.cos(ra), jnp.cos(ca), jnp.cos(ca)], axis=1)
    sin64 = jnp.concatenate([-jnp.sin(ra), jnp.sin(ra), -jnp.sin(ca), jnp.sin(ca)], axis=1)
    cos_t = jnp.concatenate([jnp.tile(cos64, (1, DA_W // DA_DH)), jnp.ones((n_ctx, DA_W), F32)], axis=0)
    sin_t = jnp.concatenate([jnp.tile(sin64, (1, DA_W // DA_DH)), jnp.zeros((n_ctx, DA_W), F32)], axis=0)
    return cos_t, sin_t


def kernel(x, c, ctx, c_ctx, ada_w, ada_b, norm1_g, norm2_g, w_in, qn_g, kn_g, lam_qk, subln_g, hg_lb,
           hg_norm_g, w_branch_a, w_branch_b, w_out, router_w, router_b, w_gu, b_gu, w_down, b_down):
    bsz, n_lat, d = x.shape
    n_ctx = ctx.shape[1]
    depth = ada_w.shape[0]
    assert bsz == 1 and d == D_MODEL and n_lat % 512 == 0 and n_ctx % ROW_TILE == 0
    t = n_lat + n_ctx

    xx = jnp.concatenate([x[0], ctx[0]], axis=0)
    cc = jnp.zeros((8, d), F32).at[0].set(c[0]).at[1].set(c_ctx)
    mods = _modulation(cc, ada_w, ada_b)

    cos_t, sin_t = _rope_tables(n_lat, n_ctx)
    lane = jnp.arange(DA_W)
    seg64 = (lane[:, None] // DA_DH == lane[None, :] // DA_DH).astype(BF16)
    cs = jnp.cumsum(jax.nn.softmax(hg_lb.astype(F32), axis=1), axis=1)
    lb_all = cs - cs[:, :1]
    rw_pad = jnp.zeros((depth, d, 128), F32).at[:, :, :N_EXPERTS].set(router_w)
    rb_pad = jnp.zeros((depth, 1, 128), F32).at[:, 0, :N_EXPERTS].set(router_b)

    key_chunk = next(c for c in ATT_KEY_CHUNKS if t % c == 0 and n_ctx <= c)
    xs = None
    for l in range(depth):
        lam_init = 0.8 - 0.6 * math.exp(-0.3 * l)
        mod = mods[l]
        p = _inproj(xx, norm1_g[l][None], mod, w_in[l].astype(BF16), n_lat)
        k_r, q_t, v_t3, qn2, kn2_tiles = _prep(p, cos_t, sin_t, seg64, jnp.tile(qn_g[l], DA_W // DA_DH)[None],
                                               jnp.tile(kn_g[l], DA_W // DA_DH)[None], key_chunk)
        oa = _attention_all(q_t, k_r, v_t3, qn2, kn2_tiles, lam_qk[l], subln_g[l][:, None], lam_init,
                            n_lat, n_ctx)
        o_f, o_b = _hgrn(p, lb_all[0, l][None], lb_all[1, l][None], n_lat)
        x1, h2, gw, gi, rk, counts = _post(
            xx, oa, o_f, o_b, p, mod, jnp.tile(hg_norm_g[l], HG_HEADS)[None], norm2_g[l][None],
            w_branch_a[l].astype(BF16), w_branch_b[l].astype(BF16), w_out[l].astype(BF16),
            rw_pad[l], rb_pad[l], n_lat)
        dest_tiles, plan, n_rows = _moe_plan(gi, rk, counts)
        xs = _dispatch(h2, dest_tiles, jnp.zeros((n_rows,) + h2.shape[1:], F32) if xs is None else xs)
        ys = _experts(xs, plan, w_gu, b_gu[:, :, None, :], w_down, b_down[:, :, None, :], l)
        xx = _combine(x1, gw, mod, ys, dest_tiles, n_lat)
    return xx[:n_lat][None]
```

```python
import functools
import math

import jax
import jax.numpy as jnp
from jax import lax
from jax.experimental import pallas as pl
from jax.experimental.pallas import tpu as pltpu

F32 = jnp.float32
BF16 = jnp.bfloat16
QK_DTYPE = jnp.float8_e4m3fn

D_MODEL = 1024
GRID_W = 64
EPS = 1e-6

DA_HEADS = 4
DA_DH = 64
DA_DV = 2 * DA_DH
DA_W = DA_HEADS * DA_DV
DA_SCALE = DA_DH ** -0.5
ROPE_PAIRS = DA_DH // 4
ROPE_BASE = 10000.0

HG_HEADS = 4
HG_DK = 128
HG_W = HG_HEADS * HG_DK
HG_CHUNK = 128

N_EXPERTS = 32
TOP_K = 4
D_FF = 1024
SWIGLU_ALPHA = 1.702
SWIGLU_LIMIT = 7.0
MOE_BLOCK = 256

IN_COLS = 6144
ROW_TILE = 256
NEG_BIG = -1e30
LOG2E = 1.4426950408889634

VMEM_LIMIT = 56 * 1024 * 1024
ATT_TQ = 256
ATT_KEY_CHUNKS = (3328, 1280, 256)
DMA_QUEUES = 2
SCORE_BOUND_MAX = 40.0
SCORE_BOUND_SLACK = 1.01


def _cparams(sem):
    return pltpu.CompilerParams(dimension_semantics=sem, vmem_limit_bytes=VMEM_LIMIT)


def _split3(x):
    a = x.astype(BF16)
    r = x - a.astype(F32)
    b = r.astype(BF16)
    c = (r - b.astype(F32)).astype(BF16)
    return a, b, c


def _dot(a, b):
    return jnp.dot(a, b, preferred_element_type=F32)


def _dot_nt(a, b):
    return lax.dot_general(a, b, (((1,), (1,)), ((), ())), preferred_element_type=F32)


def _dot_tn(a, b):
    return lax.dot_general(a, b, (((0,), (0,)), ((), ())), preferred_element_type=F32)


def _sigmoid(x):
    return 0.5 * jnp.tanh(0.5 * x) + 0.5


def _tile_transpose(x):
    g = x.shape[0]
    sub = lax.broadcasted_iota(jnp.int32, (1, 1, 1, 1, 8, 128), 4)
    x = x.reshape(g, 2, 2, 2, 8, 128)
    for axis, k in ((1, 4), (2, 2), (3, 1)):
        lo = (sub & k) == 0
        a = lax.index_in_dim(x, 0, axis, keepdims=True)
        b = lax.index_in_dim(x, 1, axis, keepdims=True)
        a2 = jnp.where(lo, a, pltpu.roll(b, k, 4))
        b2 = jnp.where(lo, pltpu.roll(a, 8 - k, 4), b)
        x = jnp.concatenate([a2, b2], axis=axis)
    return x.reshape(g, 8, 8, 128)


def _store_row_tiles(ref, val):
    n = val.shape[0]
    z = jnp.stack([val[:, s * 128:(s + 1) * 128].reshape(n // 8, 8, 128) for s in range(8)], axis=1)
    ref[...] = _tile_transpose(z).reshape(n, 8, 128)


def _load_row_tiles(ref, lo, n):
    y = _tile_transpose(ref[lo:lo + n].reshape(n // 8, 8, 8, 128))
    return jnp.concatenate([y[:, s].reshape(n, 128) for s in range(8)], axis=1)


def _mod_kernel(c_ref, w_ref, b_ref, o_ref):
    cv = c_ref[...]
    a = cv * _sigmoid(cv)
    w = w_ref[0]
    a1, a2, a3 = _split3(a)
    w1, w2, w3 = _split3(w)
    acc = _dot(a1, w1) + (_dot(a1, w2) + _dot(a2, w1)) + (_dot(a2, w2) + _dot(a1, w3) + _dot(a3, w1))
    o_ref[0] = acc + b_ref[0]


def _modulation(cc, ada_w, ada_b):
    depth, d, n = ada_w.shape
    tn = 1536
    return pl.pallas_call(
        _mod_kernel,
        grid=(depth, n // tn),
        in_specs=[
            pl.BlockSpec((8, d), lambda l, j: (0, 0)),
            pl.BlockSpec((1, d, tn), lambda l, j: (l, 0, j)),
            pl.BlockSpec((1, 1, tn), lambda l, j: (l, 0, j)),
        ],
        out_specs=pl.BlockSpec((1, 8, tn), lambda l, j: (l, 0, j)),
        out_shape=jax.ShapeDtypeStruct((depth, 8, n), F32),
        compiler_params=_cparams(("parallel", "parallel")),
        name="adaln_mod",
    )(cc, ada_w, ada_b.reshape(depth, 1, n))


def _inproj_kernel(x_ref, g_ref, mod_ref, w_ref, o_ref, h_scr, *, n_lat, tm):
    i = pl.program_id(0)
    j = pl.program_id(1)
    d = x_ref.shape[1]

    @pl.when(j == 0)
    def _():
        x = x_ref[...]
        y = x * lax.rsqrt(jnp.mean(x * x, axis=-1, keepdims=True) + EPS) * g_ref[...]
        row = i * tm + lax.broadcasted_iota(jnp.int32, (tm, 1), 0)
        lat = row < n_lat
        sh = jnp.where(lat, mod_ref[0:1, 0:d], mod_ref[1:2, 0:d])
        sc = jnp.where(lat, mod_ref[0:1, d:2 * d], mod_ref[1:2, d:2 * d])
        h_scr[...] = (y * (1.0 + sc) + sh).astype(BF16)

    o_ref[...] = _dot(h_scr[...], w_ref[...])


def _inproj(x, g, mod, w_bf, n_lat):
    t, d = x.shape
    n = w_bf.shape[1]
    tm = 1280 if t % 1280 == 0 else ROW_TILE
    tn = 1536
    return pl.pallas_call(
        functools.partial(_inproj_kernel, n_lat=n_lat, tm=tm),
        grid=(t // tm, n // tn),
        in_specs=[
            pl.BlockSpec((tm, d), lambda i, j: (i, 0)),
            pl.BlockSpec((1, d), lambda i, j: (0, 0)),
            pl.BlockSpec((8, mod.shape[1]), lambda i, j: (0, 0)),
            pl.BlockSpec((d, tn), lambda i, j: (0, j)),
        ],
        out_specs=pl.BlockSpec((tm, tn), lambda i, j: (i, j)),
        out_shape=jax.ShapeDtypeStruct((t, n), F32),
        scratch_shapes=[pltpu.VMEM((tm, d), BF16)],
        compiler_params=_cparams(("parallel", "arbitrary")),
        name="inproj",
    )(x, g, mod, w_bf)


def _segment_mean_sq(x, seg_ref, width):
    x2 = x * x
    hi = x2.astype(BF16)
    lo = (x2 - hi.astype(F32)).astype(BF16)
    seg = seg_ref[...]
    return (_dot(hi, seg) + _dot(lo, seg)) * (1.0 / width)


def _prep_kernel(q_ref, k_ref, v_ref, cos_ref, sin_ref, seg_ref, qg_ref, kg_ref,
                 k_out, qt_out, vt_out, qn_out, kn_out):
    tm = q_ref.shape[0]
    cos = cos_ref[...]
    sin = sin_ref[...]
    lane = lax.broadcasted_iota(jnp.int32, (tm, DA_W), 1)
    first_half = (lane & 31) < 16

    def norm_rope(x, g):
        y = x * lax.rsqrt(_segment_mean_sq(x, seg_ref, DA_DH) + EPS) * g
        fwd = pltpu.roll(y, DA_W - 16, 1)
        bwd = pltpu.roll(y, 16, 1)
        partner = jnp.where(first_half, fwd, bwd)
        return y * cos + partner * sin

    k_b = norm_rope(k_ref[...], kg_ref[...]).astype(QK_DTYPE)
    k_out[...] = k_b
    k_f = k_b.astype(F32)
    k_n2 = _dot((k_f * k_f).astype(BF16), seg_ref[...])
    kn_out[...] = jnp.broadcast_to(jnp.max(k_n2, axis=0, keepdims=True), kn_out.shape)

    q = norm_rope(q_ref[...], qg_ref[...]) * (DA_SCALE * LOG2E)
    lane_h = lax.broadcasted_iota(jnp.int32, (tm, DA_DV), 1)
    v = v_ref[...]
    for h in range(DA_HEADS):
        qh = q[:, h * DA_DV:(h + 1) * DA_DV]
        for m in range(2):
            keep = (lane_h < DA_DH) if m == 0 else (lane_h >= DA_DH)
            r = 2 * h + m
            qt_b = jnp.where(keep, qh, 0.0).T.astype(QK_DTYPE)
            qt_out[r * DA_DV:(r + 1) * DA_DV, :] = qt_b
            qt_f = qt_b.astype(F32)
            qn_out[r:r + 1, :] = jnp.sum(qt_f * qt_f, axis=0, keepdims=True)
        vt_out[0, h * DA_DV:(h + 1) * DA_DV, :] = v[:, h * DA_DV:(h + 1) * DA_DV].T.astype(BF16)


def _prep(p, cos_t, sin_t, seg64, qg, kg, chunk):
    t = p.shape[0]
    tm = ROW_TILE
    per_chunk = chunk // tm
    return pl.pallas_call(
        _prep_kernel,
        grid=(t // tm,),
        in_specs=[
            pl.BlockSpec((tm, DA_W), lambda i: (i, 0)),
            pl.BlockSpec((tm, DA_W), lambda i: (i, 1)),
            pl.BlockSpec((tm, DA_W), lambda i: (i, 2)),
            pl.BlockSpec((tm, DA_W), lambda i: (i, 0)),
            pl.BlockSpec((tm, DA_W), lambda i: (i, 0)),
            pl.BlockSpec((DA_W, DA_W), lambda i: (0, 0)),
            pl.BlockSpec((1, DA_W), lambda i: (0, 0)),
            pl.BlockSpec((1, DA_W), lambda i: (0, 0)),
        ],
        out_specs=[
            pl.BlockSpec((tm, DA_W), lambda i: (i, 0)),
            pl.BlockSpec((2 * DA_HEADS * DA_DV, tm), lambda i: (0, i)),
            pl.BlockSpec((1, DA_W, tm), lambda i: (i // per_chunk, 0, i % per_chunk)),
            pl.BlockSpec((2 * DA_HEADS, tm), lambda i: (0, i)),
            pl.BlockSpec((8, DA_W), lambda i: (i, 0)),
        ],
        out_shape=[
            jax.ShapeDtypeStruct((t, DA_W), QK_DTYPE),
            jax.ShapeDtypeStruct((2 * DA_HEADS * DA_DV, t), QK_DTYPE),
            jax.ShapeDtypeStruct((t // chunk, DA_W, chunk), BF16),
            jax.ShapeDtypeStruct((2 * DA_HEADS, t), F32),
            jax.ShapeDtypeStruct((8 * (t // tm), DA_W), F32),
        ],
        compiler_params=_cparams(("parallel",)),
        name="qkv_prep",
    )(p, p, p, cos_t, sin_t, seg64, qg, kg)


def _attn_finalize(lam_ref, sg_ref, o_ref, acc_scr, l_scr, lam_init):
    lq = lam_ref[...]
    lam = (jnp.exp(jnp.sum(lq[0:1] * lq[1:2], axis=-1, keepdims=True))
           - jnp.exp(jnp.sum(lq[2:3] * lq[3:4], axis=-1, keepdims=True)) + lam_init)
    for h in range(DA_HEADS):
        r = 2 * h
        o = acc_scr[r] / l_scr[r:r + 1, :] - lam * (acc_scr[r + 1] / l_scr[r + 1:r + 2, :])
        o = o * lax.rsqrt(jnp.mean(o * o, axis=0, keepdims=True) + EPS)
        o = o * sg_ref[...] * (1.0 - lam_init)
        o_ref[:, h * DA_DV:(h + 1) * DA_DV] = o.T.astype(o_ref.dtype)


def _attn_kernel(qt_ref, k_hbm, vt_hbm, lam_ref, sg_ref, o_ref, k_scr, vt_scr, m_scr, l_scr, acc_scr, sems,
                 *, lam_init, stabilised):
    @pl.when(pl.program_id(0) == 0)
    def _():
        copies = (pltpu.make_async_copy(k_hbm, k_scr, sems.at[0]), pltpu.make_async_copy(vt_hbm, vt_scr, sems.at[1]))
        for cp in copies:
            cp.start()
        for cp in copies:
            cp.wait()

    acc_scr[...] = jnp.zeros(acc_scr.shape, F32)
    l_scr[...] = jnp.zeros(l_scr.shape, F32)
    if stabilised:
        m_scr[...] = jnp.full(m_scr.shape, NEG_BIG, F32)

    def chunk(c, carry):
        for h in range(DA_HEADS):
            k_h = k_scr[c, :, h * DA_DV:(h + 1) * DA_DV]
            vt_h = vt_scr[c, h * DA_DV:(h + 1) * DA_DV, :]
            for m in range(2):
                r = 2 * h + m
                s = _dot(k_h, qt_ref[r * DA_DV:(r + 1) * DA_DV, :])
                if stabilised:
                    m_prev = m_scr[r:r + 1, :]
                    m_new = jnp.maximum(m_prev, jnp.max(s, axis=0, keepdims=True))
                    alpha = jnp.exp2(m_prev - m_new)
                    p = jnp.exp2((s - m_new).astype(BF16))
                    acc_scr[r] = alpha * acc_scr[r] + _dot(vt_h, p)
                    l_scr[r:r + 1, :] = alpha * l_scr[r:r + 1, :] + jnp.sum(p.astype(F32), axis=0, keepdims=True)
                    m_scr[r:r + 1, :] = m_new
                else:
                    p = jnp.exp2(s.astype(BF16))
                    acc_scr[r] += _dot(vt_h, p)
                    l_scr[r:r + 1, :] += jnp.sum(p.astype(F32), axis=0, keepdims=True)
        return carry

    lax.fori_loop(0, k_scr.shape[0], chunk, 0)
    _attn_finalize(lam_ref, sg_ref, o_ref, acc_scr, l_scr, lam_init)


def _attention(qt, k3, vt3, lam_qk, subln_col, lam_init, stabilised, *, q_start, n_q, tq):
    qo = q_start // tq
    return pl.pallas_call(
        functools.partial(_attn_kernel, lam_init=lam_init, stabilised=stabilised),
        grid=(n_q // tq,),
        in_specs=[
            pl.BlockSpec((qt.shape[0], tq), lambda i: (0, i + qo)),
            pl.BlockSpec(memory_space=pl.ANY),
            pl.BlockSpec(memory_space=pl.ANY),
            pl.BlockSpec((4, DA_DH), lambda i: (0, 0)),
            pl.BlockSpec((DA_DV, 1), lambda i: (0, 0)),
        ],
        out_specs=pl.BlockSpec((tq, DA_W), lambda i: (i, 0)),
        out_shape=jax.ShapeDtypeStruct((n_q, DA_W), BF16),
        scratch_shapes=[
            pltpu.VMEM(k3.shape, k3.dtype),
            pltpu.VMEM(vt3.shape, vt3.dtype),
            pltpu.VMEM((2 * DA_HEADS, tq), F32),
            pltpu.VMEM((2 * DA_HEADS, tq), F32),
            pltpu.VMEM((2 * DA_HEADS, DA_DV, tq), F32),
            pltpu.SemaphoreType.DMA((2,)),
        ],
        compiler_params=_cparams(("arbitrary",)),
        name="diff_attn_online" if stabilised else "diff_attn_plain",
    )(qt, k3, vt3, lam_qk, subln_col)


def _attention_all(qt, k, vt3, qn2, kn2_tiles, lam_qk, subln_col, lam_init, n_lat, n_ctx):
    n_chunks, _, chunk = vt3.shape
    n_maps = 2 * DA_HEADS
    kmax2 = jnp.max(kn2_tiles, axis=0).reshape(n_maps, DA_DH)[:, 0]
    bound = jnp.sqrt(jnp.max(qn2, axis=1)) * jnp.sqrt(kmax2) * SCORE_BOUND_SLACK
    bounded = jnp.max(bound) <= SCORE_BOUND_MAX
    k3 = k.reshape(n_chunks, chunk, DA_W)

    def run(stabilised):
        def f(args):
            qt_, k3_, vt3_, lam_, sg_ = args
            lat = _attention(qt_, k3_, vt3_, lam_, sg_, lam_init, stabilised, q_start=0, n_q=n_lat, tq=ATT_TQ)
            ctx = _attention(qt_, k3_[n_chunks - 1:, chunk - n_ctx:, :], vt3_[n_chunks - 1:, :, chunk - n_ctx:],
                             lam_, sg_, lam_init, stabilised, q_start=n_lat, n_q=n_ctx, tq=ATT_TQ)
            return jnp.concatenate([lat, ctx], axis=0)
        return f

    return lax.cond(bounded, run(False), run(True), (qt, k3, vt3, lam_qk, subln_col))


def _hg_chunk(q, z, v, lb, st_ref, rev):
    c, w = q.shape
    row = lax.broadcasted_iota(jnp.int32, (c, 1), 0)

    u = jnp.exp(-jnp.abs(z))
    sig_abs = 1.0 / (1.0 + u)
    y = jnp.log(1.0 - lb) + jnp.minimum(z, 0.0) - jnp.log(1.0 + u)
    log_lb = jnp.log(lb)
    log_f = jnp.maximum(log_lb, y) + jnp.log(1.0 + jnp.exp(-jnp.abs(log_lb - y)))
    key = (1.0 - lb) * jnp.where(z >= 0.0, u * sig_abs, sig_abs)

    ri = lax.broadcasted_iota(jnp.int32, (c, c), 0)
    ci = lax.broadcasted_iota(jnp.int32, (c, c), 1)
    tri = jnp.where((ci >= ri) if rev else (ci <= ri), 1.0, 0.0).astype(BF16)
    g1, g2, g3 = _split3(log_f)
    b = _dot(tri, g1) + _dot(tri, g2) + _dot(tri, g3)
    end = 0 if rev else c - 1
    b_end = b[end:end + 1, :]

    q_in = (q * jnp.exp(b)).astype(BF16)
    k_out = (key * jnp.exp(b_end - b)).astype(BF16)
    decay = jnp.exp(b_end)

    up = pltpu.roll(log_f, 1, 0)
    dn = pltpu.roll(log_f, c - 1, 0)
    q_lv, k_lv, masks = [], [], []
    lvl = c // 2
    while lvl >= 1:
        half = (row // lvl) & 1
        q_rows = (half == 0) if rev else (half == 1)
        if lvl >= 4:
            pos = lvl if rev else lvl - 1
            ref = jnp.broadcast_to(b.reshape(c // (2 * lvl), 2 * lvl, w)[:, pos:pos + 1, :],
                                   (c // (2 * lvl), 2 * lvl, w)).reshape(c, w)
            eq, ek = b - ref, ref - b
        elif lvl == 2:
            j = row & 3
            if rev:
                eq = jnp.where(j == 1, log_f, log_f + dn)
                ek = jnp.where(j == 2, 0.0, up)
            else:
                eq = jnp.where(j == 2, log_f, log_f + up)
                ek = jnp.where(j == 1, 0.0, dn)
        else:
            eq, ek = log_f, jnp.zeros_like(log_f)
        q_lv.append((q * jnp.exp(jnp.where(q_rows, eq, NEG_BIG))).astype(BF16))
        k_lv.append((key * jnp.exp(jnp.where(q_rows, NEG_BIG, ek))).astype(BF16))
        masks.append(None if 2 * lvl == c else
                     jnp.where((ri // (2 * lvl)) == (ci // (2 * lvl)), 1.0, 0.0))
        lvl //= 2
    qk = q * key
    on_diag = ri == ci

    outs = []
    for h in range(HG_HEADS):
        sl = slice(h * HG_DK, (h + 1) * HG_DK)
        a = jnp.where(on_diag, jnp.sum(qk[:, sl], axis=-1, keepdims=True), 0.0)
        for q_l, k_l, msk in zip(q_lv, k_lv, masks):
            a_l = _dot_nt(q_l[:, sl], k_l[:, sl])
            a = a + (a_l if msk is None else a_l * msk)
        st = st_ref[h]
        v_h = v[:, sl].astype(BF16)
        outs.append(_dot_nt(q_in[:, sl], st.astype(BF16)) + _dot(a.astype(BF16), v_h))
        st_ref[h] = st * decay[:, sl] + _dot_tn(v_h, k_out[:, sl])
    return jnp.concatenate(outs, axis=1)


def _hgrn_kernel(qf_ref, zf_ref, vf_ref, qb_ref, zb_ref, vb_ref, lbf_ref, lbb_ref,
                 of_ref, ob_ref, sf_scr, sb_scr):
    @pl.when(pl.program_id(0) == 0)
    def _():
        sf_scr[...] = jnp.zeros(sf_scr.shape, F32)
        sb_scr[...] = jnp.zeros(sb_scr.shape, F32)

    n_chunks = qf_ref.shape[0] // HG_CHUNK

    def body(ci, carry):
        rf = pl.ds(pl.multiple_of(ci * HG_CHUNK, HG_CHUNK), HG_CHUNK)
        of_ref[rf, :] = _hg_chunk(qf_ref[rf, :], zf_ref[rf, :], vf_ref[rf, :], lbf_ref[...], sf_scr, False)
        rb = pl.ds(pl.multiple_of((n_chunks - 1 - ci) * HG_CHUNK, HG_CHUNK), HG_CHUNK)
        ob_ref[rb, :] = _hg_chunk(qb_ref[rb, :], zb_ref[rb, :], vb_ref[rb, :], lbb_ref[...], sb_scr, True)
        return carry

    lax.fori_loop(0, n_chunks, body, 0)


def _hgrn(p, lb_f, lb_b, n_lat):
    t = p.shape[0]
    tm = ROW_TILE
    nb = t // tm
    n_lat_b = n_lat // tm
    n_ctx_b = nb - n_lat_b

    def jf(i):
        return jnp.where(i < n_ctx_b, n_lat_b + i, i - n_ctx_b)

    def jb(i):
        return jnp.where(i < n_ctx_b, nb - 1 - i, nb - 1 - i)

    def spec(order, colblk):
        return pl.BlockSpec((tm, HG_W), lambda i: (order(i), colblk))

    return pl.pallas_call(
        _hgrn_kernel,
        grid=(nb,),
        in_specs=[spec(jf, 3), spec(jf, 4), spec(jf, 6), spec(jb, 3), spec(jb, 5), spec(jb, 6),
                  pl.BlockSpec((1, HG_W), lambda i: (0, 0)), pl.BlockSpec((1, HG_W), lambda i: (0, 0))],
        out_specs=[pl.BlockSpec((tm, HG_W), lambda i: (jf(i), 0)),
                   pl.BlockSpec((tm, HG_W), lambda i: (jb(i), 0))],
        out_shape=[jax.ShapeDtypeStruct((t, HG_W), F32), jax.ShapeDtypeStruct((t, HG_W), F32)],
        scratch_shapes=[pltpu.VMEM((HG_HEADS, HG_DK, HG_DK), F32), pltpu.VMEM((HG_HEADS, HG_DK, HG_DK), F32)],
        compiler_params=_cparams(("arbitrary",)),
        name="hgrn2",
    )(p, p, p, p, p, p, lb_f, lb_b)


def _post_kernel(x_ref, oa_ref, of_ref, ob_ref, gh_ref, ga_ref, gb_ref, mod_ref, hg_ref, n2_ref,
                 wa_ref, wb_ref, wo_ref, rw_ref, rb_ref,
                 x1_ref, h2_ref, gw_ref, gi_ref, rk_ref, cnt_ref, run_scr, *, n_lat, tm):
    i = pl.program_id(0)
    d = x_ref.shape[1]

    @pl.when(i == 0)
    def _():
        run_scr[...] = jnp.zeros(run_scr.shape, F32)

    lat = i * tm < n_lat

    def mod(c):
        return jnp.where(lat, mod_ref[0:1, c * d:(c + 1) * d], mod_ref[1:2, c * d:(c + 1) * d])

    o = of_ref[...] + ob_ref[...]
    parts = []
    for h in range(HG_HEADS):
        oh = o[:, h * HG_DK:(h + 1) * HG_DK]
        parts.append(oh * lax.rsqrt(jnp.mean(oh * oh, axis=-1, keepdims=True) + EPS))
    gh = gh_ref[...]
    ob = jnp.concatenate(parts, axis=1) * hg_ref[...] * (gh * _sigmoid(gh))
    ya = _dot(oa_ref[...], wa_ref[...])
    yb = _dot(ob.astype(BF16), wb_ref[...])
    mix = _sigmoid(ga_ref[...]) * ya + _sigmoid(gb_ref[...]) * yb
    x1 = x_ref[...] + mod(2) * _dot(mix.astype(BF16), wo_ref[...])
    x1_ref[...] = x1

    h2 = x1 * lax.rsqrt(jnp.mean(x1 * x1, axis=-1, keepdims=True) + EPS) * n2_ref[...]
    h2 = h2 * (1.0 + mod(4)) + mod(3)
    _store_row_tiles(h2_ref, h2)

    h_hi = h2.astype(BF16)
    h_lo = (h2 - h_hi.astype(F32)).astype(BF16)
    rw = rw_ref[...]
    r_hi = rw.astype(BF16)
    r_lo = (rw - r_hi.astype(F32)).astype(BF16)
    logits = _dot(h_hi, r_hi) + (_dot(h_hi, r_lo) + _dot(h_lo, r_hi)) + rb_ref[...]

    lane = lax.broadcasted_iota(jnp.int32, logits.shape, 1)
    work = jnp.where(lane < N_EXPERTS, logits, -jnp.inf)
    sel_any = jnp.zeros(logits.shape, F32)
    vals, idxs = [], []
    for _ in range(TOP_K):
        mx = jnp.max(work, axis=-1, keepdims=True)
        idx = jnp.min(jnp.where(work == mx, lane, 2 * N_EXPERTS), axis=-1, keepdims=True)
        hit = lane == idx
        vals.append(mx)
        idxs.append(idx)
        sel_any = jnp.where(hit, 1.0, sel_any)
        work = jnp.where(hit, -jnp.inf, work)
    es = [jnp.exp(vk - vals[0]) for vk in vals]
    denom = es[0] + es[1] + es[2] + es[3]

    ri = lax.broadcasted_iota(jnp.int32, (tm, tm), 0)
    ci = lax.broadcasted_iota(jnp.int32, (tm, tm), 1)
    below = jnp.where(ci < ri, 1.0, 0.0).astype(BF16)
    rank_e = run_scr[0:1, :] + _dot(below, sel_any.astype(BF16))
    gw = jnp.zeros(logits.shape, F32)
    gi = jnp.zeros(logits.shape, jnp.int32)
    rk = jnp.zeros(logits.shape, jnp.int32)
    for k in range(TOP_K):
        at_k = lane == k
        gw = jnp.where(at_k, es[k] / denom, gw)
        gi = jnp.where(at_k, idxs[k], gi)
        rnk = jnp.sum(jnp.where(lane == idxs[k], rank_e, 0.0), axis=-1, keepdims=True)
        rk = jnp.where(at_k, rnk.astype(jnp.int32), rk)
    gw_ref[...] = gw
    gi_ref[...] = gi.T[0:8, :]
    rk_ref[...] = rk.T[0:8, :]
    run = run_scr[0:1, :] + jnp.sum(sel_any, axis=0, keepdims=True)
    run_scr[...] = jnp.broadcast_to(run, run_scr.shape)
    cnt_ref[...] = jnp.broadcast_to(run, cnt_ref.shape)


def _post(x, oa, o_f, o_b, p, mod, hg_g, n2_g, wa, wb, wo, rw_pad, rb_pad, n_lat):
    t, d = x.shape
    tm = ROW_TILE

    def rows(width, colblk=0):
        return pl.BlockSpec((tm, width), lambda i: (i, colblk))

    def whole(a):
        return pl.BlockSpec(a.shape, lambda i: (0,) * a.ndim)

    return pl.pallas_call(
        functools.partial(_post_kernel, n_lat=n_lat, tm=tm),
        grid=(t // tm,),
        in_specs=[rows(d), rows(DA_W), rows(HG_W), rows(HG_W),
                  rows(HG_W, 7), rows(d, 4), rows(d, 5),
                  whole(mod), whole(hg_g), whole(n2_g), whole(wa), whole(wb), whole(wo),
                  whole(rw_pad), whole(rb_pad)],
        out_specs=[rows(d), pl.BlockSpec((tm, d // 128, 128), lambda i: (i, 0, 0)), rows(128),
                   pl.BlockSpec((8, tm), lambda i: (0, i)), pl.BlockSpec((8, tm), lambda i: (0, i)),
                   pl.BlockSpec((8, 128), lambda i: (0, 0))],
        out_shape=[jax.ShapeDtypeStruct((t, d), F32), jax.ShapeDtypeStruct((t, d // 128, 128), F32),
                   jax.ShapeDtypeStruct((t, 128), F32), jax.ShapeDtypeStruct((8, t), jnp.int32),
                   jax.ShapeDtypeStruct((8, t), jnp.int32), jax.ShapeDtypeStruct((8, 128), F32)],
        scratch_shapes=[pltpu.VMEM((8, 128), F32)],
        compiler_params=_cparams(("arbitrary",)),
        name="merge_router",
    )(x, oa, o_f, o_b, p, p, p, mod, hg_g, n2_g, wa, wb, wo, rw_pad, rb_pad)


def _dispatch_kernel(dest_ref, h_ref, xs_in, xs_out, sem):
    del xs_in
    n = dest_ref.shape[0]
    tm = h_ref.shape[0]

    for k in range(n // tm):
        def start(j, c, k=k):
            for q in range(DMA_QUEUES):
                tok = DMA_QUEUES * j + q
                pltpu.make_async_copy(h_ref.at[tok], xs_out.at[dest_ref[k * tm + tok]], sem).start(priority=q)
            return c

        lax.fori_loop(0, tm // DMA_QUEUES, start, 0, unroll=8)
    pltpu.make_async_copy(xs_out.at[pl.ds(0, n)], xs_out.at[pl.ds(0, n)], sem).wait()


def _dispatch(h2, dest_tiles, xs_zero):
    t = h2.shape[0]
    tm = ROW_TILE
    return pl.pallas_call(
        _dispatch_kernel,
        grid=(t // tm,),
        in_specs=[pl.BlockSpec((tm * TOP_K,), lambda i: (i,), memory_space=pltpu.SMEM),
                  pl.BlockSpec((tm,) + h2.shape[1:], lambda i: (i, 0, 0)),
                  pl.BlockSpec(memory_space=pl.ANY)],
        out_specs=pl.BlockSpec(memory_space=pl.ANY),
        out_shape=jax.ShapeDtypeStruct(xs_zero.shape, xs_zero.dtype),
        scratch_shapes=[pltpu.SemaphoreType.DMA(())],
        input_output_aliases={2: 0},
        compiler_params=_cparams(("arbitrary",)),
        name="moe_dispatch",
    )(dest_tiles, h2, xs_zero)


def _expert_kernel(be_ref, nx_ref, sl_ref, nu_ref, xs_ref, wgu_hbm, bgu_ref, wd_hbm, bd_ref, ys_ref,
                   wgu_f32, wd_f32, wgu_bf, wd_bf, sems, *, layer):
    i = pl.program_id(0)
    used = i < nu_ref[0]

    def weight_copies(e, slot):
        return (pltpu.make_async_copy(wgu_hbm.at[layer, e], wgu_f32.at[slot], sems.at[0, slot]),
                pltpu.make_async_copy(wd_hbm.at[layer, e], wd_f32.at[slot], sems.at[1, slot]))

    @pl.when(i == 0)
    def _():
        for cp in weight_copies(be_ref[0], 0):
            cp.start()

    @pl.when(jnp.logical_and(used, jnp.logical_or(i == 0, be_ref[i] != be_ref[jnp.maximum(i - 1, 0)])))
    def _():
        slot = sl_ref[i]
        for cp in weight_copies(be_ref[i], slot):
            cp.wait()

        @pl.when(nx_ref[i] >= 0)
        def _():
            for cp in weight_copies(nx_ref[i], 1 - slot):
                cp.start()

        wgu_bf[...] = wgu_f32[slot].astype(BF16)
        wd_bf[...] = wd_f32[slot].astype(BF16)

    @pl.when(used)
    def _():
        x = _load_row_tiles(xs_ref, 0, xs_ref.shape[0]).astype(BF16)
        gu = _dot(x, wgu_bf[...]) + bgu_ref[0, 0]
        gate = jnp.minimum(gu[:, :D_FF], SWIGLU_LIMIT)
        up = jnp.clip(gu[:, D_FF:], -SWIGLU_LIMIT, SWIGLU_LIMIT)
        hdn = (up + 1.0) * gate * _sigmoid(SWIGLU_ALPHA * gate)
        _store_row_tiles(ys_ref, _dot(hdn.astype(BF16), wd_bf[...]) + bd_ref[0, 0])

    @pl.when(jnp.logical_not(used))
    def _():
        ys_ref[...] = jnp.zeros(ys_ref.shape, F32)


def _experts(xs, plan, wgu, bgu, wd, bd, layer):
    blk_e, next_e, slot, n_used = plan
    n_rows = xs.shape[0]
    d = xs.shape[1] * xs.shape[2]
    nblk = n_rows // MOE_BLOCK

    def blk(i, nu):
        return jnp.minimum(i, nu[0] - 1)

    def bias(i, be, nx, sl, nu):
        return (layer, be[blk(i, nu)], 0, 0)

    grid_spec = pltpu.PrefetchScalarGridSpec(
        num_scalar_prefetch=4,
        grid=(nblk,),
        in_specs=[
            pl.BlockSpec((MOE_BLOCK,) + xs.shape[1:], lambda i, be, nx, sl, nu: (blk(i, nu), 0, 0)),
            pl.BlockSpec(memory_space=pl.ANY),
            pl.BlockSpec((1, 1, 1, 2 * D_FF), bias),
            pl.BlockSpec(memory_space=pl.ANY),
            pl.BlockSpec((1, 1, 1, d), bias),
        ],
        out_specs=pl.BlockSpec((MOE_BLOCK,) + xs.shape[1:], lambda i, be, nx, sl, nu: (i, 0, 0)),
        scratch_shapes=[pltpu.VMEM((2, d, 2 * D_FF), F32), pltpu.VMEM((2, D_FF, d), F32),
                        pltpu.VMEM((d, 2 * D_FF), BF16), pltpu.VMEM((D_FF, d), BF16),
                        pltpu.SemaphoreType.DMA((2, 2))],
    )
    return pl.pallas_call(
        functools.partial(_expert_kernel, layer=layer),
        grid_spec=grid_spec,
        out_shape=jax.ShapeDtypeStruct(xs.shape, F32),
        compiler_params=_cparams(("arbitrary",)),
        name="moe_experts",
    )(blk_e, next_e, slot, n_used, xs, wgu, bgu, wd, bd)


def _combine_kernel(dest_ref, x1_ref, gw_ref, mod_ref, ys_ref, o_ref, buf, sem, *, n_lat, tm):
    i = pl.program_id(0)
    d = x1_ref.shape[1]
    n = dest_ref.shape[0]

    def start(j, c):
        for q in range(DMA_QUEUES):
            r = DMA_QUEUES * j + q
            pltpu.make_async_copy(ys_ref.at[dest_ref[r]], buf.at[r], sem).start(priority=q)
        return c

    lax.fori_loop(0, n // DMA_QUEUES, start, 0, unroll=8)
    pltpu.make_async_copy(ys_ref.at[pl.ds(0, n)], buf, sem).wait()

    lat = i * tm < n_lat
    g2 = jnp.where(lat, mod_ref[0:1, 5 * d:6 * d], mod_ref[1:2, 5 * d:6 * d])
    gw = gw_ref[...]
    m = gw[:, 0:1] * _load_row_tiles(buf, 0, tm)
    for k in range(1, TOP_K):
        m = m + gw[:, k:k + 1] * _load_row_tiles(buf, k * tm, tm)
    o_ref[...] = x1_ref[...] + g2 * m


def _combine(x1, gw, mod, ys, dest_tiles, n_lat):
    t, d = x1.shape
    tm = ROW_TILE
    return pl.pallas_call(
        functools.partial(_combine_kernel, n_lat=n_lat, tm=tm),
        grid=(t // tm,),
        in_specs=[pl.BlockSpec((tm * TOP_K,), lambda i: (i,), memory_space=pltpu.SMEM),
                  pl.BlockSpec((tm, d), lambda i: (i, 0)),
                  pl.BlockSpec((tm, 128), lambda i: (i, 0)),
                  pl.BlockSpec(mod.shape, lambda i: (0, 0)),
                  pl.BlockSpec(memory_space=pl.ANY)],
        out_specs=pl.BlockSpec((tm, d), lambda i: (i, 0)),
        out_shape=jax.ShapeDtypeStruct((t, d), F32),
        scratch_shapes=[pltpu.VMEM((tm * TOP_K,) + ys.shape[1:], F32), pltpu.SemaphoreType.DMA(())],
        compiler_params=_cparams(("arbitrary",)),
        name="moe_combine",
    )(dest_tiles, x1, gw, mod, ys)


def _moe_plan(gi, rk, counts):
    t = gi.shape[1]
    top_i = gi[:TOP_K]
    rank = rk[:TOP_K]
    cnt = counts[0, :N_EXPERTS].astype(jnp.int32)
    padded = (cnt + MOE_BLOCK - 1) // MOE_BLOCK * MOE_BLOCK
    pad_ends = jnp.cumsum(padded)
    pad_starts = pad_ends - padded
    dest = pad_starts[top_i] + rank
    n_rows = t * TOP_K + N_EXPERTS * MOE_BLOCK
    blk_start = jnp.arange(n_rows // MOE_BLOCK, dtype=jnp.int32) * MOE_BLOCK
    blk_e = jnp.minimum(jnp.sum(blk_start[:, None] >= pad_ends[None, :], axis=1), N_EXPERTS - 1).astype(jnp.int32)
    n_used = (pad_ends[-1:] // MOE_BLOCK).astype(jnp.int32)
    has_rows = cnt > 0
    later = jnp.where(has_rows[None, :] & (jnp.arange(N_EXPERTS)[None, :] > jnp.arange(N_EXPERTS)[:, None]),
                      jnp.arange(N_EXPERTS)[None, :], N_EXPERTS)
    next_of = jnp.min(later, axis=1)
    next_of = jnp.where(next_of < N_EXPERTS, next_of, -1).astype(jnp.int32)
    run_of = (jnp.cumsum(has_rows.astype(jnp.int32)) - 1).astype(jnp.int32)
    next_e = next_of[blk_e]
    slot = run_of[blk_e] & 1
    dest_tiles = dest.reshape(TOP_K, t // ROW_TILE, ROW_TILE).transpose(1, 0, 2).reshape(-1).astype(jnp.int32)
    return dest_tiles, (blk_e, next_e, slot, n_used), n_rows


def _rope_tables(n_lat, n_ctx):
    pos = jnp.arange(n_lat)
    row = (pos // GRID_W).astype(F32)
    col = (pos % GRID_W).astype(F32)
    freq = ROPE_BASE ** (-jnp.arange(ROPE_PAIRS, dtype=F32) / ROPE_PAIRS)
    ra = row[:, None] * freq
    ca = col[:, None] * freq
    cos64 = jnp.concatenate([jnp.cos(ra), jnp.cos(ra), jnp.cos(ca), jnp.cos(ca)], axis=1)
    sin64 = jnp.concatenate([-jnp.sin(ra), jnp.sin(ra), -jnp.sin(ca), jnp.sin(ca)], axis=1)
    cos_t = jnp.concatenate([jnp.tile(cos64, (1, DA_W // DA_DH)), jnp.ones((n_ctx, DA_W), F32)], axis=0)
    sin_t = jnp.concatenate([jnp.tile(sin64, (1, DA_W // DA_DH)), jnp.zeros((n_ctx, DA_W), F32)], axis=0)
    return cos_t, sin_t


def kernel(x, c, ctx, c_ctx, ada_w, ada_b, norm1_g, norm2_g, w_in, qn_g, kn_g, lam_qk, subln_g, hg_lb,
           hg_norm_g, w_branch_a, w_branch_b, w_out, router_w, router_b, w_gu, b_gu, w_down, b_down):
    bsz, n_lat, d = x.shape
    n_ctx = ctx.shape[1]
    depth = ada_w.shape[0]
    assert bsz == 1 and d == D_MODEL and n_lat % 512 == 0 and n_ctx % ROW_TILE == 0
    t = n_lat + n_ctx

    xx = jnp.concatenate([x[0], ctx[0]], axis=0)
    cc = jnp.zeros((8, d), F32).at[0].set(c[0]).at[1].set(c_ctx)
    mods = _modulation(cc, ada_w, ada_b)

    cos_t, sin_t = _rope_tables(n_lat, n_ctx)
    lane = jnp.arange(DA_W)
    seg64 = (lane[:, None] // DA_DH == lane[None, :] // DA_DH).astype(BF16)
    cs = jnp.cumsum(jax.nn.softmax(hg_lb.astype(F32), axis=1), axis=1)
    lb_all = cs - cs[:, :1]
    rw_pad = jnp.zeros((depth, d, 128), F32).at[:, :, :N_EXPERTS].set(router_w)
    rb_pad = jnp.zeros((depth, 1, 128), F32).at[:, 0, :N_EXPERTS].set(router_b)

    key_chunk = next(c for c in ATT_KEY_CHUNKS if t % c == 0 and n_ctx <= c)
    xs = None
    for l in range(depth):
        lam_init = 0.8 - 0.6 * math.exp(-0.3 * l)
        mod = mods[l]
        p = _inproj(xx, norm1_g[l][None], mod, w_in[l].astype(BF16), n_lat)
        k_r, q_t, v_t3, qn2, kn2_tiles = _prep(p, cos_t, sin_t, seg64, jnp.tile(qn_g[l], DA_W // DA_DH)[None],
                                               jnp.tile(kn_g[l], DA_W // DA_DH)[None], key_chunk)
        oa = _attention_all(q_t, k_r, v_t3, qn2, kn2_tiles, lam_qk[l], subln_g[l][:, None], lam_init,
                            n_lat, n_ctx)
        o_f, o_b = _hgrn(p, lb_all[0, l][None], lb_all[1, l][None], n_lat)
        x1, h2, gw, gi, rk, counts = _post(
            xx, oa, o_f, o_b, p, mod, jnp.tile(hg_norm_g[l], HG_HEADS)[None], norm2_g[l][None],
            w_branch_a[l].astype(BF16), w_branch_b[l].astype(BF16), w_out[l].astype(BF16),
            rw_pad[l], rb_pad[l], n_lat)
        dest_tiles, plan, n_rows = _moe_plan(gi, rk, counts)
        xs = _dispatch(h2, dest_tiles, jnp.zeros((n_rows,) + h2.shape[1:], F32) if xs is None else xs)
        ys = _experts(xs, plan, w_gu, b_gu[:, :, None, :], w_down, b_down[:, :, None, :], l)
        xx = _combine(x1, gw, mod, ys, dest_tiles, n_lat)
    return xx[:n_lat][None]
```

```python
import functools
import math

import jax
import jax.numpy as jnp
from jax import lax
from jax.experimental import pallas as pl
from jax.experimental.pallas import tpu as pltpu

F32 = jnp.float32
BF16 = jnp.bfloat16
QK_DTYPE = jnp.float8_e4m3fn

D_MODEL = 1024
GRID_W = 64
EPS = 1e-6

DA_HEADS = 4
DA_DH = 64
DA_DV = 2 * DA_DH
DA_W = DA_HEADS * DA_DV
DA_SCALE = DA_DH ** -0.5
ROPE_PAIRS = DA_DH // 4
ROPE_BASE = 10000.0

HG_HEADS = 4
HG_DK = 128
HG_W = HG_HEADS * HG_DK
HG_CHUNK = 128

N_EXPERTS = 32
TOP_K = 4
D_FF = 1024
SWIGLU_ALPHA = 1.702
SWIGLU_LIMIT = 7.0
MOE_BLOCK = 256

IN_COLS = 6144
ROW_TILE = 256
NEG_BIG = -1e30
LOG2E = 1.4426950408889634

VMEM_LIMIT = 56 * 1024 * 1024
ATT_TQ = 256
ATT_KEY_CHUNKS = (3328, 1280, 256)
DMA_QUEUES = 2
SCORE_BOUND_MAX = 40.0
SCORE_BOUND_SLACK = 1.01


def _cparams(sem):
    return pltpu.CompilerParams(dimension_semantics=sem, vmem_limit_bytes=VMEM_LIMIT)


def _split3(x):
    a = x.astype(BF16)
    r = x - a.astype(F32)
    b = r.astype(BF16)
    c = (r - b.astype(F32)).astype(BF16)
    return a, b, c


def _dot(a, b):
    return jnp.dot(a, b, preferred_element_type=F32)


def _dot_nt(a, b):
    return lax.dot_general(a, b, (((1,), (1,)), ((), ())), preferred_element_type=F32)


def _dot_tn(a, b):
    return lax.dot_general(a, b, (((0,), (0,)), ((), ())), preferred_element_type=F32)


def _sigmoid(x):
    return 0.5 * jnp.tanh(0.5 * x) + 0.5


def _tile_transpose(x):
    g = x.shape[0]
    sub = lax.broadcasted_iota(jnp.int32, (1, 1, 1, 1, 8, 128), 4)
    x = x.reshape(g, 2, 2, 2, 8, 128)
    for axis, k in ((1, 4), (2, 2), (3, 1)):
        lo = (sub & k) == 0
        a = lax.index_in_dim(x, 0, axis, keepdims=True)
        b = lax.index_in_dim(x, 1, axis, keepdims=True)
        a2 = jnp.where(lo, a, pltpu.roll(b, k, 4))
        b2 = jnp.where(lo, pltpu.roll(a, 8 - k, 4), b)
        x = jnp.concatenate([a2, b2], axis=axis)
    return x.reshape(g, 8, 8, 128)


def _store_row_tiles(ref, val):
    n = val.shape[0]
    z = jnp.stack([val[:, s * 128:(s + 1) * 128].reshape(n // 8, 8, 128) for s in range(8)], axis=1)
    ref[...] = _tile_transpose(z).reshape(n, 8, 128)


def _load_row_tiles(ref, lo, n):
    y = _tile_transpose(ref[lo:lo + n].reshape(n // 8, 8, 8, 128))
    return jnp.concatenate([y[:, s].reshape(n, 128) for s in range(8)], axis=1)


def _mod_kernel(c_ref, w_ref, b_ref, o_ref):
    cv = c_ref[...]
    a = cv * _sigmoid(cv)
    w = w_ref[0]
    a1, a2, a3 = _split3(a)
    w1, w2, w3 = _split3(w)
    acc = _dot(a1, w1) + (_dot(a1, w2) + _dot(a2, w1)) + (_dot(a2, w2) + _dot(a1, w3) + _dot(a3, w1))
    o_ref[0] = acc + b_ref[0]


def _modulation(cc, ada_w, ada_b):
    depth, d, n = ada_w.shape
    tn = 1536
    return pl.pallas_call(
        _mod_kernel,
        grid=(depth, n // tn),
        in_specs=[
            pl.BlockSpec((8, d), lambda l, j: (0, 0)),
            pl.BlockSpec((1, d, tn), lambda l, j: (l, 0, j)),
            pl.BlockSpec((1, 1, tn), lambda l, j: (l, 0, j)),
        ],
        out_specs=pl.BlockSpec((1, 8, tn), lambda l, j: (l, 0, j)),
        out_shape=jax.ShapeDtypeStruct((depth, 8, n), F32),
        compiler_params=_cparams(("parallel", "parallel")),
        name="adaln_mod",
    )(cc, ada_w, ada_b.reshape(depth, 1, n))


def _inproj_kernel(x_ref, g_ref, mod_ref, w_ref, o_ref, h_scr, *, n_lat, tm):
    i = pl.program_id(0)
    j = pl.program_id(1)
    d = x_ref.shape[1]

    @pl.when(j == 0)
    def _():
        x = x_ref[...]
        y = x * lax.rsqrt(jnp.mean(x * x, axis=-1, keepdims=True) + EPS) * g_ref[...]
        row = i * tm + lax.broadcasted_iota(jnp.int32, (tm, 1), 0)
        lat = row < n_lat
        sh = jnp.where(lat, mod_ref[0:1, 0:d], mod_ref[1:2, 0:d])
        sc = jnp.where(lat, mod_ref[0:1, d:2 * d], mod_ref[1:2, d:2 * d])
        h_scr[...] = (y * (1.0 + sc) + sh).astype(BF16)

    o_ref[...] = _dot(h_scr[...], w_ref[...])


def _inproj(x, g, mod, w_bf, n_lat):
    t, d = x.shape
    n = w_bf.shape[1]
    tm = 1280 if t % 1280 == 0 else ROW_TILE
    tn = 1536
    return pl.pallas_call(
        functools.partial(_inproj_kernel, n_lat=n_lat, tm=tm),
        grid=(t // tm, n // tn),
        in_specs=[
            pl.BlockSpec((tm, d), lambda i, j: (i, 0)),
            pl.BlockSpec((1, d), lambda i, j: (0, 0)),
            pl.BlockSpec((8, mod.shape[1]), lambda i, j: (0, 0)),
            pl.BlockSpec((d, tn), lambda i, j: (0, j)),
        ],
        out_specs=pl.BlockSpec((tm, tn), lambda i, j: (i, j)),
        out_shape=jax.ShapeDtypeStruct((t, n), F32),
        scratch_shapes=[pltpu.VMEM((tm, d), BF16)],
        compiler_params=_cparams(("parallel", "arbitrary")),
        name="inproj",
    )(x, g, mod, w_bf)


def _segment_mean_sq(x, seg_ref, width):
    x2 = x * x
    hi = x2.astype(BF16)
    lo = (x2 - hi.astype(F32)).astype(BF16)
    seg = seg_ref[...]
    return (_dot(hi, seg) + _dot(lo, seg)) * (1.0 / width)


def _prep_kernel(q_ref, k_ref, v_ref, cos_ref, sin_ref, seg_ref, qg_ref, kg_ref,
                 k_out, qt_out, vt_out, qn_out, kn_out):
    tm = q_ref.shape[0]
    cos = cos_ref[...]
    sin = sin_ref[...]
    lane = lax.broadcasted_iota(jnp.int32, (tm, DA_W), 1)
    first_half = (lane & 31) < 16

    def norm_rope(x, g):
        y = x * lax.rsqrt(_segment_mean_sq(x, seg_ref, DA_DH) + EPS) * g
        fwd = pltpu.roll(y, DA_W - 16, 1)
        bwd = pltpu.roll(y, 16, 1)
        partner = jnp.where(first_half, fwd, bwd)
        return y * cos + partner * sin

    k_b = norm_rope(k_ref[...], kg_ref[...]).astype(QK_DTYPE)
    k_out[...] = k_b
    k_f = k_b.astype(F32)
    k_n2 = _dot((k_f * k_f).astype(BF16), seg_ref[...])
    kn_out[...] = jnp.broadcast_to(jnp.max(k_n2, axis=0, keepdims=True), kn_out.shape)

    q = norm_rope(q_ref[...], qg_ref[...]) * (DA_SCALE * LOG2E)
    lane_h = lax.broadcasted_iota(jnp.int32, (tm, DA_DV), 1)
    v = v_ref[...]
    for h in range(DA_HEADS):
        qh = q[:, h * DA_DV:(h + 1) * DA_DV]
        for m in range(2):
            keep = (lane_h < DA_DH) if m == 0 else (lane_h >= DA_DH)
            r = 2 * h + m
            qt_b = jnp.where(keep, qh, 0.0).T.astype(QK_DTYPE)
            qt_out[r * DA_DV:(r + 1) * DA_DV, :] = qt_b
            qt_f = qt_b.astype(F32)
            qn_out[r:r + 1, :] = jnp.sum(qt_f * qt_f, axis=0, keepdims=True)
        vt_out[0, h * DA_DV:(h + 1) * DA_DV, :] = v[:, h * DA_DV:(h + 1) * DA_DV].T.astype(BF16)


def _prep(p, cos_t, sin_t, seg64, qg, kg, chunk):
    t = p.shape[0]
    tm = ROW_TILE
    per_chunk = chunk // tm
    return pl.pallas_call(
        _prep_kernel,
        grid=(t // tm,),
        in_specs=[
            pl.BlockSpec((tm, DA_W), lambda i: (i, 0)),
            pl.BlockSpec((tm, DA_W), lambda i: (i, 1)),
            pl.BlockSpec((tm, DA_W), lambda i: (i, 2)),
            pl.BlockSpec((tm, DA_W), lambda i: (i, 0)),
            pl.BlockSpec((tm, DA_W), lambda i: (i, 0)),
            pl.BlockSpec((DA_W, DA_W), lambda i: (0, 0)),
            pl.BlockSpec((1, DA_W), lambda i: (0, 0)),
            pl.BlockSpec((1, DA_W), lambda i: (0, 0)),
        ],
        out_specs=[
            pl.BlockSpec((tm, DA_W), lambda i: (i, 0)),
            pl.BlockSpec((2 * DA_HEADS * DA_DV, tm), lambda i: (0, i)),
            pl.BlockSpec((1, DA_W, tm), lambda i: (i // per_chunk, 0, i % per_chunk)),
            pl.BlockSpec((2 * DA_HEADS, tm), lambda i: (0, i)),
            pl.BlockSpec((8, DA_W), lambda i: (i, 0)),
        ],
        out_shape=[
            jax.ShapeDtypeStruct((t, DA_W), QK_DTYPE),
            jax.ShapeDtypeStruct((2 * DA_HEADS * DA_DV, t), QK_DTYPE),
            jax.ShapeDtypeStruct((t // chunk, DA_W, chunk), BF16),
            jax.ShapeDtypeStruct((2 * DA_HEADS, t), F32),
            jax.ShapeDtypeStruct((8 * (t // tm), DA_W), F32),
        ],
        compiler_params=_cparams(("parallel",)),
        name="qkv_prep",
    )(p, p, p, cos_t, sin_t, seg64, qg, kg)


def _attn_finalize(lam_ref, sg_ref, o_ref, acc_scr, l_scr, lam_init):
    lq = lam_ref[...]
    lam = (jnp.exp(jnp.sum(lq[0:1] * lq[1:2], axis=-1, keepdims=True))
           - jnp.exp(jnp.sum(lq[2:3] * lq[3:4], axis=-1, keepdims=True)) + lam_init)
    for h in range(DA_HEADS):
        r = 2 * h
        o = acc_scr[r] / l_scr[r:r + 1, :] - lam * (acc_scr[r + 1] / l_scr[r + 1:r + 2, :])
        o = o * lax.rsqrt(jnp.mean(o * o, axis=0, keepdims=True) + EPS)
        o = o * sg_ref[...] * (1.0 - lam_init)
        o_ref[:, h * DA_DV:(h + 1) * DA_DV] = o.T.astype(o_ref.dtype)


def _attn_kernel(qt_ref, k_hbm, vt_hbm, lam_ref, sg_ref, o_ref, k_scr, vt_scr, m_scr, l_scr, acc_scr, sems,
                 *, lam_init, stabilised):
    @pl.when(pl.program_id(0) == 0)
    def _():
        copies = (pltpu.make_async_copy(k_hbm, k_scr, sems.at[0]), pltpu.make_async_copy(vt_hbm, vt_scr, sems.at[1]))
        for cp in copies:
            cp.start()
        for cp in copies:
            cp.wait()

    acc_scr[...] = jnp.zeros(acc_scr.shape, F32)
    l_scr[...] = jnp.zeros(l_scr.shape, F32)
    if stabilised:
        m_scr[...] = jnp.full(m_scr.shape, NEG_BIG, F32)

    def chunk(c, carry):
        for h in range(DA_HEADS):
            k_h = k_scr[c, :, h * DA_DV:(h + 1) * DA_DV]
            vt_h = vt_scr[c, h * DA_DV:(h + 1) * DA_DV, :]
            for m in range(2):
                r = 2 * h + m
                s = _dot(k_h, qt_ref[r * DA_DV:(r + 1) * DA_DV, :])
                if stabilised:
                    m_prev = m_scr[r:r + 1, :]
                    m_new = jnp.maximum(m_prev, jnp.max(s, axis=0, keepdims=True))
                    alpha = jnp.exp2(m_prev - m_new)
                    p = jnp.exp2((s - m_new).astype(BF16))
                    acc_scr[r] = alpha * acc_scr[r] + _dot(vt_h, p)
                    l_scr[r:r + 1, :] = alpha * l_scr[r:r + 1, :] + jnp.sum(p.astype(F32), axis=0, keepdims=True)
                    m_scr[r:r + 1, :] = m_new
                else:
                    p = jnp.exp2(s.astype(BF16))
                    acc_scr[r] += _dot(vt_h, p)
                    l_scr[r:r + 1, :] += jnp.sum(p.astype(F32), axis=0, keepdims=True)
        return carry

    lax.fori_loop(0, k_scr.shape[0], chunk, 0)
    _attn_finalize(lam_ref, sg_ref, o_ref, acc_scr, l_scr, lam_init)


def _attention(qt, k3, vt3, lam_qk, subln_col, lam_init, stabilised, *, q_start, n_q, tq):
    qo = q_start // tq
    return pl.pallas_call(
        functools.partial(_attn_kernel, lam_init=lam_init, stabilised=stabilised),
        grid=(n_q // tq,),
        in_specs=[
            pl.BlockSpec((qt.shape[0], tq), lambda i: (0, i + qo)),
            pl.BlockSpec(memory_space=pl.ANY),
            pl.BlockSpec(memory_space=pl.ANY),
            pl.BlockSpec((4, DA_DH), lambda i: (0, 0)),
            pl.BlockSpec((DA_DV, 1), lambda i: (0, 0)),
        ],
        out_specs=pl.BlockSpec((tq, DA_W), lambda i: (i, 0)),
        out_shape=jax.ShapeDtypeStruct((n_q, DA_W), BF16),
        scratch_shapes=[
            pltpu.VMEM(k3.shape, k3.dtype),
            pltpu.VMEM(vt3.shape, vt3.dtype),
            pltpu.VMEM((2 * DA_HEADS, tq), F32),
            pltpu.VMEM((2 * DA_HEADS, tq), F32),
            pltpu.VMEM((2 * DA_HEADS, DA_DV, tq), F32),
            pltpu.SemaphoreType.DMA((2,)),
        ],
        compiler_params=_cparams(("arbitrary",)),
        name="diff_attn_online" if stabilised else "diff_attn_plain",
    )(qt, k3, vt3, lam_qk, subln_col)


def _attention_all(qt, k, vt3, qn2, kn2_tiles, lam_qk, subln_col, lam_init, n_lat, n_ctx):
    n_chunks, _, chunk = vt3.shape
    n_maps = 2 * DA_HEADS
    kmax2 = jnp.max(kn2_tiles, axis=0).reshape(n_maps, DA_DH)[:, 0]
    bound = jnp.sqrt(jnp.max(qn2, axis=1)) * jnp.sqrt(kmax2) * SCORE_BOUND_SLACK
    bounded = jnp.max(bound) <= SCORE_BOUND_MAX
    k3 = k.reshape(n_chunks, chunk, DA_W)

    def run(stabilised):
        def f(args):
            qt_, k3_, vt3_, lam_, sg_ = args
            lat = _attention(qt_, k3_, vt3_, lam_, sg_, lam_init, stabilised, q_start=0, n_q=n_lat, tq=ATT_TQ)
            ctx = _attention(qt_, k3_[n_chunks - 1:, chunk - n_ctx:, :], vt3_[n_chunks - 1:, :, chunk - n_ctx:],
                             lam_, sg_, lam_init, stabilised, q_start=n_lat, n_q=n_ctx, tq=ATT_TQ)
            return jnp.concatenate([lat, ctx], axis=0)
        return f

    return lax.cond(bounded, run(False), run(True), (qt, k3, vt3, lam_qk, subln_col))


def _hg_chunk(q, z, v, lb, st_ref, rev):
    c, w = q.shape
    row = lax.broadcasted_iota(jnp.int32, (c, 1), 0)

    u = jnp.exp(-jnp.abs(z))
    sig_abs = 1.0 / (1.0 + u)
    y = jnp.log(1.0 - lb) + jnp.minimum(z, 0.0) - jnp.log(1.0 + u)
    log_lb = jnp.log(lb)
    log_f = jnp.maximum(log_lb, y) + jnp.log(1.0 + jnp.exp(-jnp.abs(log_lb - y)))
    key = (1.0 - lb) * jnp.where(z >= 0.0, u * sig_abs, sig_abs)

    ri = lax.broadcasted_iota(jnp.int32, (c, c), 0)
    ci = lax.broadcasted_iota(jnp.int32, (c, c), 1)
    tri = jnp.where((ci >= ri) if rev else (ci <= ri), 1.0, 0.0).astype(BF16)
    g1, g2, g3 = _split3(log_f)
    b = _dot(tri, g1) + _dot(tri, g2) + _dot(tri, g3)
    end = 0 if rev else c - 1
    b_end = b[end:end + 1, :]

    q_in = (q * jnp.exp(b)).astype(BF16)
    k_out = (key * jnp.exp(b_end - b)).astype(BF16)
    decay = jnp.exp(b_end)

    up = pltpu.roll(log_f, 1, 0)
    dn = pltpu.roll(log_f, c - 1, 0)
    q_lv, k_lv, masks = [], [], []
    lvl = c // 2
    while lvl >= 1:
        half = (row // lvl) & 1
        q_rows = (half == 0) if rev else (half == 1)
        if lvl >= 4:
            pos = lvl if rev else lvl - 1
            ref = jnp.broadcast_to(b.reshape(c // (2 * lvl), 2 * lvl, w)[:, pos:pos + 1, :],
                                   (c // (2 * lvl), 2 * lvl, w)).reshape(c, w)
            eq, ek = b - ref, ref - b
        elif lvl == 2:
            j = row & 3
            if rev:
                eq = jnp.where(j == 1, log_f, log_f + dn)
                ek = jnp.where(j == 2, 0.0, up)
            else:
                eq = jnp.where(j == 2, log_f, log_f + up)
                ek = jnp.where(j == 1, 0.0, dn)
        else:
            eq, ek = log_f, jnp.zeros_like(log_f)
        q_lv.append((q * jnp.exp(jnp.where(q_rows, eq, NEG_BIG))).astype(BF16))
        k_lv.append((key * jnp.exp(jnp.where(q_rows, NEG_BIG, ek))).astype(BF16))
        masks.append(None if 2 * lvl == c else
                     jnp.where((ri // (2 * lvl)) == (ci // (2 * lvl)), 1.0, 0.0))
        lvl //= 2
    qk = q * key
    on_diag = ri == ci

    outs = []
    for h in range(HG_HEADS):
        sl = slice(h * HG_DK, (h + 1) * HG_DK)
        a = jnp.where(on_diag, jnp.sum(qk[:, sl], axis=-1, keepdims=True), 0.0)
        for q_l, k_l, msk in zip(q_lv, k_lv, masks):
            a_l = _dot_nt(q_l[:, sl], k_l[:, sl])
            a = a + (a_l if msk is None else a_l * msk)
        st = st_ref[h]
        v_h = v[:, sl].astype(BF16)
        outs.append(_dot_nt(q_in[:, sl], st.astype(BF16)) + _dot(a.astype(BF16), v_h))
        st_ref[h] = st * decay[:, sl] + _dot_tn(v_h, k_out[:, sl])
    return jnp.concatenate(outs, axis=1)


def _hgrn_kernel(qf_ref, zf_ref, vf_ref, qb_ref, zb_ref, vb_ref, lbf_ref, lbb_ref,
                 of_ref, ob_ref, sf_scr, sb_scr):
    @pl.when(pl.program_id(0) == 0)
    def _():
        sf_scr[...] = jnp.zeros(sf_scr.shape, F32)
        sb_scr[...] = jnp.zeros(sb_scr.shape, F32)

    n_chunks = qf_ref.shape[0] // HG_CHUNK

    def body(ci, carry):
        rf = pl.ds(pl.multiple_of(ci * HG_CHUNK, HG_CHUNK), HG_CHUNK)
        of_ref[rf, :] = _hg_chunk(qf_ref[rf, :], zf_ref[rf, :], vf_ref[rf, :], lbf_ref[...], sf_scr, False)
        rb = pl.ds(pl.multiple_of((n_chunks - 1 - ci) * HG_CHUNK, HG_CHUNK), HG_CHUNK)
        ob_ref[rb, :] = _hg_chunk(qb_ref[rb, :], zb_ref[rb, :], vb_ref[rb, :], lbb_ref[...], sb_scr, True)
        return carry

    lax.fori_loop(0, n_chunks, body, 0)


def _hgrn(p, lb_f, lb_b, n_lat):
    t = p.shape[0]
    tm = ROW_TILE
    nb = t // tm
    n_lat_b = n_lat // tm
    n_ctx_b = nb - n_lat_b

    def jf(i):
        return jnp.where(i < n_ctx_b, n_lat_b + i, i - n_ctx_b)

    def jb(i):
        return jnp.where(i < n_ctx_b, nb - 1 - i, nb - 1 - i)

    def spec(order, colblk):
        return pl.BlockSpec((tm, HG_W), lambda i: (order(i), colblk))

    return pl.pallas_call(
        _hgrn_kernel,
        grid=(nb,),
        in_specs=[spec(jf, 3), spec(jf, 4), spec(jf, 6), spec(jb, 3), spec(jb, 5), spec(jb, 6),
                  pl.BlockSpec((1, HG_W), lambda i: (0, 0)), pl.BlockSpec((1, HG_W), lambda i: (0, 0))],
        out_specs=[pl.BlockSpec((tm, HG_W), lambda i: (jf(i), 0)),
                   pl.BlockSpec((tm, HG_W), lambda i: (jb(i), 0))],
        out_shape=[jax.ShapeDtypeStruct((t, HG_W), F32), jax.ShapeDtypeStruct((t, HG_W), F32)],
        scratch_shapes=[pltpu.VMEM((HG_HEADS, HG_DK, HG_DK), F32), pltpu.VMEM((HG_HEADS, HG_DK, HG_DK), F32)],
        compiler_params=_cparams(("arbitrary",)),
        name="hgrn2",
    )(p, p, p, p, p, p, lb_f, lb_b)


def _post_kernel(x_ref, oa_ref, of_ref, ob_ref, gh_ref, ga_ref, gb_ref, mod_ref, hg_ref, n2_ref,
                 wa_ref, wb_ref, wo_ref, rw_ref, rb_ref,
                 x1_ref, h2_ref, gw_ref, gi_ref, rk_ref, cnt_ref, run_scr, *, n_lat, tm):
    i = pl.program_id(0)
    d = x_ref.shape[1]

    @pl.when(i == 0)
    def _():
        run_scr[...] = jnp.zeros(run_scr.shape, F32)

    lat = i * tm < n_lat

    def mod(c):
        return jnp.where(lat, mod_ref[0:1, c * d:(c + 1) * d], mod_ref[1:2, c * d:(c + 1) * d])

    o = of_ref[...] + ob_ref[...]
    parts = []
    for h in range(HG_HEADS):
        oh = o[:, h * HG_DK:(h + 1) * HG_DK]
        parts.append(oh * lax.rsqrt(jnp.mean(oh * oh, axis=-1, keepdims=True) + EPS))
    gh = gh_ref[...]
    ob = jnp.concatenate(parts, axis=1) * hg_ref[...] * (gh * _sigmoid(gh))
    ya = _dot(oa_ref[...], wa_ref[...])
    yb = _dot(ob.astype(BF16), wb_ref[...])
    mix = _sigmoid(ga_ref[...]) * ya + _sigmoid(gb_ref[...]) * yb
    x1 = x_ref[...] + mod(2) * _dot(mix.astype(BF16), wo_ref[...])
    x1_ref[...] = x1

    h2 = x1 * lax.rsqrt(jnp.mean(x1 * x1, axis=-1, keepdims=True) + EPS) * n2_ref[...]
    h2 = h2 * (1.0 + mod(4)) + mod(3)
    _store_row_tiles(h2_ref, h2)

    h_hi = h2.astype(BF16)
    h_lo = (h2 - h_hi.astype(F32)).astype(BF16)
    rw = rw_ref[...]
    r_hi = rw.astype(BF16)
    r_lo = (rw - r_hi.astype(F32)).astype(BF16)
    logits = _dot(h_hi, r_hi) + (_dot(h_hi, r_lo) + _dot(h_lo, r_hi)) + rb_ref[...]

    lane = lax.broadcasted_iota(jnp.int32, logits.shape, 1)
    work = jnp.where(lane < N_EXPERTS, logits, -jnp.inf)
    sel_any = jnp.zeros(logits.shape, F32)
    vals, idxs = [], []
    for _ in range(TOP_K):
        mx = jnp.max(work, axis=-1, keepdims=True)
        idx = jnp.min(jnp.where(work == mx, lane, 2 * N_EXPERTS), axis=-1, keepdims=True)
        hit = lane == idx
        vals.append(mx)
        idxs.append(idx)
        sel_any = jnp.where(hit, 1.0, sel_any)
        work = jnp.where(hit, -jnp.inf, work)
    es = [jnp.exp(vk - vals[0]) for vk in vals]
    denom = es[0] + es[1] + es[2] + es[3]

    ri = lax.broadcasted_iota(jnp.int32, (tm, tm), 0)
    ci = lax.broadcasted_iota(jnp.int32, (tm, tm), 1)
    below = jnp.where(ci < ri, 1.0, 0.0).astype(BF16)
    rank_e = run_scr[0:1, :] + _dot(below, sel_any.astype(BF16))
    gw = jnp.zeros(logits.shape, F32)
    gi = jnp.zeros(logits.shape, jnp.int32)
    rk = jnp.zeros(logits.shape, jnp.int32)
    for k in range(TOP_K):
        at_k = lane == k
        gw = jnp.where(at_k, es[k] / denom, gw)
        gi = jnp.where(at_k, idxs[k], gi)
        rnk = jnp.sum(jnp.where(lane == idxs[k], rank_e, 0.0), axis=-1, keepdims=True)
        rk = jnp.where(at_k, rnk.astype(jnp.int32), rk)
    gw_ref[...] = gw
    gi_ref[...] = gi.T[0:8, :]
    rk_ref[...] = rk.T[0:8, :]
    run = run_scr[0:1, :] + jnp.sum(sel_any, axis=0, keepdims=True)
    run_scr[...] = jnp.broadcast_to(run, run_scr.shape)
    cnt_ref[...] = jnp.broadcast_to(run, cnt_ref.shape)


def _post(x, oa, o_f, o_b, p, mod, hg_g, n2_g, wa, wb, wo, rw_pad, rb_pad, n_lat):
    t, d = x.shape
    tm = ROW_TILE

    def rows(width, colblk=0):
        return pl.BlockSpec((tm, width), lambda i: (i, colblk))

    def whole(a):
        return pl.BlockSpec(a.shape, lambda i: (0,) * a.ndim)

    return pl.pallas_call(
        functools.partial(_post_kernel, n_lat=n_lat, tm=tm),
        grid=(t // tm,),
        in_specs=[rows(d), rows(DA_W), rows(HG_W), rows(HG_W),
                  rows(HG_W, 7), rows(d, 4), rows(d, 5),
                  whole(mod), whole(hg_g), whole(n2_g), whole(wa), whole(wb), whole(wo),
                  whole(rw_pad), whole(rb_pad)],
        out_specs=[rows(d), pl.BlockSpec((tm, d // 128, 128), lambda i: (i, 0, 0)), rows(128),
                   pl.BlockSpec((8, tm), lambda i: (0, i)), pl.BlockSpec((8, tm), lambda i: (0, i)),
                   pl.BlockSpec((8, 128), lambda i: (0, 0))],
        out_shape=[jax.ShapeDtypeStruct((t, d), F32), jax.ShapeDtypeStruct((t, d // 128, 128), F32),
                   jax.ShapeDtypeStruct((t, 128), F32), jax.ShapeDtypeStruct((8, t), jnp.int32),
                   jax.ShapeDtypeStruct((8, t), jnp.int32), jax.ShapeDtypeStruct((8, 128), F32)],
        scratch_shapes=[pltpu.VMEM((8, 128), F32)],
        compiler_params=_cparams(("arbitrary",)),
        name="merge_router",
    )(x, oa, o_f, o_b, p, p, p, mod, hg_g, n2_g, wa, wb, wo, rw_pad, rb_pad)


def _dispatch_kernel(dest_ref, h_ref, xs_in, xs_out, sem):
    del xs_in
    n = dest_ref.shape[0]
    tm = h_ref.shape[0]

    for k in range(n // tm):
        def start(j, c, k=k):
            for q in range(DMA_QUEUES):
                tok = DMA_QUEUES * j + q
                pltpu.make_async_copy(h_ref.at[tok], xs_out.at[dest_ref[k * tm + tok]], sem).start(priority=q)
            return c

        lax.fori_loop(0, tm // DMA_QUEUES, start, 0, unroll=8)
    pltpu.make_async_copy(xs_out.at[pl.ds(0, n)], xs_out.at[pl.ds(0, n)], sem).wait()


def _dispatch(h2, dest_tiles, xs_zero):
    t = h2.shape[0]
    tm = ROW_TILE
    return pl.pallas_call(
        _dispatch_kernel,
        grid=(t // tm,),
        in_specs=[pl.BlockSpec((tm * TOP_K,), lambda i: (i,), memory_space=pltpu.SMEM),
                  pl.BlockSpec((tm,) + h2.shape[1:], lambda i: (i, 0, 0)),
                  pl.BlockSpec(memory_space=pl.ANY)],
        out_specs=pl.BlockSpec(memory_space=pl.ANY),
        out_shape=jax.ShapeDtypeStruct(xs_zero.shape, xs_zero.dtype),
        scratch_shapes=[pltpu.SemaphoreType.DMA(())],
        input_output_aliases={2: 0},
        compiler_params=_cparams(("arbitrary",)),
        name="moe_dispatch",
    )(dest_tiles, h2, xs_zero)


def _expert_kernel(be_ref, nx_ref, sl_ref, nu_ref, xs_ref, wgu_hbm, bgu_ref, wd_hbm, bd_ref, ys_ref,
                   wgu_f32, wd_f32, wgu_bf, wd_bf, sems, *, layer):
    i = pl.program_id(0)
    used = i < nu_ref[0]

    def weight_copies(e, slot):
        return (pltpu.make_async_copy(wgu_hbm.at[layer, e], wgu_f32.at[slot], sems.at[0, slot]),
                pltpu.make_async_copy(wd_hbm.at[layer, e], wd_f32.at[slot], sems.at[1, slot]))

    @pl.when(i == 0)
    def _():
        for cp in weight_copies(be_ref[0], 0):
            cp.start()

    @pl.when(jnp.logical_and(used, jnp.logical_or(i == 0, be_ref[i] != be_ref[jnp.maximum(i - 1, 0)])))
    def _():
        slot = sl_ref[i]
        for cp in weight_copies(be_ref[i], slot):
            cp.wait()

        @pl.when(nx_ref[i] >= 0)
        def _():
            for cp in weight_copies(nx_ref[i], 1 - slot):
                cp.start()

        wgu_bf[...] = wgu_f32[slot].astype(BF16)
        wd_bf[...] = wd_f32[slot].astype(BF16)

    @pl.when(used)
    def _():
        x = _load_row_tiles(xs_ref, 0, xs_ref.shape[0]).astype(BF16)
        gu = _dot(x, wgu_bf[...]) + bgu_ref[0, 0]
        gate = jnp.minimum(gu[:, :D_FF], SWIGLU_LIMIT)
        up = jnp.clip(gu[:, D_FF:], -SWIGLU_LIMIT, SWIGLU_LIMIT)
        hdn = (up + 1.0) * gate * _sigmoid(SWIGLU_ALPHA * gate)
        _store_row_tiles(ys_ref, _dot(hdn.astype(BF16), wd_bf[...]) + bd_ref[0, 0])

    @pl.when(jnp.logical_not(used))
    def _():
        ys_ref[...] = jnp.zeros(ys_ref.shape, F32)


def _experts(xs, plan, wgu, bgu, wd, bd, layer):
    blk_e, next_e, slot, n_used = plan
    n_rows = xs.shape[0]
    d = xs.shape[1] * xs.shape[2]
    nblk = n_rows // MOE_BLOCK

    def blk(i, nu):
        return jnp.minimum(i, nu[0] - 1)

    def bias(i, be, nx, sl, nu):
        return (layer, be[blk(i, nu)], 0, 0)

    grid_spec = pltpu.PrefetchScalarGridSpec(
        num_scalar_prefetch=4,
        grid=(nblk,),
        in_specs=[
            pl.BlockSpec((MOE_BLOCK,) + xs.shape[1:], lambda i, be, nx, sl, nu: (blk(i, nu), 0, 0)),
            pl.BlockSpec(memory_space=pl.ANY),
            pl.BlockSpec((1, 1, 1, 2 * D_FF), bias),
            pl.BlockSpec(memory_space=pl.ANY),
            pl.BlockSpec((1, 1, 1, d), bias),
        ],
        out_specs=pl.BlockSpec((MOE_BLOCK,) + xs.shape[1:], lambda i, be, nx, sl, nu: (i, 0, 0)),
        scratch_shapes=[pltpu.VMEM((2, d, 2 * D_FF), F32), pltpu.VMEM((2, D_FF, d), F32),
                        pltpu.VMEM((d, 2 * D_FF), BF16), pltpu.VMEM((D_FF, d), BF16),
                        pltpu.SemaphoreType.DMA((2, 2))],
    )
    return pl.pallas_call(
        functools.partial(_expert_kernel, layer=layer),
        grid_spec=grid_spec,
        out_shape=jax.ShapeDtypeStruct(xs.shape, F32),
        compiler_params=_cparams(("arbitrary",)),
        name="moe_experts",
    )(blk_e, next_e, slot, n_used, xs, wgu, bgu, wd, bd)


def _combine_kernel(dest_ref, x1_ref, gw_ref, mod_ref, ys_ref, o_ref, buf, sem, *, n_lat, tm):
    i = pl.program_id(0)
    d = x1_ref.shape[1]
    n = dest_ref.shape[0]

    def start(j, c):
        for q in range(DMA_QUEUES):
            r = DMA_QUEUES * j + q
            pltpu.make_async_copy(ys_ref.at[dest_ref[r]], buf.at[r], sem).start(priority=q)
        return c

    lax.fori_loop(0, n // DMA_QUEUES, start, 0, unroll=8)
    pltpu.make_async_copy(ys_ref.at[pl.ds(0, n)], buf, sem).wait()

    lat = i * tm < n_lat
    g2 = jnp.where(lat, mod_ref[0:1, 5 * d:6 * d], mod_ref[1:2, 5 * d:6 * d])
    gw = gw_ref[...]
    m = gw[:, 0:1] * _load_row_tiles(buf, 0, tm)
    for k in range(1, TOP_K):
        m = m + gw[:, k:k + 1] * _load_row_tiles(buf, k * tm, tm)
    o_ref[...] = x1_ref[...] + g2 * m


def _combine(x1, gw, mod, ys, dest_tiles, n_lat):
    t, d = x1.shape
    tm = ROW_TILE
    return pl.pallas_call(
        functools.partial(_combine_kernel, n_lat=n_lat, tm=tm),
        grid=(t // tm,),
        in_specs=[pl.BlockSpec((tm * TOP_K,), lambda i: (i,), memory_space=pltpu.SMEM),
                  pl.BlockSpec((tm, d), lambda i: (i, 0)),
                  pl.BlockSpec((tm, 128), lambda i: (i, 0)),
                  pl.BlockSpec(mod.shape, lambda i: (0, 0)),
                  pl.BlockSpec(memory_space=pl.ANY)],
        out_specs=pl.BlockSpec((tm, d), lambda i: (i, 0)),
        out_shape=jax.ShapeDtypeStruct((t, d), F32),
        scratch_shapes=[pltpu.VMEM((tm * TOP_K,) + ys.shape[1:], F32), pltpu.SemaphoreType.DMA(())],
        compiler_params=_cparams(("arbitrary",)),
        name="moe_combine",
    )(dest_tiles, x1, gw, mod, ys)


def _moe_plan(gi, rk, counts):
    t = gi.shape[1]
    top_i = gi[:TOP_K]
    rank = rk[:TOP_K]
    cnt = counts[0, :N_EXPERTS].astype(jnp.int32)
    padded = (cnt + MOE_BLOCK - 1) // MOE_BLOCK * MOE_BLOCK
    pad_ends = jnp.cumsum(padded)
    pad_starts = pad_ends - padded
    experts = jnp.arange(N_EXPERTS, dtype=jnp.int32)[:, None, None]
    dest = jnp.sum(jnp.where(top_i[None] == experts, pad_starts[:, None, None], 0), axis=0) + rank
    n_rows = t * TOP_K + N_EXPERTS * MOE_BLOCK
    blk_start = jnp.arange(n_rows // MOE_BLOCK, dtype=jnp.int32) * MOE_BLOCK
    blk_e = jnp.minimum(jnp.sum(blk_start[:, None] >= pad_ends[None, :], axis=1), N_EXPERTS - 1).astype(jnp.int32)
    n_used = (pad_ends[-1:] // MOE_BLOCK).astype(jnp.int32)
    has_rows = cnt > 0
    later = jnp.where(has_rows[None, :] & (jnp.arange(N_EXPERTS)[None, :] > jnp.arange(N_EXPERTS)[:, None]),
                      jnp.arange(N_EXPERTS)[None, :], N_EXPERTS)
    next_of = jnp.min(later, axis=1)
    next_of = jnp.where(next_of < N_EXPERTS, next_of, -1).astype(jnp.int32)
    run_of = (jnp.cumsum(has_rows.astype(jnp.int32)) - 1).astype(jnp.int32)
    next_e = next_of[blk_e]
    slot = run_of[blk_e] & 1
    dest_tiles = dest.reshape(TOP_K, t // ROW_TILE, ROW_TILE).transpose(1, 0, 2).reshape(-1).astype(jnp.int32)
    return dest_tiles, (blk_e, next_e, slot, n_used), n_rows


def _rope_tables(n_lat, n_ctx):
    pos = jnp.arange(n_lat)
    row = (pos // GRID_W).astype(F32)
    col = (pos % GRID_W).astype(F32)
    freq = ROPE_BASE ** (-jnp.arange(ROPE_PAIRS, dtype=F32) / ROPE_PAIRS)
    ra = row[:, None] * freq
    ca = col[:, None] * freq
    cos64 = jnp.concatenate([jnp.cos(ra), jnp.cos(ra), jnp.cos(ca), jnp.cos(ca)], axis=1)
    sin64 = jnp.concatenate([-jnp.sin(ra), jnp.sin(ra), -jnp.sin(ca), jnp.sin(ca)], axis=1)
    cos_t = jnp.concatenate([jnp.tile(cos64, (1, DA_W // DA_DH)), jnp.ones((n_ctx, DA_W), F32)], axis=0)
    sin_t = jnp.concatenate([jnp.tile(sin64, (1, DA_W // DA_DH)), jnp.zeros((n_ctx, DA_W), F32)], axis=0)
    return cos_t, sin_t


def kernel(x, c, ctx, c_ctx, ada_w, ada_b, norm1_g, norm2_g, w_in, qn_g, kn_g, lam_qk, subln_g, hg_lb,
           hg_norm_g, w_branch_a, w_branch_b, w_out, router_w, router_b, w_gu, b_gu, w_down, b_down):
    bsz, n_lat, d = x.shape
    n_ctx = ctx.shape[1]
    depth = ada_w.shape[0]
    assert bsz == 1 and d == D_MODEL and n_lat % 512 == 0 and n_ctx % ROW_TILE == 0
    t = n_lat + n_ctx

    xx = jnp.concatenate([x[0], ctx[0]], axis=0)
    cc = jnp.zeros((8, d), F32).at[0].set(c[0]).at[1].set(c_ctx)
    mods = _modulation(cc, ada_w, ada_b)

    cos_t, sin_t = _rope_tables(n_lat, n_ctx)
    lane = jnp.arange(DA_W)
    seg64 = (lane[:, None] // DA_DH == lane[None, :] // DA_DH).astype(BF16)
    cs = jnp.cumsum(jax.nn.softmax(hg_lb.astype(F32), axis=1), axis=1)
    lb_all = cs - cs[:, :1]
    rw_pad = jnp.zeros((depth, d, 128), F32).at[:, :, :N_EXPERTS].set(router_w)
    rb_pad = jnp.zeros((depth, 1, 128), F32).at[:, 0, :N_EXPERTS].set(router_b)

    key_chunk = next(c for c in ATT_KEY_CHUNKS if t % c == 0 and n_ctx <= c)
    xs = None
    for l in range(depth):
        lam_init = 0.8 - 0.6 * math.exp(-0.3 * l)
        mod = mods[l]
        p = _inproj(xx, norm1_g[l][None], mod, w_in[l].astype(BF16), n_lat)
        k_r, q_t, v_t3, qn2, kn2_tiles = _prep(p, cos_t, sin_t, seg64, jnp.tile(qn_g[l], DA_W // DA_DH)[None],
                                               jnp.tile(kn_g[l], DA_W // DA_DH)[None], key_chunk)
        oa = _attention_all(q_t, k_r, v_t3, qn2, kn2_tiles, lam_qk[l], subln_g[l][:, None], lam_init,
                            n_lat, n_ctx)
        o_f, o_b = _hgrn(p, lb_all[0, l][None], lb_all[1, l][None], n_lat)
        x1, h2, gw, gi, rk, counts = _post(
            xx, oa, o_f, o_b, p, mod, jnp.tile(hg_norm_g[l], HG_HEADS)[None], norm2_g[l][None],
            w_branch_a[l].astype(BF16), w_branch_b[l].astype(BF16), w_out[l].astype(BF16),
            rw_pad[l], rb_pad[l], n_lat)
        dest_tiles, plan, n_rows = _moe_plan(gi, rk, counts)
        xs = _dispatch(h2, dest_tiles, jnp.zeros((n_rows,) + h2.shape[1:], F32) if xs is None else xs)
        ys = _experts(xs, plan, w_gu, b_gu[:, :, None, :], w_down, b_down[:, :, None, :], l)
        xx = _combine(x1, gw, mod, ys, dest_tiles, n_lat)
    return xx[:n_lat][None]
```

```python
import functools
import math

import jax
import jax.numpy as jnp
from jax import lax
from jax.experimental import pallas as pl
from jax.experimental.pallas import tpu as pltpu

F32 = jnp.float32
BF16 = jnp.bfloat16
QK_DTYPE = jnp.float8_e4m3fn

D_MODEL = 1024
GRID_W = 64
EPS = 1e-6

DA_HEADS = 4
DA_DH = 64
DA_DV = 2 * DA_DH
DA_W = DA_HEADS * DA_DV
DA_SCALE = DA_DH ** -0.5
ROPE_PAIRS = DA_DH // 4
ROPE_BASE = 10000.0

HG_HEADS = 4
HG_DK = 128
HG_W = HG_HEADS * HG_DK
HG_CHUNK = 128

N_EXPERTS = 32
TOP_K = 4
D_FF = 1024
SWIGLU_ALPHA = 1.702
SWIGLU_LIMIT = 7.0
MOE_BLOCK = 256

ROW_TILE = 256
NEG_BIG = -1e30
LOG2E = 1.4426950408889634

VMEM_LIMIT = 56 * 1024 * 1024
ATT_TQ = 256
ATT_KEY_CHUNKS = (3328, 1280, 256)
DMA_QUEUES = 2
SCORE_BOUND_MAX = 40.0
SCORE_BOUND_SLACK = 1.01


def _cparams(sem):
    return pltpu.CompilerParams(dimension_semantics=sem, vmem_limit_bytes=VMEM_LIMIT)


def _split3(x):
    a = x.astype(BF16)
    r = x - a.astype(F32)
    b = r.astype(BF16)
    c = (r - b.astype(F32)).astype(BF16)
    return a, b, c


def _dot(a, b):
    return jnp.dot(a, b, preferred_element_type=F32)


def _dot_nt(a, b):
    return lax.dot_general(a, b, (((1,), (1,)), ((), ())), preferred_element_type=F32)


def _dot_tn(a, b):
    return lax.dot_general(a, b, (((0,), (0,)), ((), ())), preferred_element_type=F32)


def _sigmoid(x):
    return 0.5 * jnp.tanh(0.5 * x) + 0.5


def _tile_transpose(x):
    g = x.shape[0]
    sub = lax.broadcasted_iota(jnp.int32, (1, 1, 1, 1, 8, 128), 4)
    x = x.reshape(g, 2, 2, 2, 8, 128)
    for axis, k in ((1, 4), (2, 2), (3, 1)):
        lo = (sub & k) == 0
        a = lax.index_in_dim(x, 0, axis, keepdims=True)
        b = lax.index_in_dim(x, 1, axis, keepdims=True)
        a2 = jnp.where(lo, a, pltpu.roll(b, k, 4))
        b2 = jnp.where(lo, pltpu.roll(a, 8 - k, 4), b)
        x = jnp.concatenate([a2, b2], axis=axis)
    return x.reshape(g, 8, 8, 128)


def _store_row_tiles(ref, val):
    n = val.shape[0]
    z = jnp.stack([val[:, s * 128:(s + 1) * 128].reshape(n // 8, 8, 128) for s in range(8)], axis=1)
    ref[...] = _tile_transpose(z).reshape(n, 8, 128)


def _load_row_tiles(ref, lo, n):
    y = _tile_transpose(ref[lo:lo + n].reshape(n // 8, 8, 8, 128))
    return jnp.concatenate([y[:, s].reshape(n, 128) for s in range(8)], axis=1)


def _mod_kernel(c_ref, w_ref, b_ref, o_ref):
    cv = c_ref[...]
    a = cv * _sigmoid(cv)
    w = w_ref[0]
    a1, a2, a3 = _split3(a)
    w1, w2, w3 = _split3(w)
    acc = _dot(a1, w1) + (_dot(a1, w2) + _dot(a2, w1)) + (_dot(a2, w2) + _dot(a1, w3) + _dot(a3, w1))
    o_ref[0] = acc + b_ref[0]


def _modulation(cc, ada_w, ada_b):
    depth, d, n = ada_w.shape
    tn = 1536
    return pl.pallas_call(
        _mod_kernel,
        grid=(depth, n // tn),
        in_specs=[
            pl.BlockSpec((8, d), lambda l, j: (0, 0)),
            pl.BlockSpec((1, d, tn), lambda l, j: (l, 0, j)),
            pl.BlockSpec((1, 1, tn), lambda l, j: (l, 0, j)),
        ],
        out_specs=pl.BlockSpec((1, 8, tn), lambda l, j: (l, 0, j)),
        out_shape=jax.ShapeDtypeStruct((depth, 8, n), F32),
        compiler_params=_cparams(("parallel", "parallel")),
        name="adaln_mod",
    )(cc, ada_w, ada_b.reshape(depth, 1, n))


def _inproj_kernel(x_ref, g_ref, mod_ref, w_ref, o_ref, h_scr, *, n_lat, tm):
    i = pl.program_id(0)
    j = pl.program_id(1)
    d = x_ref.shape[1]

    @pl.when(j == 0)
    def _():
        x = x_ref[...]
        y = x * lax.rsqrt(jnp.mean(x * x, axis=-1, keepdims=True) + EPS) * g_ref[...]
        row = i * tm + lax.broadcasted_iota(jnp.int32, (tm, 1), 0)
        lat = row < n_lat
        sh = jnp.where(lat, mod_ref[0:1, 0:d], mod_ref[1:2, 0:d])
        sc = jnp.where(lat, mod_ref[0:1, d:2 * d], mod_ref[1:2, d:2 * d])
        h_scr[...] = (y * (1.0 + sc) + sh).astype(BF16)

    o_ref[...] = _dot(h_scr[...], w_ref[...])


def _inproj(x, g, mod, w_bf, n_lat):
    t, d = x.shape
    n = w_bf.shape[1]
    tm = 1280 if t % 1280 == 0 else ROW_TILE
    tn = 1536
    return pl.pallas_call(
        functools.partial(_inproj_kernel, n_lat=n_lat, tm=tm),
        grid=(t // tm, n // tn),
        in_specs=[
            pl.BlockSpec((tm, d), lambda i, j: (i, 0)),
            pl.BlockSpec((1, d), lambda i, j: (0, 0)),
            pl.BlockSpec((8, mod.shape[1]), lambda i, j: (0, 0)),
            pl.BlockSpec((d, tn), lambda i, j: (0, j)),
        ],
        out_specs=pl.BlockSpec((tm, tn), lambda i, j: (i, j)),
        out_shape=jax.ShapeDtypeStruct((t, n), F32),
        scratch_shapes=[pltpu.VMEM((tm, d), BF16)],
        compiler_params=_cparams(("parallel", "arbitrary")),
        name="inproj",
    )(x, g, mod, w_bf)


def _segment_mean_sq(x, seg_ref, width):
    x2 = x * x
    hi = x2.astype(BF16)
    lo = (x2 - hi.astype(F32)).astype(BF16)
    seg = seg_ref[...]
    return (_dot(hi, seg) + _dot(lo, seg)) * (1.0 / width)


def _prep_kernel(q_ref, k_ref, v_ref, cos_ref, sin_ref, seg_ref, qg_ref, kg_ref,
                 k_out, qt_out, vt_out, qn_out, kn_out):
    tm = q_ref.shape[0]
    cos = cos_ref[...]
    sin = sin_ref[...]
    lane = lax.broadcasted_iota(jnp.int32, (tm, DA_W), 1)
    first_half = (lane & 31) < 16

    def norm_rope(x, g):
        y = x * lax.rsqrt(_segment_mean_sq(x, seg_ref, DA_DH) + EPS) * g
        fwd = pltpu.roll(y, DA_W - 16, 1)
        bwd = pltpu.roll(y, 16, 1)
        partner = jnp.where(first_half, fwd, bwd)
        return y * cos + partner * sin

    k_b = norm_rope(k_ref[...], kg_ref[...]).astype(QK_DTYPE)
    k_out[...] = k_b
    k_f = k_b.astype(F32)
    k_n2 = _dot((k_f * k_f).astype(BF16), seg_ref[...])
    kn_out[...] = jnp.broadcast_to(jnp.max(k_n2, axis=0, keepdims=True), kn_out.shape)

    q = norm_rope(q_ref[...], qg_ref[...]) * (DA_SCALE * LOG2E)
    lane_h = lax.broadcasted_iota(jnp.int32, (tm, DA_DV), 1)
    v = v_ref[...]
    for h in range(DA_HEADS):
        qh = q[:, h * DA_DV:(h + 1) * DA_DV]
        for m in range(2):
            keep = (lane_h < DA_DH) if m == 0 else (lane_h >= DA_DH)
            r = 2 * h + m
            qt_b = jnp.where(keep, qh, 0.0).T.astype(QK_DTYPE)
            qt_out[r * DA_DV:(r + 1) * DA_DV, :] = qt_b
            qt_f = qt_b.astype(F32)
            qn_out[r:r + 1, :] = jnp.sum(qt_f * qt_f, axis=0, keepdims=True)
        vt_out[0, h * DA_DV:(h + 1) * DA_DV, :] = v[:, h * DA_DV:(h + 1) * DA_DV].T.astype(BF16)


def _prep(p, cos_t, sin_t, seg64, qg, kg, chunk):
    t = p.shape[0]
    tm = ROW_TILE
    per_chunk = chunk // tm
    return pl.pallas_call(
        _prep_kernel,
        grid=(t // tm,),
        in_specs=[
            pl.BlockSpec((tm, DA_W), lambda i: (i, 0)),
            pl.BlockSpec((tm, DA_W), lambda i: (i, 1)),
            pl.BlockSpec((tm, DA_W), lambda i: (i, 2)),
            pl.BlockSpec((tm, DA_W), lambda i: (i, 0)),
            pl.BlockSpec((tm, DA_W), lambda i: (i, 0)),
            pl.BlockSpec((DA_W, DA_W), lambda i: (0, 0)),
            pl.BlockSpec((1, DA_W), lambda i: (0, 0)),
            pl.BlockSpec((1, DA_W), lambda i: (0, 0)),
        ],
        out_specs=[
            pl.BlockSpec((tm, DA_W), lambda i: (i, 0)),
            pl.BlockSpec((2 * DA_HEADS * DA_DV, tm), lambda i: (0, i)),
            pl.BlockSpec((1, DA_W, tm), lambda i: (i // per_chunk, 0, i % per_chunk)),
            pl.BlockSpec((2 * DA_HEADS, tm), lambda i: (0, i)),
            pl.BlockSpec((8, DA_W), lambda i: (i, 0)),
        ],
        out_shape=[
            jax.ShapeDtypeStruct((t, DA_W), QK_DTYPE),
            jax.ShapeDtypeStruct((2 * DA_HEADS * DA_DV, t), QK_DTYPE),
            jax.ShapeDtypeStruct((t // chunk, DA_W, chunk), BF16),
            jax.ShapeDtypeStruct((2 * DA_HEADS, t), F32),
            jax.ShapeDtypeStruct((8 * (t // tm), DA_W), F32),
        ],
        compiler_params=_cparams(("parallel",)),
        name="qkv_prep",
    )(p, p, p, cos_t, sin_t, seg64, qg, kg)


def _attn_finalize(lam_ref, sg_ref, o_ref, acc_scr, l_scr, lam_init):
    lq = lam_ref[...]
    lam = (jnp.exp(jnp.sum(lq[0:1] * lq[1:2], axis=-1, keepdims=True))
           - jnp.exp(jnp.sum(lq[2:3] * lq[3:4], axis=-1, keepdims=True)) + lam_init)
    for h in range(DA_HEADS):
        r = 2 * h
        o = acc_scr[r] / l_scr[r:r + 1, :] - lam * (acc_scr[r + 1] / l_scr[r + 1:r + 2, :])
        o = o * lax.rsqrt(jnp.mean(o * o, axis=0, keepdims=True) + EPS)
        o = o * sg_ref[...] * (1.0 - lam_init)
        o_ref[:, h * DA_DV:(h + 1) * DA_DV] = o.T.astype(o_ref.dtype)


def _attn_kernel(qt_ref, k_hbm, vt_hbm, lam_ref, sg_ref, o_ref, k_scr, vt_scr, m_scr, l_scr, acc_scr, sems,
                 *, lam_init, stabilised):
    @pl.when(pl.program_id(0) == 0)
    def _():
        copies = (pltpu.make_async_copy(k_hbm, k_scr, sems.at[0]), pltpu.make_async_copy(vt_hbm, vt_scr, sems.at[1]))
        for cp in copies:
            cp.start()
        for cp in copies:
            cp.wait()

    acc_scr[...] = jnp.zeros(acc_scr.shape, F32)
    l_scr[...] = jnp.zeros(l_scr.shape, F32)
    if stabilised:
        m_scr[...] = jnp.full(m_scr.shape, NEG_BIG, F32)

    def chunk(c, carry):
        for h in range(DA_HEADS):
            k_h = k_scr[c, :, h * DA_DV:(h + 1) * DA_DV]
            vt_h = vt_scr[c, h * DA_DV:(h + 1) * DA_DV, :]
            for m in range(2):
                r = 2 * h + m
                s = _dot(k_h, qt_ref[r * DA_DV:(r + 1) * DA_DV, :])
                if stabilised:
                    m_prev = m_scr[r:r + 1, :]
                    m_new = jnp.maximum(m_prev, jnp.max(s, axis=0, keepdims=True))
                    alpha = jnp.exp2(m_prev - m_new)
                    p = jnp.exp2((s - m_new).astype(BF16))
                    acc_scr[r] = alpha * acc_scr[r] + _dot(vt_h, p)
                    l_scr[r:r + 1, :] = alpha * l_scr[r:r + 1, :] + jnp.sum(p.astype(F32), axis=0, keepdims=True)
                    m_scr[r:r + 1, :] = m_new
                else:
                    p = jnp.exp2(s.astype(BF16))
                    acc_scr[r] += _dot(vt_h, p)
                    l_scr[r:r + 1, :] += jnp.sum(p.astype(F32), axis=0, keepdims=True)
        return carry

    lax.fori_loop(0, k_scr.shape[0], chunk, 0)
    _attn_finalize(lam_ref, sg_ref, o_ref, acc_scr, l_scr, lam_init)


def _attention(qt, k3, vt3, lam_qk, subln_col, lam_init, stabilised, *, q_start, n_q, tq):
    qo = q_start // tq
    return pl.pallas_call(
        functools.partial(_attn_kernel, lam_init=lam_init, stabilised=stabilised),
        grid=(n_q // tq,),
        in_specs=[
            pl.BlockSpec((qt.shape[0], tq), lambda i: (0, i + qo)),
            pl.BlockSpec(memory_space=pl.ANY),
            pl.BlockSpec(memory_space=pl.ANY),
            pl.BlockSpec((4, DA_DH), lambda i: (0, 0)),
            pl.BlockSpec((DA_DV, 1), lambda i: (0, 0)),
        ],
        out_specs=pl.BlockSpec((tq, DA_W), lambda i: (i, 0)),
        out_shape=jax.ShapeDtypeStruct((n_q, DA_W), BF16),
        scratch_shapes=[
            pltpu.VMEM(k3.shape, k3.dtype),
            pltpu.VMEM(vt3.shape, vt3.dtype),
            pltpu.VMEM((2 * DA_HEADS, tq), F32),
            pltpu.VMEM((2 * DA_HEADS, tq), F32),
            pltpu.VMEM((2 * DA_HEADS, DA_DV, tq), F32),
            pltpu.SemaphoreType.DMA((2,)),
        ],
        compiler_params=_cparams(("arbitrary",)),
        name="diff_attn_online" if stabilised else "diff_attn_plain",
    )(qt, k3, vt3, lam_qk, subln_col)


def _attention_all(qt, k, vt3, qn2, kn2_tiles, lam_qk, subln_col, lam_init, n_lat, n_ctx):
    n_chunks, _, chunk = vt3.shape
    n_maps = 2 * DA_HEADS
    kmax2 = jnp.max(kn2_tiles, axis=0).reshape(n_maps, DA_DH)[:, 0]
    bound = jnp.sqrt(jnp.max(qn2, axis=1)) * jnp.sqrt(kmax2) * SCORE_BOUND_SLACK
    bounded = jnp.max(bound) <= SCORE_BOUND_MAX
    k3 = k.reshape(n_chunks, chunk, DA_W)

    def run(stabilised):
        def f(args):
            qt_, k3_, vt3_, lam_, sg_ = args
            lat = _attention(qt_, k3_, vt3_, lam_, sg_, lam_init, stabilised, q_start=0, n_q=n_lat, tq=ATT_TQ)
            ctx = _attention(qt_, k3_[n_chunks - 1:, chunk - n_ctx:, :], vt3_[n_chunks - 1:, :, chunk - n_ctx:],
                             lam_, sg_, lam_init, stabilised, q_start=n_lat, n_q=n_ctx, tq=ATT_TQ)
            return jnp.concatenate([lat, ctx], axis=0)
        return f

    return lax.cond(bounded, run(False), run(True), (qt, k3, vt3, lam_qk, subln_col))


def _hg_chunk(q, z, v, lb, st_ref, rev):
    c, w = q.shape
    row = lax.broadcasted_iota(jnp.int32, (c, 1), 0)

    u = jnp.exp(-jnp.abs(z))
    sig_abs = 1.0 / (1.0 + u)
    y = jnp.log(1.0 - lb) + jnp.minimum(z, 0.0) - jnp.log(1.0 + u)
    log_lb = jnp.log(lb)
    log_f = jnp.maximum(log_lb, y) + jnp.log(1.0 + jnp.exp(-jnp.abs(log_lb - y)))
    key = (1.0 - lb) * jnp.where(z >= 0.0, u * sig_abs, sig_abs)

    ri = lax.broadcasted_iota(jnp.int32, (c, c), 0)
    ci = lax.broadcasted_iota(jnp.int32, (c, c), 1)
    tri = jnp.where((ci >= ri) if rev else (ci <= ri), 1.0, 0.0).astype(BF16)
    g1, g2, g3 = _split3(log_f)
    b = _dot(tri, g1) + _dot(tri, g2) + _dot(tri, g3)
    end = 0 if rev else c - 1
    b_end = b[end:end + 1, :]

    q_in = (q * jnp.exp(b)).astype(BF16)
    k_out = (key * jnp.exp(b_end - b)).astype(BF16)
    decay = jnp.exp(b_end)

    up = pltpu.roll(log_f, 1, 0)
    dn = pltpu.roll(log_f, c - 1, 0)
    q_lv, k_lv, masks = [], [], []
    lvl = c // 2
    while lvl >= 1:
        half = (row // lvl) & 1
        q_rows = (half == 0) if rev else (half == 1)
        if lvl >= 4:
            pos = lvl if rev else lvl - 1
            ref = jnp.broadcast_to(b.reshape(c // (2 * lvl), 2 * lvl, w)[:, pos:pos + 1, :],
                                   (c // (2 * lvl), 2 * lvl, w)).reshape(c, w)
            eq, ek = b - ref, ref - b
        elif lvl == 2:
            j = row & 3
            if rev:
                eq = jnp.where(j == 1, log_f, log_f + dn)
                ek = jnp.where(j == 2, 0.0, up)
            else:
                eq = jnp.where(j == 2, log_f, log_f + up)
                ek = jnp.where(j == 1, 0.0, dn)
        else:
            eq, ek = log_f, jnp.zeros_like(log_f)
        q_lv.append((q * jnp.exp(jnp.where(q_rows, eq, NEG_BIG))).astype(BF16))
        k_lv.append((key * jnp.exp(jnp.where(q_rows, NEG_BIG, ek))).astype(BF16))
        masks.append(None if 2 * lvl == c else
                     jnp.where((ri // (2 * lvl)) == (ci // (2 * lvl)), 1.0, 0.0))
        lvl //= 2
    qk = q * key
    on_diag = ri == ci

    outs = []
    for h in range(HG_HEADS):
        sl = slice(h * HG_DK, (h + 1) * HG_DK)
        a = jnp.where(on_diag, jnp.sum(qk[:, sl], axis=-1, keepdims=True), 0.0)
        for q_l, k_l, msk in zip(q_lv, k_lv, masks):
            a_l = _dot_nt(q_l[:, sl], k_l[:, sl])
            a = a + (a_l if msk is None else a_l * msk)
        st = st_ref[h]
        v_h = v[:, sl].astype(BF16)
        outs.append(_dot_nt(q_in[:, sl], st.astype(BF16)) + _dot(a.astype(BF16), v_h))
        st_ref[h] = st * decay[:, sl] + _dot_tn(v_h, k_out[:, sl])
    return jnp.concatenate(outs, axis=1)


def _hgrn_kernel(qf_ref, zf_ref, vf_ref, qb_ref, zb_ref, vb_ref, lbf_ref, lbb_ref,
                 of_ref, ob_ref, sf_scr, sb_scr):
    @pl.when(pl.program_id(0) == 0)
    def _():
        sf_scr[...] = jnp.zeros(sf_scr.shape, F32)
        sb_scr[...] = jnp.zeros(sb_scr.shape, F32)

    n_chunks = qf_ref.shape[0] // HG_CHUNK

    def body(ci, carry):
        rf = pl.ds(pl.multiple_of(ci * HG_CHUNK, HG_CHUNK), HG_CHUNK)
        of_ref[rf, :] = _hg_chunk(qf_ref[rf, :], zf_ref[rf, :], vf_ref[rf, :], lbf_ref[...], sf_scr, False)
        rb = pl.ds(pl.multiple_of((n_chunks - 1 - ci) * HG_CHUNK, HG_CHUNK), HG_CHUNK)
        ob_ref[rb, :] = _hg_chunk(qb_ref[rb, :], zb_ref[rb, :], vb_ref[rb, :], lbb_ref[...], sb_scr, True)
        return carry

    lax.fori_loop(0, n_chunks, body, 0)


def _hgrn(p, lb_f, lb_b, n_lat):
    t = p.shape[0]
    tm = ROW_TILE
    nb = t // tm
    n_lat_b = n_lat // tm
    n_ctx_b = nb - n_lat_b

    def jf(i):
        return jnp.where(i < n_ctx_b, n_lat_b + i, i - n_ctx_b)

    def jb(i):
        return jnp.where(i < n_ctx_b, nb - 1 - i, nb - 1 - i)

    def spec(order, colblk):
        return pl.BlockSpec((tm, HG_W), lambda i: (order(i), colblk))

    return pl.pallas_call(
        _hgrn_kernel,
        grid=(nb,),
        in_specs=[spec(jf, 3), spec(jf, 4), spec(jf, 6), spec(jb, 3), spec(jb, 5), spec(jb, 6),
                  pl.BlockSpec((1, HG_W), lambda i: (0, 0)), pl.BlockSpec((1, HG_W), lambda i: (0, 0))],
        out_specs=[pl.BlockSpec((tm, HG_W), lambda i: (jf(i), 0)),
                   pl.BlockSpec((tm, HG_W), lambda i: (jb(i), 0))],
        out_shape=[jax.ShapeDtypeStruct((t, HG_W), F32), jax.ShapeDtypeStruct((t, HG_W), F32)],
        scratch_shapes=[pltpu.VMEM((HG_HEADS, HG_DK, HG_DK), F32), pltpu.VMEM((HG_HEADS, HG_DK, HG_DK), F32)],
        compiler_params=_cparams(("arbitrary",)),
        name="hgrn2",
    )(p, p, p, p, p, p, lb_f, lb_b)


def _post_kernel(x_ref, oa_ref, of_ref, ob_ref, gh_ref, ga_ref, gb_ref, mod_ref, hg_ref, n2_ref,
                 wa_ref, wb_ref, wo_ref, rw_ref, rb_ref,
                 x1_ref, h2_ref, gw_ref, gi_ref, rk_ref, cnt_ref, run_scr, *, n_lat, tm):
    i = pl.program_id(0)
    d = x_ref.shape[1]

    @pl.when(i == 0)
    def _():
        run_scr[...] = jnp.zeros(run_scr.shape, F32)

    lat = i * tm < n_lat

    def mod(c):
        return jnp.where(lat, mod_ref[0:1, c * d:(c + 1) * d], mod_ref[1:2, c * d:(c + 1) * d])

    o = of_ref[...] + ob_ref[...]
    parts = []
    for h in range(HG_HEADS):
        oh = o[:, h * HG_DK:(h + 1) * HG_DK]
        parts.append(oh * lax.rsqrt(jnp.mean(oh * oh, axis=-1, keepdims=True) + EPS))
    gh = gh_ref[...]
    ob = jnp.concatenate(parts, axis=1) * hg_ref[...] * (gh * _sigmoid(gh))
    ya = _dot(oa_ref[...], wa_ref[...])
    yb = _dot(ob.astype(BF16), wb_ref[...])
    mix = _sigmoid(ga_ref[...]) * ya + _sigmoid(gb_ref[...]) * yb
    x1 = x_ref[...] + mod(2) * _dot(mix.astype(BF16), wo_ref[...])
    x1_ref[...] = x1

    h2 = x1 * lax.rsqrt(jnp.mean(x1 * x1, axis=-1, keepdims=True) + EPS) * n2_ref[...]
    h2 = h2 * (1.0 + mod(4)) + mod(3)
    _store_row_tiles(h2_ref, h2)

    h_hi = h2.astype(BF16)
    h_lo = (h2 - h_hi.astype(F32)).astype(BF16)
    rw = rw_ref[...]
    r_hi = rw.astype(BF16)
    r_lo = (rw - r_hi.astype(F32)).astype(BF16)
    logits = _dot(h_hi, r_hi) + (_dot(h_hi, r_lo) + _dot(h_lo, r_hi)) + rb_ref[...]

    lane = lax.broadcasted_iota(jnp.int32, logits.shape, 1)
    work = jnp.where(lane < N_EXPERTS, logits, -jnp.inf)
    sel_any = jnp.zeros(logits.shape, F32)
    vals, idxs = [], []
    for _ in range(TOP_K):
        mx = jnp.max(work, axis=-1, keepdims=True)
        idx = jnp.min(jnp.where(work == mx, lane, 2 * N_EXPERTS), axis=-1, keepdims=True)
        hit = lane == idx
        vals.append(mx)
        idxs.append(idx)
        sel_any = jnp.where(hit, 1.0, sel_any)
        work = jnp.where(hit, -jnp.inf, work)
    es = [jnp.exp(vk - vals[0]) for vk in vals]
    denom = es[0] + es[1] + es[2] + es[3]

    ri = lax.broadcasted_iota(jnp.int32, (tm, tm), 0)
    ci = lax.broadcasted_iota(jnp.int32, (tm, tm), 1)
    below = jnp.where(ci < ri, 1.0, 0.0).astype(BF16)
    rank_e = run_scr[0:1, :] + _dot(below, sel_any.astype(BF16))
    gw = jnp.zeros(logits.shape, F32)
    gi = jnp.zeros(logits.shape, jnp.int32)
    rk = jnp.zeros(logits.shape, jnp.int32)
    for k in range(TOP_K):
        at_k = lane == k
        gw = jnp.where(at_k, es[k] / denom, gw)
        gi = jnp.where(at_k, idxs[k], gi)
        rnk = jnp.sum(jnp.where(lane == idxs[k], rank_e, 0.0), axis=-1, keepdims=True)
        rk = jnp.where(at_k, rnk.astype(jnp.int32), rk)
    gw_ref[...] = gw
    gi_ref[...] = gi.T[0:8, :]
    rk_ref[...] = rk.T[0:8, :]
    run = run_scr[0:1, :] + jnp.sum(sel_any, axis=0, keepdims=True)
    run_scr[...] = jnp.broadcast_to(run, run_scr.shape)
    cnt_ref[...] = jnp.broadcast_to(run, cnt_ref.shape)


def _post(x, oa, o_f, o_b, p, mod, hg_g, n2_g, wa, wb, wo, rw_pad, rb_pad, n_lat):
    t, d = x.shape
    tm = ROW_TILE

    def rows(width, colblk=0):
        return pl.BlockSpec((tm, width), lambda i: (i, colblk))

    def whole(a):
        return pl.BlockSpec(a.shape, lambda i: (0,) * a.ndim)

    return pl.pallas_call(
        functools.partial(_post_kernel, n_lat=n_lat, tm=tm),
        grid=(t // tm,),
        in_specs=[rows(d), rows(DA_W), rows(HG_W), rows(HG_W),
                  rows(HG_W, 7), rows(d, 4), rows(d, 5),
                  whole(mod), whole(hg_g), whole(n2_g), whole(wa), whole(wb), whole(wo),
                  whole(rw_pad), whole(rb_pad)],
        out_specs=[rows(d), pl.BlockSpec((tm, d // 128, 128), lambda i: (i, 0, 0)), rows(128),
                   pl.BlockSpec((8, tm), lambda i: (0, i)), pl.BlockSpec((8, tm), lambda i: (0, i)),
                   pl.BlockSpec((8, 128), lambda i: (0, 0))],
        out_shape=[jax.ShapeDtypeStruct((t, d), F32), jax.ShapeDtypeStruct((t, d // 128, 128), F32),
                   jax.ShapeDtypeStruct((t, 128), F32), jax.ShapeDtypeStruct((8, t), jnp.int32),
                   jax.ShapeDtypeStruct((8, t), jnp.int32), jax.ShapeDtypeStruct((8, 128), F32)],
        scratch_shapes=[pltpu.VMEM((8, 128), F32)],
        compiler_params=_cparams(("arbitrary",)),
        name="merge_router",
    )(x, oa, o_f, o_b, p, p, p, mod, hg_g, n2_g, wa, wb, wo, rw_pad, rb_pad)


def _dispatch_kernel(dest_ref, h_ref, xs_in, xs_out, sem):
    del xs_in
    n = dest_ref.shape[0]
    tm = h_ref.shape[0]

    for k in range(n // tm):
        def start(j, c, k=k):
            for q in range(DMA_QUEUES):
                tok = DMA_QUEUES * j + q
                pltpu.make_async_copy(h_ref.at[tok], xs_out.at[dest_ref[k * tm + tok]], sem).start(priority=q)
            return c

        lax.fori_loop(0, tm // DMA_QUEUES, start, 0, unroll=8)
    pltpu.make_async_copy(xs_out.at[pl.ds(0, n)], xs_out.at[pl.ds(0, n)], sem).wait()


def _dispatch(h2, dest_tiles, xs_zero):
    t = h2.shape[0]
    tm = ROW_TILE
    return pl.pallas_call(
        _dispatch_kernel,
        grid=(t // tm,),
        in_specs=[pl.BlockSpec((tm * TOP_K,), lambda i: (i,), memory_space=pltpu.SMEM),
                  pl.BlockSpec((tm,) + h2.shape[1:], lambda i: (i, 0, 0)),
                  pl.BlockSpec(memory_space=pl.ANY)],
        out_specs=pl.BlockSpec(memory_space=pl.ANY),
        out_shape=jax.ShapeDtypeStruct(xs_zero.shape, xs_zero.dtype),
        scratch_shapes=[pltpu.SemaphoreType.DMA(())],
        input_output_aliases={2: 0},
        compiler_params=_cparams(("arbitrary",)),
        name="moe_dispatch",
    )(dest_tiles, h2, xs_zero)


def _expert_kernel(be_ref, nx_ref, sl_ref, nu_ref, xs_ref, wgu_hbm, bgu_ref, wd_hbm, bd_ref, ys_ref,
                   wgu_f32, wd_f32, wgu_bf, wd_bf, sems, *, layer):
    i = pl.program_id(0)
    used = i < nu_ref[0]

    def weight_copies(e, slot):
        return (pltpu.make_async_copy(wgu_hbm.at[layer, e], wgu_f32.at[slot], sems.at[0, slot]),
                pltpu.make_async_copy(wd_hbm.at[layer, e], wd_f32.at[slot], sems.at[1, slot]))

    @pl.when(i == 0)
    def _():
        for cp in weight_copies(be_ref[0], 0):
            cp.start()

    @pl.when(jnp.logical_and(used, jnp.logical_or(i == 0, be_ref[i] != be_ref[jnp.maximum(i - 1, 0)])))
    def _():
        slot = sl_ref[i]
        for cp in weight_copies(be_ref[i], slot):
            cp.wait()

        @pl.when(nx_ref[i] >= 0)
        def _():
            for cp in weight_copies(nx_ref[i], 1 - slot):
                cp.start()

        wgu_bf[...] = wgu_f32[slot].astype(BF16)
        wd_bf[...] = wd_f32[slot].astype(BF16)

    @pl.when(used)
    def _():
        x = _load_row_tiles(xs_ref, 0, xs_ref.shape[0]).astype(BF16)
        gu = _dot(x, wgu_bf[...]) + bgu_ref[0, 0]
        gate = jnp.minimum(gu[:, :D_FF], SWIGLU_LIMIT)
        up = jnp.clip(gu[:, D_FF:], -SWIGLU_LIMIT, SWIGLU_LIMIT)
        hdn = (up + 1.0) * gate * _sigmoid(SWIGLU_ALPHA * gate)
        _store_row_tiles(ys_ref, _dot(hdn.astype(BF16), wd_bf[...]) + bd_ref[0, 0])

    @pl.when(jnp.logical_not(used))
    def _():
        ys_ref[...] = jnp.zeros(ys_ref.shape, F32)


def _experts(xs, plan, wgu, bgu, wd, bd, layer):
    blk_e, next_e, slot, n_used = plan
    n_rows = xs.shape[0]
    d = xs.shape[1] * xs.shape[2]
    nblk = n_rows // MOE_BLOCK

    def blk(i, nu):
        return jnp.minimum(i, nu[0] - 1)

    def bias(i, be, nx, sl, nu):
        return (layer, be[blk(i, nu)], 0, 0)

    grid_spec = pltpu.PrefetchScalarGridSpec(
        num_scalar_prefetch=4,
        grid=(nblk,),
        in_specs=[
            pl.BlockSpec((MOE_BLOCK,) + xs.shape[1:], lambda i, be, nx, sl, nu: (blk(i, nu), 0, 0)),
            pl.BlockSpec(memory_space=pl.ANY),
            pl.BlockSpec((1, 1, 1, 2 * D_FF), bias),
            pl.BlockSpec(memory_space=pl.ANY),
            pl.BlockSpec((1, 1, 1, d), bias),
        ],
        out_specs=pl.BlockSpec((MOE_BLOCK,) + xs.shape[1:], lambda i, be, nx, sl, nu: (i, 0, 0)),
        scratch_shapes=[pltpu.VMEM((2, d, 2 * D_FF), F32), pltpu.VMEM((2, D_FF, d), F32),
                        pltpu.VMEM((d, 2 * D_FF), BF16), pltpu.VMEM((D_FF, d), BF16),
                        pltpu.SemaphoreType.DMA((2, 2))],
    )
    return pl.pallas_call(
        functools.partial(_expert_kernel, layer=layer),
        grid_spec=grid_spec,
        out_shape=jax.ShapeDtypeStruct(xs.shape, F32),
        compiler_params=_cparams(("arbitrary",)),
        name="moe_experts",
    )(blk_e, next_e, slot, n_used, xs, wgu, bgu, wd, bd)


def _combine_kernel(dest_ref, x1_ref, gw_ref, mod_ref, ys_ref, o_ref, buf, sem, *, n_lat, tm):
    i = pl.program_id(0)
    d = x1_ref.shape[1]
    n = dest_ref.shape[0]

    def start(j, c):
        for q in range(DMA_QUEUES):
            r = DMA_QUEUES * j + q
            pltpu.make_async_copy(ys_ref.at[dest_ref[r]], buf.at[r], sem).start(priority=q)
        return c

    lax.fori_loop(0, n // DMA_QUEUES, start, 0, unroll=8)
    pltpu.make_async_copy(ys_ref.at[pl.ds(0, n)], buf, sem).wait()

    lat = i * tm < n_lat
    g2 = jnp.where(lat, mod_ref[0:1, 5 * d:6 * d], mod_ref[1:2, 5 * d:6 * d])
    gw = gw_ref[...]
    m = gw[:, 0:1] * _load_row_tiles(buf, 0, tm)
    for k in range(1, TOP_K):
        m = m + gw[:, k:k + 1] * _load_row_tiles(buf, k * tm, tm)
    o_ref[...] = x1_ref[...] + g2 * m


def _combine(x1, gw, mod, ys, dest_tiles, n_lat):
    t, d = x1.shape
    tm = ROW_TILE
    return pl.pallas_call(
        functools.partial(_combine_kernel, n_lat=n_lat, tm=tm),
        grid=(t // tm,),
        in_specs=[pl.BlockSpec((tm * TOP_K,), lambda i: (i,), memory_space=pltpu.SMEM),
                  pl.BlockSpec((tm, d), lambda i: (i, 0)),
                  pl.BlockSpec((tm, 128), lambda i: (i, 0)),
                  pl.BlockSpec(mod.shape, lambda i: (0, 0)),
                  pl.BlockSpec(memory_space=pl.ANY)],
        out_specs=pl.BlockSpec((tm, d), lambda i: (i, 0)),
        out_shape=jax.ShapeDtypeStruct((t, d), F32),
        scratch_shapes=[pltpu.VMEM((tm * TOP_K,) + ys.shape[1:], F32), pltpu.SemaphoreType.DMA(())],
        compiler_params=_cparams(("arbitrary",)),
        name="moe_combine",
    )(dest_tiles, x1, gw, mod, ys)


def _moe_plan(gi, rk, counts):
    t = gi.shape[1]
    top_i = gi[:TOP_K]
    rank = rk[:TOP_K]
    cnt = counts[0, :N_EXPERTS].astype(jnp.int32)
    padded = (cnt + MOE_BLOCK - 1) // MOE_BLOCK * MOE_BLOCK
    pad_ends = jnp.cumsum(padded)
    pad_starts = pad_ends - padded
    experts = jnp.arange(N_EXPERTS, dtype=jnp.int32)[:, None, None]
    dest = jnp.sum(jnp.where(top_i[None] == experts, pad_starts[:, None, None], 0), axis=0) + rank
    n_rows = t * TOP_K + N_EXPERTS * MOE_BLOCK
    blk_start = jnp.arange(n_rows // MOE_BLOCK, dtype=jnp.int32) * MOE_BLOCK
    blk_e = jnp.minimum(jnp.sum(blk_start[:, None] >= pad_ends[None, :], axis=1), N_EXPERTS - 1).astype(jnp.int32)
    n_used = (pad_ends[-1:] // MOE_BLOCK).astype(jnp.int32)
    has_rows = cnt > 0
    later = jnp.where(has_rows[None, :] & (jnp.arange(N_EXPERTS)[None, :] > jnp.arange(N_EXPERTS)[:, None]),
                      jnp.arange(N_EXPERTS)[None, :], N_EXPERTS)
    next_of = jnp.min(later, axis=1)
    next_of = jnp.where(next_of < N_EXPERTS, next_of, -1).astype(jnp.int32)
    run_of = (jnp.cumsum(has_rows.astype(jnp.int32)) - 1).astype(jnp.int32)
    next_e = next_of[blk_e]
    slot = run_of[blk_e] & 1
    dest_tiles = dest.reshape(TOP_K, t // ROW_TILE, ROW_TILE).transpose(1, 0, 2).reshape(-1).astype(jnp.int32)
    return dest_tiles, (blk_e, next_e, slot, n_used), n_rows


def _rope_tables(n_lat, n_ctx):
    pos = jnp.arange(n_lat)
    row = (pos // GRID_W).astype(F32)
    col = (pos % GRID_W).astype(F32)
    freq = ROPE_BASE ** (-jnp.arange(ROPE_PAIRS, dtype=F32) / ROPE_PAIRS)
    ra = row[:, None] * freq
    ca = col[:, None] * freq
    cos64 = jnp.concatenate([jnp.cos(ra), jnp.cos(ra), jnp.cos(ca), jnp.cos(ca)], axis=1)
    sin64 = jnp.concatenate([-jnp.sin(ra), jnp.sin(ra), -jnp.sin(ca), jnp.sin(ca)], axis=1)
    cos_t = jnp.concatenate([jnp.tile(cos64, (1, DA_W // DA_DH)), jnp.ones((n_ctx, DA_W), F32)], axis=0)
    sin_t = jnp.concatenate([jnp.tile(sin64, (1, DA_W // DA_DH)), jnp.zeros((n_ctx, DA_W), F32)], axis=0)
    return cos_t, sin_t


def kernel(x, c, ctx, c_ctx, ada_w, ada_b, norm1_g, norm2_g, w_in, qn_g, kn_g, lam_qk, subln_g, hg_lb,
           hg_norm_g, w_branch_a, w_branch_b, w_out, router_w, router_b, w_gu, b_gu, w_down, b_down):
    bsz, n_lat, d = x.shape
    n_ctx = ctx.shape[1]
    depth = ada_w.shape[0]
    assert bsz == 1 and d == D_MODEL and n_lat % 512 == 0 and n_ctx % ROW_TILE == 0
    t = n_lat + n_ctx

    xx = jnp.concatenate([x[0], ctx[0]], axis=0)
    cc = jnp.zeros((8, d), F32).at[0].set(c[0]).at[1].set(c_ctx)
    mods = _modulation(cc, ada_w, ada_b)

    cos_t, sin_t = _rope_tables(n_lat, n_ctx)
    lane = jnp.arange(DA_W)
    seg64 = (lane[:, None] // DA_DH == lane[None, :] // DA_DH).astype(BF16)
    cs = jnp.cumsum(jax.nn.softmax(hg_lb.astype(F32), axis=1), axis=1)
    lb_all = cs - cs[:, :1]
    rw_pad = jnp.zeros((depth, d, 128), F32).at[:, :, :N_EXPERTS].set(router_w)
    rb_pad = jnp.zeros((depth, 1, 128), F32).at[:, 0, :N_EXPERTS].set(router_b)

    key_chunk = next(c for c in ATT_KEY_CHUNKS if t % c == 0 and n_ctx <= c)
    xs = None
    for l in range(depth):
        lam_init = 0.8 - 0.6 * math.exp(-0.3 * l)
        mod = mods[l]
        p = _inproj(xx, norm1_g[l][None], mod, w_in[l].astype(BF16), n_lat)
        k_r, q_t, v_t3, qn2, kn2_tiles = _prep(p, cos_t, sin_t, seg64, jnp.tile(qn_g[l], DA_W // DA_DH)[None],
                                               jnp.tile(kn_g[l], DA_W // DA_DH)[None], key_chunk)
        oa = _attention_all(q_t, k_r, v_t3, qn2, kn2_tiles, lam_qk[l], subln_g[l][:, None], lam_init,
                            n_lat, n_ctx)
        o_f, o_b = _hgrn(p, lb_all[0, l][None], lb_all[1, l][None], n_lat)
        x1, h2, gw, gi, rk, counts = _post(
            xx, oa, o_f, o_b, p, mod, jnp.tile(hg_norm_g[l], HG_HEADS)[None], norm2_g[l][None],
            w_branch_a[l].astype(BF16), w_branch_b[l].astype(BF16), w_out[l].astype(BF16),
            rw_pad[l], rb_pad[l], n_lat)
        dest_tiles, plan, n_rows = _moe_plan(gi, rk, counts)
        xs = _dispatch(h2, dest_tiles, jnp.zeros((n_rows,) + h2.shape[1:], F32) if xs is None else xs)
        ys = _experts(xs, plan, w_gu, b_gu[:, :, None, :], w_down, b_down[:, :, None, :], l)
        xx = _combine(x1, gw, mod, ys, dest_tiles, n_lat)
    return xx[:n_lat][None]
```

```python
import functools
import math

import jax
import jax.numpy as jnp
from jax import lax
from jax.experimental import pallas as pl
from jax.experimental.pallas import tpu as pltpu

F32 = jnp.float32
BF16 = jnp.bfloat16
QK_DTYPE = jnp.float8_e4m3fn

D_MODEL = 1024
GRID_W = 64
EPS = 1e-6

DA_HEADS = 4
DA_DH = 64
DA_DV = 2 * DA_DH
DA_W = DA_HEADS * DA_DV
DA_SCALE = DA_DH ** -0.5
ROPE_PAIRS = DA_DH // 4
ROPE_BASE = 10000.0

HG_HEADS = 4
HG_DK = 128
HG_W = HG_HEADS * HG_DK
HG_CHUNK = 128

N_EXPERTS = 32
TOP_K = 4
D_FF = 1024
SWIGLU_ALPHA = 1.702
SWIGLU_LIMIT = 7.0
MOE_BLOCK = 256

ROW_TILE = 256
NEG_BIG = -1e30
LOG2E = 1.4426950408889634

VMEM_LIMIT = 56 * 1024 * 1024
ATT_TQ = 256
ATT_KEY_CHUNKS = (3328, 1280, 256)
DMA_QUEUES = 2
SCORE_BOUND_MAX = 40.0
SCORE_BOUND_SLACK = 1.01


def _cparams(sem):
    return pltpu.CompilerParams(dimension_semantics=sem, vmem_limit_bytes=VMEM_LIMIT)


def _split3(x):
    a = x.astype(BF16)
    r = x - a.astype(F32)
    b = r.astype(BF16)
    c = (r - b.astype(F32)).astype(BF16)
    return a, b, c


def _dot(a, b):
    return jnp.dot(a, b, preferred_element_type=F32)


def _dot_nt(a, b):
    return lax.dot_general(a, b, (((1,), (1,)), ((), ())), preferred_element_type=F32)


def _dot_tn(a, b):
    return lax.dot_general(a, b, (((0,), (0,)), ((), ())), preferred_element_type=F32)


def _sigmoid(x):
    return 0.5 * jnp.tanh(0.5 * x) + 0.5


def _tile_transpose(x):
    g = x.shape[0]
    sub = lax.broadcasted_iota(jnp.int32, (1, 1, 1, 1, 8, 128), 4)
    x = x.reshape(g, 2, 2, 2, 8, 128)
    for axis, k in ((1, 4), (2, 2), (3, 1)):
        lo = (sub & k) == 0
        a = lax.index_in_dim(x, 0, axis, keepdims=True)
        b = lax.index_in_dim(x, 1, axis, keepdims=True)
        a2 = jnp.where(lo, a, pltpu.roll(b, k, 4))
        b2 = jnp.where(lo, pltpu.roll(a, 8 - k, 4), b)
        x = jnp.concatenate([a2, b2], axis=axis)
    return x.reshape(g, 8, 8, 128)


def _store_row_tiles(ref, val):
    n = val.shape[0]
    z = jnp.stack([val[:, s * 128:(s + 1) * 128].reshape(n // 8, 8, 128) for s in range(8)], axis=1)
    ref[...] = _tile_transpose(z).reshape(n, 8, 128)


def _load_row_tiles(ref, lo, n):
    y = _tile_transpose(ref[lo:lo + n].reshape(n // 8, 8, 8, 128))
    return jnp.concatenate([y[:, s].reshape(n, 128) for s in range(8)], axis=1)


def _mod_kernel(c_ref, w_ref, b_ref, o_ref):
    cv = c_ref[...]
    a = cv * _sigmoid(cv)
    w = w_ref[0]
    a1, a2, a3 = _split3(a)
    w1, w2, w3 = _split3(w)
    acc = _dot(a1, w1) + (_dot(a1, w2) + _dot(a2, w1)) + (_dot(a2, w2) + _dot(a1, w3) + _dot(a3, w1))
    o_ref[0] = acc + b_ref[0]


def _modulation(cc, ada_w, ada_b):
    depth, d, n = ada_w.shape
    tn = 1536
    return pl.pallas_call(
        _mod_kernel,
        grid=(depth, n // tn),
        in_specs=[
            pl.BlockSpec((8, d), lambda l, j: (0, 0)),
            pl.BlockSpec((1, d, tn), lambda l, j: (l, 0, j)),
            pl.BlockSpec((1, 1, tn), lambda l, j: (l, 0, j)),
        ],
        out_specs=pl.BlockSpec((1, 8, tn), lambda l, j: (l, 0, j)),
        out_shape=jax.ShapeDtypeStruct((depth, 8, n), F32),
        compiler_params=_cparams(("parallel", "parallel")),
        name="adaln_mod",
    )(cc, ada_w, ada_b.reshape(depth, 1, n))


def _inproj_kernel(x_ref, g_ref, mod_ref, w_ref, o_ref, h_scr, *, n_lat, tm):
    i = pl.program_id(0)
    j = pl.program_id(1)
    d = x_ref.shape[1]

    @pl.when(j == 0)
    def _():
        x = x_ref[...]
        y = x * lax.rsqrt(jnp.mean(x * x, axis=-1, keepdims=True) + EPS) * g_ref[...]
        row = i * tm + lax.broadcasted_iota(jnp.int32, (tm, 1), 0)
        lat = row < n_lat
        sh = jnp.where(lat, mod_ref[0:1, 0:d], mod_ref[1:2, 0:d])
        sc = jnp.where(lat, mod_ref[0:1, d:2 * d], mod_ref[1:2, d:2 * d])
        h_scr[...] = (y * (1.0 + sc) + sh).astype(BF16)

    o_ref[...] = _dot(h_scr[...], w_ref[...])


def _inproj(x, g, mod, w_bf, n_lat):
    t, d = x.shape
    n = w_bf.shape[1]
    tm = 1280 if t % 1280 == 0 else ROW_TILE
    tn = 1536
    return pl.pallas_call(
        functools.partial(_inproj_kernel, n_lat=n_lat, tm=tm),
        grid=(t // tm, n // tn),
        in_specs=[
            pl.BlockSpec((tm, d), lambda i, j: (i, 0)),
            pl.BlockSpec((1, d), lambda i, j: (0, 0)),
            pl.BlockSpec((8, mod.shape[1]), lambda i, j: (0, 0)),
            pl.BlockSpec((d, tn), lambda i, j: (0, j)),
        ],
        out_specs=pl.BlockSpec((tm, tn), lambda i, j: (i, j)),
        out_shape=jax.ShapeDtypeStruct((t, n), F32),
        scratch_shapes=[pltpu.VMEM((tm, d), BF16)],
        compiler_params=_cparams(("parallel", "arbitrary")),
        name="inproj",
    )(x, g, mod, w_bf)


def _segment_mean_sq(x, seg_ref, width):
    x2 = x * x
    hi = x2.astype(BF16)
    lo = (x2 - hi.astype(F32)).astype(BF16)
    seg = seg_ref[...]
    return (_dot(hi, seg) + _dot(lo, seg)) * (1.0 / width)


def _prep_kernel(q_ref, k_ref, v_ref, cos_ref, sin_ref, seg_ref, qg_ref, kg_ref,
                 k_out, qt_out, vt_out, qn_out, kn_out):
    tm = q_ref.shape[0]
    cos = cos_ref[...]
    sin = sin_ref[...]
    lane = lax.broadcasted_iota(jnp.int32, (tm, DA_W), 1)
    first_half = (lane & 31) < 16

    def norm_rope(x, g):
        y = x * lax.rsqrt(_segment_mean_sq(x, seg_ref, DA_DH) + EPS) * g
        fwd = pltpu.roll(y, DA_W - 16, 1)
        bwd = pltpu.roll(y, 16, 1)
        partner = jnp.where(first_half, fwd, bwd)
        return y * cos + partner * sin

    k_b = norm_rope(k_ref[...], kg_ref[...]).astype(QK_DTYPE)
    k_out[...] = k_b
    k_f = k_b.astype(F32)
    k_n2 = _dot((k_f * k_f).astype(BF16), seg_ref[...])
    kn_out[...] = jnp.broadcast_to(jnp.max(k_n2, axis=0, keepdims=True), kn_out.shape)

    q = norm_rope(q_ref[...], qg_ref[...]) * (DA_SCALE * LOG2E)
    lane_h = lax.broadcasted_iota(jnp.int32, (tm, DA_DV), 1)
    v = v_ref[...]
    for h in range(DA_HEADS):
        qh = q[:, h * DA_DV:(h + 1) * DA_DV]
        for m in range(2):
            keep = (lane_h < DA_DH) if m == 0 else (lane_h >= DA_DH)
            r = 2 * h + m
            qt_b = jnp.where(keep, qh, 0.0).T.astype(QK_DTYPE)
            qt_out[r * DA_DV:(r + 1) * DA_DV, :] = qt_b
            qt_f = qt_b.astype(F32)
            qn_out[r:r + 1, :] = jnp.sum(qt_f * qt_f, axis=0, keepdims=True)
        vt_out[0, h * DA_DV:(h + 1) * DA_DV, :] = v[:, h * DA_DV:(h + 1) * DA_DV].T.astype(BF16)


def _prep(p, cos_t, sin_t, seg64, qg, kg, chunk):
    t = p.shape[0]
    tm = ROW_TILE
    per_chunk = chunk // tm
    return pl.pallas_call(
        _prep_kernel,
        grid=(t // tm,),
        in_specs=[
            pl.BlockSpec((tm, DA_W), lambda i: (i, 0)),
            pl.BlockSpec((tm, DA_W), lambda i: (i, 1)),
            pl.BlockSpec((tm, DA_W), lambda i: (i, 2)),
            pl.BlockSpec((tm, DA_W), lambda i: (i, 0)),
            pl.BlockSpec((tm, DA_W), lambda i: (i, 0)),
            pl.BlockSpec((DA_W, DA_W), lambda i: (0, 0)),
            pl.BlockSpec((1, DA_W), lambda i: (0, 0)),
            pl.BlockSpec((1, DA_W), lambda i: (0, 0)),
        ],
        out_specs=[
            pl.BlockSpec((tm, DA_W), lambda i: (i, 0)),
            pl.BlockSpec((2 * DA_HEADS * DA_DV, tm), lambda i: (0, i)),
            pl.BlockSpec((1, DA_W, tm), lambda i: (i // per_chunk, 0, i % per_chunk)),
            pl.BlockSpec((2 * DA_HEADS, tm), lambda i: (0, i)),
            pl.BlockSpec((8, DA_W), lambda i: (i, 0)),
        ],
        out_shape=[
            jax.ShapeDtypeStruct((t, DA_W), QK_DTYPE),
            jax.ShapeDtypeStruct((2 * DA_HEADS * DA_DV, t), QK_DTYPE),
            jax.ShapeDtypeStruct((t // chunk, DA_W, chunk), BF16),
            jax.ShapeDtypeStruct((2 * DA_HEADS, t), F32),
            jax.ShapeDtypeStruct((8 * (t // tm), DA_W), F32),
        ],
        compiler_params=_cparams(("parallel",)),
        name="qkv_prep",
    )(p, p, p, cos_t, sin_t, seg64, qg, kg)


def _attn_finalize(lam_ref, sg_ref, o_ref, acc_scr, l_scr, lam_init):
    lq = lam_ref[...]
    lam = (jnp.exp(jnp.sum(lq[0:1] * lq[1:2], axis=-1, keepdims=True))
           - jnp.exp(jnp.sum(lq[2:3] * lq[3:4], axis=-1, keepdims=True)) + lam_init)
    for h in range(DA_HEADS):
        r = 2 * h
        o = acc_scr[r] / l_scr[r:r + 1, :] - lam * (acc_scr[r + 1] / l_scr[r + 1:r + 2, :])
        o = o * lax.rsqrt(jnp.mean(o * o, axis=0, keepdims=True) + EPS)
        o = o * sg_ref[...] * (1.0 - lam_init)
        o_ref[:, h * DA_DV:(h + 1) * DA_DV] = o.T.astype(o_ref.dtype)


def _attn_kernel(qt_ref, k_hbm, vt_hbm, lam_ref, sg_ref, o_ref, k_scr, vt_scr, m_scr, l_scr, acc_scr, sems,
                 *, lam_init, stabilised):
    @pl.when(pl.program_id(0) == 0)
    def _():
        copies = (pltpu.make_async_copy(k_hbm, k_scr, sems.at[0]), pltpu.make_async_copy(vt_hbm, vt_scr, sems.at[1]))
        for cp in copies:
            cp.start()
        for cp in copies:
            cp.wait()

    acc_scr[...] = jnp.zeros(acc_scr.shape, F32)
    l_scr[...] = jnp.zeros(l_scr.shape, F32)
    if stabilised:
        m_scr[...] = jnp.full(m_scr.shape, NEG_BIG, F32)

    def chunk(c, carry):
        for h in range(DA_HEADS):
            k_h = k_scr[c, :, h * DA_DV:(h + 1) * DA_DV]
            vt_h = vt_scr[c, h * DA_DV:(h + 1) * DA_DV, :]
            for m in range(2):
                r = 2 * h + m
                s = _dot(k_h, qt_ref[r * DA_DV:(r + 1) * DA_DV, :])
                if stabilised:
                    m_prev = m_scr[r:r + 1, :]
                    m_new = jnp.maximum(m_prev, jnp.max(s, axis=0, keepdims=True))
                    alpha = jnp.exp2(m_prev - m_new)
                    p = jnp.exp2((s - m_new).astype(BF16))
                    acc_scr[r] = alpha * acc_scr[r] + _dot(vt_h, p)
                    l_scr[r:r + 1, :] = alpha * l_scr[r:r + 1, :] + jnp.sum(p.astype(F32), axis=0, keepdims=True)
                    m_scr[r:r + 1, :] = m_new
                else:
                    p = jnp.exp2(s.astype(BF16))
                    acc_scr[r] += _dot(vt_h, p)
                    l_scr[r:r + 1, :] += jnp.sum(p.astype(F32), axis=0, keepdims=True)
        return carry

    lax.fori_loop(0, k_scr.shape[0], chunk, 0)
    _attn_finalize(lam_ref, sg_ref, o_ref, acc_scr, l_scr, lam_init)


def _attention(qt, k3, vt3, lam_qk, subln_col, lam_init, stabilised, *, q_start, n_q, tq):
    qo = q_start // tq
    return pl.pallas_call(
        functools.partial(_attn_kernel, lam_init=lam_init, stabilised=stabilised),
        grid=(n_q // tq,),
        in_specs=[
            pl.BlockSpec((qt.shape[0], tq), lambda i: (0, i + qo)),
            pl.BlockSpec(memory_space=pl.ANY),
            pl.BlockSpec(memory_space=pl.ANY),
            pl.BlockSpec((4, DA_DH), lambda i: (0, 0)),
            pl.BlockSpec((DA_DV, 1), lambda i: (0, 0)),
        ],
        out_specs=pl.BlockSpec((tq, DA_W), lambda i: (i, 0)),
        out_shape=jax.ShapeDtypeStruct((n_q, DA_W), BF16),
        scratch_shapes=[
            pltpu.VMEM(k3.shape, k3.dtype),
            pltpu.VMEM(vt3.shape, vt3.dtype),
            pltpu.VMEM((2 * DA_HEADS, tq), F32),
            pltpu.VMEM((2 * DA_HEADS, tq), F32),
            pltpu.VMEM((2 * DA_HEADS, DA_DV, tq), F32),
            pltpu.SemaphoreType.DMA((2,)),
        ],
        compiler_params=_cparams(("arbitrary",)),
        name="diff_attn_online" if stabilised else "diff_attn_plain",
    )(qt, k3, vt3, lam_qk, subln_col)


def _attention_all(qt, k, vt3, qn2, kn2_tiles, lam_qk, subln_col, lam_init, n_lat, n_ctx):
    n_chunks, _, chunk = vt3.shape
    n_maps = 2 * DA_HEADS
    kmax2 = jnp.max(kn2_tiles, axis=0).reshape(n_maps, DA_DH)[:, 0]
    bound = jnp.sqrt(jnp.max(qn2, axis=1)) * jnp.sqrt(kmax2) * SCORE_BOUND_SLACK
    bounded = jnp.max(bound) <= SCORE_BOUND_MAX
    k3 = k.reshape(n_chunks, chunk, DA_W)

    def run(stabilised):
        def f(args):
            qt_, k3_, vt3_, lam_, sg_ = args
            lat = _attention(qt_, k3_, vt3_, lam_, sg_, lam_init, stabilised, q_start=0, n_q=n_lat, tq=ATT_TQ)
            ctx = _attention(qt_, k3_[n_chunks - 1:, chunk - n_ctx:, :], vt3_[n_chunks - 1:, :, chunk - n_ctx:],
                             lam_, sg_, lam_init, stabilised, q_start=n_lat, n_q=n_ctx, tq=ATT_TQ)
            return jnp.concatenate([lat, ctx], axis=0)
        return f

    return lax.cond(bounded, run(False), run(True), (qt, k3, vt3, lam_qk, subln_col))


def _hg_chunk(q, z, v, lb, st_ref, rev):
    c, w = q.shape
    row = lax.broadcasted_iota(jnp.int32, (c, 1), 0)

    u = jnp.exp(-jnp.abs(z))
    sig_abs = 1.0 / (1.0 + u)
    y = jnp.log(1.0 - lb) + jnp.minimum(z, 0.0) - jnp.log(1.0 + u)
    log_lb = jnp.log(lb)
    log_f = jnp.maximum(log_lb, y) + jnp.log(1.0 + jnp.exp(-jnp.abs(log_lb - y)))
    key = (1.0 - lb) * jnp.where(z >= 0.0, u * sig_abs, sig_abs)

    ri = lax.broadcasted_iota(jnp.int32, (c, c), 0)
    ci = lax.broadcasted_iota(jnp.int32, (c, c), 1)
    tri = jnp.where((ci >= ri) if rev else (ci <= ri), 1.0, 0.0).astype(BF16)
    g1, g2, g3 = _split3(log_f)
    b = _dot(tri, g1) + _dot(tri, g2) + _dot(tri, g3)
    end = 0 if rev else c - 1
    b_end = b[end:end + 1, :]

    q_in = (q * jnp.exp(b)).astype(BF16)
    k_out = (key * jnp.exp(b_end - b)).astype(BF16)
    decay = jnp.exp(b_end)

    up = pltpu.roll(log_f, 1, 0)
    dn = pltpu.roll(log_f, c - 1, 0)
    q_lv, k_lv, masks = [], [], []
    lvl = c // 2
    while lvl >= 1:
        half = (row // lvl) & 1
        q_rows = (half == 0) if rev else (half == 1)
        if lvl >= 4:
            pos = lvl if rev else lvl - 1
            ref = jnp.broadcast_to(b.reshape(c // (2 * lvl), 2 * lvl, w)[:, pos:pos + 1, :],
                                   (c // (2 * lvl), 2 * lvl, w)).reshape(c, w)
            eq, ek = b - ref, ref - b
        elif lvl == 2:
            j = row & 3
            if rev:
                eq = jnp.where(j == 1, log_f, log_f + dn)
                ek = jnp.where(j == 2, 0.0, up)
            else:
                eq = jnp.where(j == 2, log_f, log_f + up)
                ek = jnp.where(j == 1, 0.0, dn)
        else:
            eq, ek = log_f, jnp.zeros_like(log_f)
        q_lv.append((q * jnp.exp(jnp.where(q_rows, eq, NEG_BIG))).astype(BF16))
        k_lv.append((key * jnp.exp(jnp.where(q_rows, NEG_BIG, ek))).astype(BF16))
        masks.append(None if 2 * lvl == c else
                     jnp.where((ri // (2 * lvl)) == (ci // (2 * lvl)), 1.0, 0.0))
        lvl //= 2
    qk = q * key
    on_diag = ri == ci

    outs = []
    for h in range(HG_HEADS):
        sl = slice(h * HG_DK, (h + 1) * HG_DK)
        a = jnp.where(on_diag, jnp.sum(qk[:, sl], axis=-1, keepdims=True), 0.0)
        for q_l, k_l, msk in zip(q_lv, k_lv, masks):
            a_l = _dot_nt(q_l[:, sl], k_l[:, sl])
            a = a + (a_l if msk is None else a_l * msk)
        st = st_ref[h]
        v_h = v[:, sl].astype(BF16)
        outs.append(_dot_nt(q_in[:, sl], st.astype(BF16)) + _dot(a.astype(BF16), v_h))
        st_ref[h] = st * decay[:, sl] + _dot_tn(v_h, k_out[:, sl])
    return jnp.concatenate(outs, axis=1)


def _hgrn_kernel(qf_ref, zf_ref, vf_ref, qb_ref, zb_ref, vb_ref, lbf_ref, lbb_ref,
                 of_ref, ob_ref, sf_scr, sb_scr):
    @pl.when(pl.program_id(0) == 0)
    def _():
        sf_scr[...] = jnp.zeros(sf_scr.shape, F32)
        sb_scr[...] = jnp.zeros(sb_scr.shape, F32)

    n_chunks = qf_ref.shape[0] // HG_CHUNK

    def body(ci, carry):
        rf = pl.ds(pl.multiple_of(ci * HG_CHUNK, HG_CHUNK), HG_CHUNK)
        of_ref[rf, :] = _hg_chunk(qf_ref[rf, :], zf_ref[rf, :], vf_ref[rf, :], lbf_ref[...], sf_scr, False)
        rb = pl.ds(pl.multiple_of((n_chunks - 1 - ci) * HG_CHUNK, HG_CHUNK), HG_CHUNK)
        ob_ref[rb, :] = _hg_chunk(qb_ref[rb, :], zb_ref[rb, :], vb_ref[rb, :], lbb_ref[...], sb_scr, True)
        return carry

    lax.fori_loop(0, n_chunks, body, 0)


def _hgrn(p, lb_f, lb_b, n_lat):
    t = p.shape[0]
    tm = ROW_TILE
    nb = t // tm
    n_lat_b = n_lat // tm
    n_ctx_b = nb - n_lat_b

    def jf(i):
        return jnp.where(i < n_ctx_b, n_lat_b + i, i - n_ctx_b)

    def jb(i):
        return jnp.where(i < n_ctx_b, nb - 1 - i, nb - 1 - i)

    def spec(order, colblk):
        return pl.BlockSpec((tm, HG_W), lambda i: (order(i), colblk))

    return pl.pallas_call(
        _hgrn_kernel,
        grid=(nb,),
        in_specs=[spec(jf, 3), spec(jf, 4), spec(jf, 6), spec(jb, 3), spec(jb, 5), spec(jb, 6),
                  pl.BlockSpec((1, HG_W), lambda i: (0, 0)), pl.BlockSpec((1, HG_W), lambda i: (0, 0))],
        out_specs=[pl.BlockSpec((tm, HG_W), lambda i: (jf(i), 0)),
                   pl.BlockSpec((tm, HG_W), lambda i: (jb(i), 0))],
        out_shape=[jax.ShapeDtypeStruct((t, HG_W), F32), jax.ShapeDtypeStruct((t, HG_W), F32)],
        scratch_shapes=[pltpu.VMEM((HG_HEADS, HG_DK, HG_DK), F32), pltpu.VMEM((HG_HEADS, HG_DK, HG_DK), F32)],
        compiler_params=_cparams(("arbitrary",)),
        name="hgrn2",
    )(p, p, p, p, p, p, lb_f, lb_b)


def _post_kernel(x_ref, oa_ref, of_ref, ob_ref, gh_ref, ga_ref, gb_ref, mod_ref, hg_ref, n2_ref,
                 wa_ref, wb_ref, wo_ref, rw_ref, rb_ref,
                 x1_ref, h2_ref, gw_ref, gi_ref, rk_ref, cnt_ref, run_scr, *, n_lat, tm):
    i = pl.program_id(0)
    d = x_ref.shape[1]

    @pl.when(i == 0)
    def _():
        run_scr[...] = jnp.zeros(run_scr.shape, F32)

    lat = i * tm < n_lat

    def mod(c):
        return jnp.where(lat, mod_ref[0:1, c * d:(c + 1) * d], mod_ref[1:2, c * d:(c + 1) * d])

    o = of_ref[...] + ob_ref[...]
    parts = []
    for h in range(HG_HEADS):
        oh = o[:, h * HG_DK:(h + 1) * HG_DK]
        parts.append(oh * lax.rsqrt(jnp.mean(oh * oh, axis=-1, keepdims=True) + EPS))
    gh = gh_ref[...]
    ob = jnp.concatenate(parts, axis=1) * hg_ref[...] * (gh * _sigmoid(gh))
    ya = _dot(oa_ref[...], wa_ref[...])
    yb = _dot(ob.astype(BF16), wb_ref[...])
    mix = _sigmoid(ga_ref[...]) * ya + _sigmoid(gb_ref[...]) * yb
    x1 = x_ref[...] + mod(2) * _dot(mix.astype(BF16), wo_ref[...])
    x1_ref[...] = x1

    h2 = x1 * lax.rsqrt(jnp.mean(x1 * x1, axis=-1, keepdims=True) + EPS) * n2_ref[...]
    h2 = h2 * (1.0 + mod(4)) + mod(3)
    _store_row_tiles(h2_ref, h2)

    h_hi = h2.astype(BF16)
    h_lo = (h2 - h_hi.astype(F32)).astype(BF16)
    rw = rw_ref[...]
    r_hi = rw.astype(BF16)
    r_lo = (rw - r_hi.astype(F32)).astype(BF16)
    logits = _dot(h_hi, r_hi) + (_dot(h_hi, r_lo) + _dot(h_lo, r_hi)) + rb_ref[...]

    lane = lax.broadcasted_iota(jnp.int32, logits.shape, 1)
    work = jnp.where(lane < N_EXPERTS, logits, -jnp.inf)
    sel_any = jnp.zeros(logits.shape, F32)
    vals, idxs = [], []
    for _ in range(TOP_K):
        mx = jnp.max(work, axis=-1, keepdims=True)
        idx = jnp.min(jnp.where(work == mx, lane, 2 * N_EXPERTS), axis=-1, keepdims=True)
        hit = lane == idx
        vals.append(mx)
        idxs.append(idx)
        sel_any = jnp.where(hit, 1.0, sel_any)
        work = jnp.where(hit, -jnp.inf, work)
    es = [jnp.exp(vk - vals[0]) for vk in vals]
    denom = es[0] + es[1] + es[2] + es[3]

    ri = lax.broadcasted_iota(jnp.int32, (tm, tm), 0)
    ci = lax.broadcasted_iota(jnp.int32, (tm, tm), 1)
    below = jnp.where(ci < ri, 1.0, 0.0).astype(BF16)
    rank_e = run_scr[0:1, :] + _dot(below, sel_any.astype(BF16))
    gw = jnp.zeros(logits.shape, F32)
    gi = jnp.zeros(logits.shape, jnp.int32)
    rk = jnp.zeros(logits.shape, jnp.int32)
    for k in range(TOP_K):
        at_k = lane == k
        gw = jnp.where(at_k, es[k] / denom, gw)
        gi = jnp.where(at_k, idxs[k], gi)
        rnk = jnp.sum(jnp.where(lane == idxs[k], rank_e, 0.0), axis=-1, keepdims=True)
        rk = jnp.where(at_k, rnk.astype(jnp.int32), rk)
    gw_ref[...] = gw
    gi_ref[...] = gi.T[0:8, :]
    rk_ref[...] = rk.T[0:8, :]
    run = run_scr[0:1, :] + jnp.sum(sel_any, axis=0, keepdims=True)
    run_scr[...] = jnp.broadcast_to(run, run_scr.shape)
    cnt_ref[...] = jnp.broadcast_to(run, cnt_ref.shape)


def _post(x, oa, o_f, o_b, p, mod, hg_g, n2_g, wa, wb, wo, rw_pad, rb_pad, n_lat):
    t, d = x.shape
    tm = ROW_TILE

    def rows(width, colblk=0):
        return pl.BlockSpec((tm, width), lambda i: (i, colblk))

    def whole(a):
        return pl.BlockSpec(a.shape, lambda i: (0,) * a.ndim)

    return pl.pallas_call(
        functools.partial(_post_kernel, n_lat=n_lat, tm=tm),
        grid=(t // tm,),
        in_specs=[rows(d), rows(DA_W), rows(HG_W), rows(HG_W),
                  rows(HG_W, 7), rows(d, 4), rows(d, 5),
                  whole(mod), whole(hg_g), whole(n2_g), whole(wa), whole(wb), whole(wo),
                  whole(rw_pad), whole(rb_pad)],
        out_specs=[rows(d), pl.BlockSpec((tm, d // 128, 128), lambda i: (i, 0, 0)), rows(128),
                   pl.BlockSpec((8, tm), lambda i: (0, i)), pl.BlockSpec((8, tm), lambda i: (0, i)),
                   pl.BlockSpec((8, 128), lambda i: (0, 0))],
        out_shape=[jax.ShapeDtypeStruct((t, d), F32), jax.ShapeDtypeStruct((t, d // 128, 128), F32),
                   jax.ShapeDtypeStruct((t, 128), F32), jax.ShapeDtypeStruct((8, t), jnp.int32),
                   jax.ShapeDtypeStruct((8, t), jnp.int32), jax.ShapeDtypeStruct((8, 128), F32)],
        scratch_shapes=[pltpu.VMEM((8, 128), F32)],
        compiler_params=_cparams(("arbitrary",)),
        name="merge_router",
    )(x, oa, o_f, o_b, p, p, p, mod, hg_g, n2_g, wa, wb, wo, rw_pad, rb_pad)


def _dispatch_kernel(dest_ref, h_ref, xs_in, xs_out, sem):
    del xs_in
    n = dest_ref.shape[0]
    tm = h_ref.shape[0]

    for k in range(n // tm):
        def start(j, c, k=k):
            for q in range(DMA_QUEUES):
                tok = DMA_QUEUES * j + q
                pltpu.make_async_copy(h_ref.at[tok], xs_out.at[dest_ref[k * tm + tok]], sem).start(priority=q)
            return c

        lax.fori_loop(0, tm // DMA_QUEUES, start, 0, unroll=8)
    pltpu.make_async_copy(xs_out.at[pl.ds(0, n)], xs_out.at[pl.ds(0, n)], sem).wait()


def _dispatch(h2, dest_tiles, xs_zero):
    t = h2.shape[0]
    tm = ROW_TILE
    return pl.pallas_call(
        _dispatch_kernel,
        grid=(t // tm,),
        in_specs=[pl.BlockSpec((tm * TOP_K,), lambda i: (i,), memory_space=pltpu.SMEM),
                  pl.BlockSpec((tm,) + h2.shape[1:], lambda i: (i, 0, 0)),
                  pl.BlockSpec(memory_space=pl.ANY)],
        out_specs=pl.BlockSpec(memory_space=pl.ANY),
        out_shape=jax.ShapeDtypeStruct(xs_zero.shape, xs_zero.dtype),
        scratch_shapes=[pltpu.SemaphoreType.DMA(())],
        input_output_aliases={2: 0},
        compiler_params=_cparams(("arbitrary",)),
        name="moe_dispatch",
    )(dest_tiles, h2, xs_zero)


def _expert_kernel(be_ref, nx_ref, sl_ref, nu_ref, xs_ref, wgu_hbm, bgu_ref, wd_hbm, bd_ref, ys_ref,
                   wgu_f32, wd_f32, wgu_bf, wd_bf, sems, *, layer):
    i = pl.program_id(0)
    used = i < nu_ref[0]

    def weight_copies(e, slot):
        return (pltpu.make_async_copy(wgu_hbm.at[layer, e], wgu_f32.at[slot], sems.at[0, slot]),
                pltpu.make_async_copy(wd_hbm.at[layer, e], wd_f32.at[slot], sems.at[1, slot]))

    @pl.when(i == 0)
    def _():
        for cp in weight_copies(be_ref[0], 0):
            cp.start()

    @pl.when(jnp.logical_and(used, jnp.logical_or(i == 0, be_ref[i] != be_ref[jnp.maximum(i - 1, 0)])))
    def _():
        slot = sl_ref[i]
        for cp in weight_copies(be_ref[i], slot):
            cp.wait()

        @pl.when(nx_ref[i] >= 0)
        def _():
            for cp in weight_copies(nx_ref[i], 1 - slot):
                cp.start()

        wgu_bf[...] = wgu_f32[slot].astype(BF16)
        wd_bf[...] = wd_f32[slot].astype(BF16)

    @pl.when(used)
    def _():
        x = _load_row_tiles(xs_ref, 0, xs_ref.shape[0]).astype(BF16)
        gu = _dot(x, wgu_bf[...]) + bgu_ref[0, 0]
        gate = jnp.minimum(gu[:, :D_FF], SWIGLU_LIMIT)
        up = jnp.clip(gu[:, D_FF:], -SWIGLU_LIMIT, SWIGLU_LIMIT)
        hdn = (up + 1.0) * gate * _sigmoid(SWIGLU_ALPHA * gate)
        _store_row_tiles(ys_ref, _dot(hdn.astype(BF16), wd_bf[...]) + bd_ref[0, 0])

    @pl.when(jnp.logical_not(used))
    def _():
        ys_ref[...] = jnp.zeros(ys_ref.shape, F32)


def _experts(xs, plan, wgu, bgu, wd, bd, layer):
    blk_e, next_e, slot, n_used = plan
    n_rows = xs.shape[0]
    d = xs.shape[1] * xs.shape[2]
    nblk = n_rows // MOE_BLOCK

    def blk(i, nu):
        return jnp.minimum(i, nu[0] - 1)

    def bias(i, be, nx, sl, nu):
        return (layer, be[blk(i, nu)], 0, 0)

    grid_spec = pltpu.PrefetchScalarGridSpec(
        num_scalar_prefetch=4,
        grid=(nblk,),
        in_specs=[
            pl.BlockSpec((MOE_BLOCK,) + xs.shape[1:], lambda i, be, nx, sl, nu: (blk(i, nu), 0, 0)),
            pl.BlockSpec(memory_space=pl.ANY),
            pl.BlockSpec((1, 1, 1, 2 * D_FF), bias),
            pl.BlockSpec(memory_space=pl.ANY),
            pl.BlockSpec((1, 1, 1, d), bias),
        ],
        out_specs=pl.BlockSpec((MOE_BLOCK,) + xs.shape[1:], lambda i, be, nx, sl, nu: (i, 0, 0)),
        scratch_shapes=[pltpu.VMEM((2, d, 2 * D_FF), F32), pltpu.VMEM((2, D_FF, d), F32),
                        pltpu.VMEM((d, 2 * D_FF), BF16), pltpu.VMEM((D_FF, d), BF16),
                        pltpu.SemaphoreType.DMA((2, 2))],
    )
    return pl.pallas_call(
        functools.partial(_expert_kernel, layer=layer),
        grid_spec=grid_spec,
        out_shape=jax.ShapeDtypeStruct(xs.shape, F32),
        compiler_params=_cparams(("arbitrary",)),
        name="moe_experts",
    )(blk_e, next_e, slot, n_used, xs, wgu, bgu, wd, bd)


def _combine_kernel(dest_ref, next_ref, x1_ref, gw_ref, mod_ref, ys_ref, o_ref, buf, sems, *, n_lat, tm):
    i = pl.program_id(0)
    d = x1_ref.shape[1]
    n = dest_ref.shape[0]

    def gather(idx_ref, slot):
        def start(j, c):
            for q in range(DMA_QUEUES):
                r = DMA_QUEUES * j + q
                pltpu.make_async_copy(ys_ref.at[idx_ref[r]], buf.at[slot, r], sems.at[slot]).start(priority=q)
            return c

        lax.fori_loop(0, n // DMA_QUEUES, start, 0, unroll=8)

    @pl.when(i == 0)
    def _():
        gather(dest_ref, 0)

    @pl.when(i + 1 < pl.num_programs(0))
    def _():
        gather(next_ref, (i + 1) & 1)

    slot = i & 1
    cur = buf.at[slot]
    pltpu.make_async_copy(ys_ref.at[pl.ds(0, n)], cur, sems.at[slot]).wait()

    lat = i * tm < n_lat
    g2 = jnp.where(lat, mod_ref[0:1, 5 * d:6 * d], mod_ref[1:2, 5 * d:6 * d])
    gw = gw_ref[...]
    m = gw[:, 0:1] * _load_row_tiles(cur, 0, tm)
    for k in range(1, TOP_K):
        m = m + gw[:, k:k + 1] * _load_row_tiles(cur, k * tm, tm)
    o_ref[...] = x1_ref[...] + g2 * m


def _combine(x1, gw, mod, ys, dest_tiles, n_lat):
    t, d = x1.shape
    tm = ROW_TILE
    return pl.pallas_call(
        functools.partial(_combine_kernel, n_lat=n_lat, tm=tm),
        grid=(t // tm,),
        in_specs=[pl.BlockSpec((tm * TOP_K,), lambda i: (i,), memory_space=pltpu.SMEM),
                  pl.BlockSpec((tm * TOP_K,), lambda i: (jnp.minimum(i + 1, t // tm - 1),), memory_space=pltpu.SMEM),
                  pl.BlockSpec((tm, d), lambda i: (i, 0)),
                  pl.BlockSpec((tm, 128), lambda i: (i, 0)),
                  pl.BlockSpec(mod.shape, lambda i: (0, 0)),
                  pl.BlockSpec(memory_space=pl.ANY)],
        out_specs=pl.BlockSpec((tm, d), lambda i: (i, 0)),
        out_shape=jax.ShapeDtypeStruct((t, d), F32),
        scratch_shapes=[pltpu.VMEM((2, tm * TOP_K) + ys.shape[1:], F32), pltpu.SemaphoreType.DMA((2,))],
        compiler_params=_cparams(("arbitrary",)),
        name="moe_combine",
    )(dest_tiles, dest_tiles, x1, gw, mod, ys)


def _moe_plan(gi, rk, counts):
    t = gi.shape[1]
    top_i = gi[:TOP_K]
    rank = rk[:TOP_K]
    cnt = counts[0, :N_EXPERTS].astype(jnp.int32)
    padded = (cnt + MOE_BLOCK - 1) // MOE_BLOCK * MOE_BLOCK
    pad_ends = jnp.cumsum(padded)
    pad_starts = pad_ends - padded
    experts = jnp.arange(N_EXPERTS, dtype=jnp.int32)[:, None, None]
    dest = jnp.sum(jnp.where(top_i[None] == experts, pad_starts[:, None, None], 0), axis=0) + rank
    n_rows = t * TOP_K + N_EXPERTS * MOE_BLOCK
    blk_start = jnp.arange(n_rows // MOE_BLOCK, dtype=jnp.int32) * MOE_BLOCK
    blk_e = jnp.minimum(jnp.sum(blk_start[:, None] >= pad_ends[None, :], axis=1), N_EXPERTS - 1).astype(jnp.int32)
    n_used = (pad_ends[-1:] // MOE_BLOCK).astype(jnp.int32)
    has_rows = cnt > 0
    later = jnp.where(has_rows[None, :] & (jnp.arange(N_EXPERTS)[None, :] > jnp.arange(N_EXPERTS)[:, None]),
                      jnp.arange(N_EXPERTS)[None, :], N_EXPERTS)
    next_of = jnp.min(later, axis=1)
    next_of = jnp.where(next_of < N_EXPERTS, next_of, -1).astype(jnp.int32)
    run_of = (jnp.cumsum(has_rows.astype(jnp.int32)) - 1).astype(jnp.int32)
    next_e = next_of[blk_e]
    slot = run_of[blk_e] & 1
    dest_tiles = dest.reshape(TOP_K, t // ROW_TILE, ROW_TILE).transpose(1, 0, 2).reshape(-1).astype(jnp.int32)
    return dest_tiles, (blk_e, next_e, slot, n_used), n_rows


def _rope_tables(n_lat, n_ctx):
    pos = jnp.arange(n_lat)
    row = (pos // GRID_W).astype(F32)
    col = (pos % GRID_W).astype(F32)
    freq = ROPE_BASE ** (-jnp.arange(ROPE_PAIRS, dtype=F32) / ROPE_PAIRS)
    ra = row[:, None] * freq
    ca = col[:, None] * freq
    cos64 = jnp.concatenate([jnp.cos(ra), jnp.cos(ra), jnp.cos(ca), jnp.cos(ca)], axis=1)
    sin64 = jnp.concatenate([-jnp.sin(ra), jnp.sin(ra), -jnp.sin(ca), jnp.sin(ca)], axis=1)
    cos_t = jnp.concatenate([jnp.tile(cos64, (1, DA_W // DA_DH)), jnp.ones((n_ctx, DA_W), F32)], axis=0)
    sin_t = jnp.concatenate([jnp.tile(sin64, (1, DA_W // DA_DH)), jnp.zeros((n_ctx, DA_W), F32)], axis=0)
    return cos_t, sin_t


def kernel(x, c, ctx, c_ctx, ada_w, ada_b, norm1_g, norm2_g, w_in, qn_g, kn_g, lam_qk, subln_g, hg_lb,
           hg_norm_g, w_branch_a, w_branch_b, w_out, router_w, router_b, w_gu, b_gu, w_down, b_down):
    bsz, n_lat, d = x.shape
    n_ctx = ctx.shape[1]
    depth = ada_w.shape[0]
    assert bsz == 1 and d == D_MODEL and n_lat % 512 == 0 and n_ctx % ROW_TILE == 0
    t = n_lat + n_ctx

    xx = jnp.concatenate([x[0], ctx[0]], axis=0)
    cc = jnp.zeros((8, d), F32).at[0].set(c[0]).at[1].set(c_ctx)
    mods = _modulation(cc, ada_w, ada_b)

    cos_t, sin_t = _rope_tables(n_lat, n_ctx)
    lane = jnp.arange(DA_W)
    seg64 = (lane[:, None] // DA_DH == lane[None, :] // DA_DH).astype(BF16)
    cs = jnp.cumsum(jax.nn.softmax(hg_lb.astype(F32), axis=1), axis=1)
    lb_all = cs - cs[:, :1]
    rw_pad = jnp.zeros((depth, d, 128), F32).at[:, :, :N_EXPERTS].set(router_w)
    rb_pad = jnp.zeros((depth, 1, 128), F32).at[:, 0, :N_EXPERTS].set(router_b)

    key_chunk = next(c for c in ATT_KEY_CHUNKS if t % c == 0 and n_ctx <= c)
    xs = None
    for l in range(depth):
        lam_init = 0.8 - 0.6 * math.exp(-0.3 * l)
        mod = mods[l]
        p = _inproj(xx, norm1_g[l][None], mod, w_in[l].astype(BF16), n_lat)
        k_r, q_t, v_t3, qn2, kn2_tiles = _prep(p, cos_t, sin_t, seg64, jnp.tile(qn_g[l], DA_W // DA_DH)[None],
                                               jnp.tile(kn_g[l], DA_W // DA_DH)[None], key_chunk)
        oa = _attention_all(q_t, k_r, v_t3, qn2, kn2_tiles, lam_qk[l], subln_g[l][:, None], lam_init,
                            n_lat, n_ctx)
        o_f, o_b = _hgrn(p, lb_all[0, l][None], lb_all[1, l][None], n_lat)
        x1, h2, gw, gi, rk, counts = _post(
            xx, oa, o_f, o_b, p, mod, jnp.tile(hg_norm_g[l], HG_HEADS)[None], norm2_g[l][None],
            w_branch_a[l].astype(BF16), w_branch_b[l].astype(BF16), w_out[l].astype(BF16),
            rw_pad[l], rb_pad[l], n_lat)
        dest_tiles, plan, n_rows = _moe_plan(gi, rk, counts)
        xs = _dispatch(h2, dest_tiles, jnp.zeros((n_rows,) + h2.shape[1:], F32) if xs is None else xs)
        ys = _experts(xs, plan, w_gu, b_gu[:, :, None, :], w_down, b_down[:, :, None, :], l)
        xx = _combine(x1, gw, mod, ys, dest_tiles, n_lat)
    return xx[:n_lat][None]
```
